```python
import jax, jax.numpy as jnp
from jax import lax
import numpy as np

D_MODEL = 1024
BATCH = 16
SEQ = 2048
DEPTH = 1

CHUNK = 64
Q_BLOCK = 128
MLA_HEADS = 8
QK_NOPE_DIM = 64
QK_ROPE_DIM = 32
QK_HEAD_DIM = QK_NOPE_DIM + QK_ROPE_DIM
V_HEAD_DIM = 64
Q_LORA_RANK = 256
KV_LORA_RANK = 128
ROPE_THETA = 10000.0
CONV_WIDTH = D_MODEL // 2
CONV_KERNEL = 31
N_BRANCHES = 2
N_GROUPS = 4
EXPERTS_PER_GROUP = 8
N_EXPERTS = N_GROUPS * EXPERTS_PER_GROUP
TOP_K_IN_GROUP = 2
EXPERT_FF = 256
EPS = 1e-6

IN_SIZES = (Q_LORA_RANK, KV_LORA_RANK, QK_ROPE_DIM, 2 * CONV_WIDTH, N_BRANCHES * D_MODEL)
IN_COLS = sum(IN_SIZES)
IN_SPLITS = tuple(int(v) for v in np.cumsum(IN_SIZES)[:-1])

kernel_name = "hybrid_mla_conformer_hiermoe_block"


def rms_norm(x, g):
    xf = x.astype(jnp.float32)
    y = xf * lax.rsqrt(jnp.mean(xf * xf, axis=-1, keepdims=True) + EPS)
    return (y * g.astype(jnp.float32)).astype(x.dtype)


def layer_norm(x, g, b):
    xf = x.astype(jnp.float32)
    mu = jnp.mean(xf, axis=-1, keepdims=True)
    var = jnp.mean(jnp.square(xf - mu), axis=-1, keepdims=True)
    y = (xf - mu) * lax.rsqrt(var + EPS)
    return (y * g.astype(jnp.float32) + b.astype(jnp.float32)).astype(x.dtype)


def apply_rope(x, positions):
    half = QK_ROPE_DIM // 2
    inv_freq = ROPE_THETA ** (-jnp.arange(half, dtype=jnp.float32) / half)
    ang = positions.astype(jnp.float32)[..., None] * inv_freq
    cos = jnp.cos(ang)[:, :, None, :]
    sin = jnp.sin(ang)[:, :, None, :]
    xf = x.astype(jnp.float32)
    x1, x2 = xf[..., :half], xf[..., half:]
    out = jnp.concatenate([x1 * cos - x2 * sin, x2 * cos + x1 * sin], axis=-1)
    return out.astype(x.dtype)


def chunk_causal_attention(q, k, v):
    B, T, H, Dh = q.shape
    Dv = v.shape[-1]
    nb = T // Q_BLOCK
    qb = q.reshape(B, nb, Q_BLOCK, H, Dh).transpose(1, 0, 2, 3, 4)
    key_chunk = jnp.arange(T) // CHUNK
    scale = Dh ** -0.5

    def one_block(args):
        qi, bi = args
        s = jnp.einsum('bqhd,bkhd->bhqk', qi, k, preferred_element_type=jnp.float32) * scale
        q_chunk = (bi * Q_BLOCK + jnp.arange(Q_BLOCK)) // CHUNK
        mask = key_chunk[None, :] <= q_chunk[:, None]
        s = jnp.where(mask[None, None], s, -jnp.inf)
        p = jax.nn.softmax(s, axis=-1).astype(v.dtype)
        return jnp.einsum('bhqk,bkhd->bqhd', p, v)

    out = lax.map(one_block, (qb, jnp.arange(nb)))
    return out.transpose(1, 0, 2, 3, 4).reshape(B, T, H, Dv)


def causal_depthwise_conv(u, w, b):
    K, C = w.shape
    y = lax.conv_general_dilated(
        u, w[:, None, :].astype(u.dtype), window_strides=(1,), padding=[(K - 1, 0)],
        dimension_numbers=('NWC', 'WIO', 'NWC'), feature_group_count=C)
    return y + b.astype(u.dtype)


def hierarchical_moe(h, w_gr, b_gr, w_er, b_er, w_eg, w_eu, w_ed):
    B, T, D = h.shape
    n = h.reshape(B * T, D)
    group_logits = jnp.matmul(n, w_gr).astype(jnp.float32) + b_gr.astype(jnp.float32)
    group_prob = jax.nn.softmax(group_logits, axis=-1)
    g_idx = jnp.argmax(group_logits, axis=-1)
    g_p = jnp.take_along_axis(group_prob, g_idx[:, None], axis=-1)
    exp_logits = (jnp.matmul(n, w_er).astype(jnp.float32) + b_er.astype(jnp.float32))
    exp_logits = exp_logits.reshape(B * T, N_GROUPS, EXPERTS_PER_GROUP)
    sel = jnp.take_along_axis(exp_logits, g_idx[:, None, None], axis=1)[:, 0]
    within = jax.nn.softmax(sel, axis=-1)
    top_p, top_i = lax.top_k(within, TOP_K_IN_GROUP)
    top_p = top_p / jnp.sum(top_p, axis=-1, keepdims=True)
    expert_id = g_idx[:, None] * EXPERTS_PER_GROUP + top_i
    weights = g_p * top_p
    combine = jnp.sum(jax.nn.one_hot(expert_id, N_EXPERTS, dtype=jnp.float32) * weights[..., None], axis=1)
    combine = combine.reshape(B, T, N_EXPERTS).astype(h.dtype)

    def per_sequence(args):
        xb, cb = args
        a = jnp.einsum('td,edf->tef', xb, w_eg)
        u = jnp.einsum('td,edf->tef', xb, w_eu)
        mid = jax.nn.silu(a) * u * cb[..., None]
        return jnp.einsum('tef,efd->td', mid, w_ed)

    return lax.map(per_sequence, (h, combine))


def setup_inputs(seed: int = 0) -> dict:
    key = jax.random.key(seed)
    ks = jax.random.split(key, 24)
    f32 = jnp.float32
    L, D, H = DEPTH, D_MODEL, MLA_HEADS

    def nrm(k, shape, fan_in):
        return jax.random.normal(k, shape, f32) * (fan_in ** -0.5)

    def gain(k, shape):
        return 1.0 + 0.02 * jax.random.normal(k, shape, f32)

    def bias(k, shape, s=0.02):
        return s * jax.random.normal(k, shape, f32)

    x = jax.random.normal(ks[0], (BATCH, SEQ, D), f32)
    offset = jax.random.randint(ks[1], (BATCH, 1), 0, 4096, dtype=jnp.int32)
    positions = (offset + jnp.arange(SEQ, dtype=jnp.int32)[None, :]).astype(jnp.int32)
    return {
        "x": x,
        "positions": positions,
        "g_mix": gain(ks[2], (L, D)),
        "w_in": nrm(ks[3], (L, D, IN_COLS), D),
        "g_q_lora": gain(ks[4], (L, Q_LORA_RANK)),
        "w_uq": nrm(ks[5], (L, Q_LORA_RANK, H * QK_HEAD_DIM), Q_LORA_RANK),
        "g_kv_lora": gain(ks[6], (L, KV_LORA_RANK)),
        "w_ukv": nrm(ks[7], (L, KV_LORA_RANK, H * (QK_NOPE_DIM + V_HEAD_DIM)), KV_LORA_RANK),
        "g_qk_q": gain(ks[8], (L, QK_HEAD_DIM)),
        "g_qk_k": gain(ks[9], (L, QK_HEAD_DIM)),
        "w_o_mla": nrm(ks[10], (L, H * V_HEAD_DIM, D), H * V_HEAD_DIM),
        "w_dw": nrm(ks[11], (L, CONV_KERNEL, CONV_WIDTH), CONV_KERNEL),
        "b_dw": bias(ks[12], (L, CONV_WIDTH)),
        "g_conv_ln": gain(ks[13], (L, CONV_WIDTH)),
        "b_conv_ln": bias(ks[14], (L, CONV_WIDTH)),
        "w_conv_out": nrm(ks[15], (L, CONV_WIDTH, D), CONV_WIDTH),
        "b_gates": bias(ks[16], (L, N_BRANCHES * D)),
        "w_out": nrm(ks[17], (L, D, D), D),
        "g_ffn": gain(ks[18], (L, D)),
        "w_group_router": nrm(ks[19], (L, D, N_GROUPS), D),
        "b_group_router": bias(ks[20], (L, N_GROUPS), 0.01),
        "w_expert_router": nrm(ks[21], (L, D, N_EXPERTS), D),
        "b_expert_router": bias(ks[22], (L, N_EXPERTS), 0.01),
        "w_e_gate": nrm(jax.random.fold_in(ks[23], 0), (L, N_EXPERTS, D, EXPERT_FF), D),
        "w_e_up": nrm(jax.random.fold_in(ks[23], 1), (L, N_EXPERTS, D, EXPERT_FF), D),
        "w_e_down": nrm(jax.random.fold_in(ks[23], 2), (L, N_EXPERTS, EXPERT_FF, D), EXPERT_FF),
    }


def reference(x, positions, g_mix, w_in, g_q_lora, w_uq, g_kv_lora, w_ukv, g_qk_q, g_qk_k,
              w_o_mla, w_dw, b_dw, g_conv_ln, b_conv_ln, w_conv_out, b_gates, w_out,
              g_ffn, w_group_router, b_group_router, w_expert_router, b_expert_router,
              w_e_gate, w_e_up, w_e_down):
    B, T, D = x.shape
    H = MLA_HEADS
    for l in range(DEPTH):
        h = rms_norm(x, g_mix[l])
        proj = jnp.matmul(h, w_in[l])
        q_down, kv_down, k_rope, conv_in, gate_logits = jnp.split(proj, IN_SPLITS, axis=-1)

        q = jnp.matmul(rms_norm(q_down, g_q_lora[l]), w_uq[l]).reshape(B, T, H, QK_HEAD_DIM)
        kv = jnp.matmul(rms_norm(kv_down, g_kv_lora[l]), w_ukv[l]).reshape(B, T, H, QK_NOPE_DIM + V_HEAD_DIM)
        k_nope, v = kv[..., :QK_NOPE_DIM], kv[..., QK_NOPE_DIM:]
        k_r = jnp.broadcast_to(k_rope[:, :, None, :], (B, T, H, QK_ROPE_DIM))
        k = jnp.concatenate([k_nope, k_r], axis=-1)
        q = rms_norm(q, g_qk_q[l])
        k = rms_norm(k, g_qk_k[l])
        q = jnp.concatenate([q[..., :QK_NOPE_DIM], apply_rope(q[..., QK_NOPE_DIM:], positions)], axis=-1)
        k = jnp.concatenate([k[..., :QK_NOPE_DIM], apply_rope(k[..., QK_NOPE_DIM:], positions)], axis=-1)
        attn = chunk_causal_attention(q, k, v).reshape(B, T, H * V_HEAD_DIM)
        y_a = jnp.matmul(attn, w_o_mla[l])

        c_val, c_gate = conv_in[..., :CONV_WIDTH], conv_in[..., CONV_WIDTH:]
        u = c_val * jax.nn.sigmoid(c_gate)
        u = causal_depthwise_conv(u, w_dw[l], b_dw[l])
        u = jax.nn.silu(layer_norm(u, g_conv_ln[l], b_conv_ln[l]))
        y_b = jnp.matmul(u, w_conv_out[l])

        gates = jax.nn.sigmoid(gate_logits + b_gates[l].astype(gate_logits.dtype))
        merged = gates[..., :D] * y_a + gates[..., D:] * y_b
        x = x + jnp.matmul(merged, w_out[l])

        h2 = rms_norm(x, g_ffn[l])
        x = x + hierarchical_moe(h2, w_group_router[l], b_group_router[l], w_expert_router[l],
                                 b_expert_router[l], w_e_gate[l], w_e_up[l], w_e_down[l])
    return x
```

```python
import functools

import jax
import jax.numpy as jnp
import numpy as np
from jax import lax
from jax.experimental import pallas as pl
from jax.experimental.pallas import tpu as pltpu

D_MODEL = 1024
N_HEADS = 8
NOPE = 64
ROPE = 32
ROPE_HALF = ROPE // 2
QK_DIM = NOPE + ROPE
V_DIM = 64
Q_LORA = 256
KV_LORA = 128
CONV_W = 512
CONV_K = 31
N_GROUPS = 4
EXPERTS_PER_GROUP = 8
N_EXPERTS = N_GROUPS * EXPERTS_PER_GROUP
EXPERT_FF = 256
EPS = 1e-6
ROPE_THETA = 10000.0
CHUNK = 64

LANES = 128
SUBLANES = 8
HEAD_SLOT = LANES
VMEM_LIMIT_BYTES = 56 * 1024 * 1024

TOKEN_BLOCK = 512
ATTN_TILE = 256
CONV_HALO = 32
ROW_TILE = 128
COMBINE_TILE = 256
RANK_CHUNK = 256

F32 = jnp.float32
BF16 = jnp.bfloat16
NEG_INF = float("-inf")


def _dot(a, b, **kw):
    return jnp.dot(a, b, preferred_element_type=F32, **kw)


def _rms(x, g):
    return x * lax.rsqrt(jnp.mean(x * x, axis=-1, keepdims=True) + EPS) * g


def _sigmoid(x):
    return 1.0 / (1.0 + jnp.exp(-x))


def _params(*sem):
    return pltpu.CompilerParams(dimension_semantics=sem, vmem_limit_bytes=VMEM_LIMIT_BYTES)


def _full(shape):
    nd = len(shape)
    return pl.BlockSpec(shape, lambda *_: (0,) * nd)


def _rope_table_kernel(pos_ref, freq_ref, cos_ref, sin_ref):
    ang = pos_ref[...].astype(F32) * freq_ref[...]
    cos_ref[...] = jnp.cos(ang)
    sin_ref[...] = jnp.sin(ang)


def _rope_tables(positions):
    n = positions.size
    per_row = LANES // ROPE_HALF
    rows = n // per_row
    pos_rep = jnp.repeat(positions.reshape(rows, per_row), ROPE_HALF, axis=1)
    inv_freq = ROPE_THETA ** (-jnp.arange(ROPE_HALF, dtype=F32) / ROPE_HALF)
    freq = jnp.tile(inv_freq, per_row).reshape(1, LANES)
    rb = min(rows, 512)
    cos, sin = pl.pallas_call(
        _rope_table_kernel,
        grid=(rows // rb,),
        in_specs=[pl.BlockSpec((rb, LANES), lambda i: (i, 0)), _full((1, LANES))],
        out_specs=[pl.BlockSpec((rb, LANES), lambda i: (i, 0))] * 2,
        out_shape=[jax.ShapeDtypeStruct((rows, LANES), F32)] * 2,
        compiler_params=_params("parallel"),
        name="rope_tables",
    )(pos_rep, freq)
    return cos.reshape(n, ROPE_HALF), sin.reshape(n, ROPE_HALF)


def _pre_attention_kernel(x_ref, cos_ref, sin_ref, gmix_ref, wa_ref, wc_ref, wg_ref, bg_ref,
                          gql_ref, wuq_ref, gkv_ref, wk_ref, wv_ref, rrep_ref, gq_ref, gk_ref,
                          q_out, k_out, v_out, u_out, gate_out):
    tm = x_ref.shape[0]
    hb = _rms(x_ref[...], gmix_ref[...]).astype(BF16)

    a = _dot(hb, wa_ref[...])
    qdn = _rms(a[:, :Q_LORA], gql_ref[...]).astype(BF16)
    kvn = _rms(a[:, Q_LORA:Q_LORA + KV_LORA], gkv_ref[...]).astype(BF16)
    kr = a[:, Q_LORA + KV_LORA:].astype(BF16)
    qf = _dot(qdn, wuq_ref[...])
    kf = _dot(kvn, wk_ref[...]) + _dot(kr, rrep_ref[...])
    v_out[...] = _dot(kvn, wv_ref[...]).astype(BF16)

    cos = cos_ref[...]
    sin = sin_ref[...]
    z = lambda w: jnp.zeros((tm, w), F32)
    c_tab = jnp.concatenate([jnp.ones((tm, NOPE), F32), cos, cos, z(HEAD_SLOT - QK_DIM)], axis=-1)
    s_up = jnp.concatenate([z(NOPE + ROPE_HALF), sin, z(HEAD_SLOT - QK_DIM)], axis=-1)
    s_dn = jnp.concatenate([z(NOPE), -sin, z(HEAD_SLOT - NOPE - ROPE_HALF)], axis=-1)

    def norm_rope(f, g_ref, out, scale):
        ta = c_tab * g_ref[0:1, :]
        tu = s_up * g_ref[1:2, :]
        td = s_dn * g_ref[2:3, :]
        for h in range(N_HEADS):
            xs = f[:, h * HEAD_SLOT:(h + 1) * HEAD_SLOT]
            ss = jnp.sum(xs * xs, axis=-1, keepdims=True)
            s = lax.rsqrt(ss * (1.0 / QK_DIM) + EPS) * scale
            y = xs * ta + pltpu.roll(xs, ROPE_HALF, 1) * tu + pltpu.roll(xs, HEAD_SLOT - ROPE_HALF, 1) * td
            out[:, h * HEAD_SLOT:(h + 1) * HEAD_SLOT] = (y * s).astype(BF16)

    norm_rope(qf, gq_ref, q_out, QK_DIM ** -0.5)
    norm_rope(kf, gk_ref, k_out, 1.0)

    c = _dot(hb, wc_ref[...])
    u_out[...] = c[:, :CONV_W] * _sigmoid(c[:, CONV_W:])
    gate_out[...] = _sigmoid(_dot(hb, wg_ref[...]) + bg_ref[...]).astype(BF16)


def _pre_attention(x2, cos, sin, w, tm):
    n = x2.shape[0]
    row = lambda width: pl.BlockSpec((tm, width), lambda i: (i, 0))
    hs = N_HEADS * HEAD_SLOT
    weights = [w["g_mix"], w["wa"], w["wc"], w["wg"], w["b_gates"], w["g_q_lora"], w["wuq"],
               w["g_kv_lora"], w["wk"], w["wv"], w["rrep"], w["gq3"], w["gk3"]]
    return pl.pallas_call(
        _pre_attention_kernel,
        grid=(n // tm,),
        in_specs=[row(D_MODEL), row(ROPE_HALF), row(ROPE_HALF)] + [_full(a.shape) for a in weights],
        out_specs=[row(hs), row(hs), row(hs), row(CONV_W), row(2 * D_MODEL)],
        out_shape=[jax.ShapeDtypeStruct((n, hs), BF16), jax.ShapeDtypeStruct((n, hs), BF16),
                   jax.ShapeDtypeStruct((n, hs), BF16), jax.ShapeDtypeStruct((n, CONV_W), F32),
                   jax.ShapeDtypeStruct((n, 2 * D_MODEL), BF16)],
        compiler_params=_params("parallel"),
        name="pre_attention",
    )(x2, cos, sin, *weights)


def _attention_kernel(q_ref, k_ref, v_ref, o_ref, m_sc, l_sc, acc_sc):
    i = pl.program_id(2)
    t = ATTN_TILE
    r = lax.broadcasted_iota(jnp.int32, (t, t), 0) // CHUNK
    c = lax.broadcasted_iota(jnp.int32, (t, t), 1) // CHUNK
    diag_mask = c <= r
    for h in range(2):
        m_sc[h] = jnp.full((t, 1), NEG_INF, F32)
        l_sc[h] = jnp.zeros((t, 1), F32)
        acc_sc[h] = jnp.zeros((t, HEAD_SLOT), F32)

    def block(j, masked):
        start = pl.multiple_of(j * t, t)
        for h in range(2):
            sl = slice(h * HEAD_SLOT, (h + 1) * HEAD_SLOT)
            q = q_ref[:, sl]
            kt = k_ref[pl.ds(start, t), sl]
            s = lax.dot_general(q, kt, (((1,), (1,)), ((), ())), preferred_element_type=F32)
            if masked:
                s = jnp.where(diag_mask, s, NEG_INF)
            m_old = m_sc[h]
            m_new = jnp.maximum(m_old, jnp.max(s, axis=-1, keepdims=True))
            alpha = jnp.exp(m_old - m_new)
            p = jnp.exp(s - m_new)
            l_sc[h] = alpha * l_sc[h] + jnp.sum(p, axis=-1, keepdims=True)
            acc_sc[h] = alpha * acc_sc[h] + _dot(p.astype(BF16), v_ref[pl.ds(start, t), sl])
            m_sc[h] = m_new

    def body(j, carry):
        block(j, False)
        return carry

    lax.fori_loop(0, i, body, 0)
    block(i, True)
    o = acc_sc[0] / l_sc[0] + acc_sc[1] / l_sc[1]
    o_ref[...] = o.astype(BF16)


def _attention(q, k, v, b, t):
    hs2 = 2 * HEAD_SLOT
    n = b * t
    nq = t // ATTN_TILE
    qspec = pl.BlockSpec((ATTN_TILE, hs2), lambda bi, p, i: (bi * nq + i, p))
    kvspec = pl.BlockSpec((t, hs2), lambda bi, p, i: (bi, p))
    return pl.pallas_call(
        _attention_kernel,
        grid=(b, N_HEADS // 2, nq),
        in_specs=[qspec, kvspec, kvspec],
        out_specs=pl.BlockSpec((ATTN_TILE, HEAD_SLOT), lambda bi, p, i: (bi * nq + i, p)),
        out_shape=jax.ShapeDtypeStruct((n, N_HEADS * V_DIM), BF16),
        scratch_shapes=[pltpu.VMEM((2, ATTN_TILE, 1), F32), pltpu.VMEM((2, ATTN_TILE, 1), F32),
                        pltpu.VMEM((2, ATTN_TILE, HEAD_SLOT), F32)],
        compiler_params=_params("parallel", "parallel", "arbitrary"),
        name="attention",
    )(q, k, v)


def _merge_kernel(blocks_per_seq, attn_ref, u_ref, halo_ref, gate_ref, x_ref, wo_ref, wdw_ref, bdw_ref,
                  gln_ref, bln_ref, wco_ref, wout_ref, gffn_ref, wr_ref, br_ref,
                  x1_out, h2p_out, route_out, ubuf):
    tm = x_ref.shape[0]
    first = (pl.program_id(0) % blocks_per_seq) == 0
    halo = halo_ref[...]
    ubuf[0:CONV_HALO, :] = jnp.where(first, jnp.zeros_like(halo), halo)
    ubuf[CONV_HALO:, :] = u_ref[...]

    base = CONV_HALO - (CONV_K - 1)
    conv = jnp.zeros((tm, CONV_W), F32) + bdw_ref[...]
    for j in range(CONV_K):
        conv = conv + ubuf[base + j:base + j + tm, :] * wdw_ref[j:j + 1, :]
    mu = jnp.mean(conv, axis=-1, keepdims=True)
    cen = conv - mu
    var = jnp.mean(cen * cen, axis=-1, keepdims=True)
    ln = cen * lax.rsqrt(var + EPS) * gln_ref[...] + bln_ref[...]
    ub = (ln * _sigmoid(ln)).astype(BF16)
    y_b = _dot(ub, wco_ref[...])
    y_a = _dot(attn_ref[...], wo_ref[...])
    merged = gate_ref[:, :D_MODEL].astype(F32) * y_a + gate_ref[:, D_MODEL:].astype(F32) * y_b
    x1 = x_ref[...] + _dot(merged.astype(BF16), wout_ref[...])
    x1_out[...] = x1

    h2 = _rms(x1, gffn_ref[...])
    half = D_MODEL // 2
    lo = lax.bitcast_convert_type(h2[:, :half].astype(BF16).astype(F32), jnp.uint32)
    hi = lax.bitcast_convert_type(h2[:, half:].astype(BF16).astype(F32), jnp.uint32)
    h2p_out[...] = (lo >> 16) | hi

    logits = _dot(h2, wr_ref[...], precision=lax.Precision.HIGHEST) + br_ref[...]
    lane_i = lax.broadcasted_iota(jnp.int32, logits.shape, 1)
    lane = lane_i.astype(F32)
    big = float(LANES)
    is_g = (lane_i >= N_EXPERTS) & (lane_i < N_EXPERTS + N_GROUPS)
    gl = jnp.where(is_g, logits, NEG_INF)
    gmax = jnp.max(gl, axis=-1, keepdims=True)
    gidx = jnp.min(jnp.where(gl == gmax, lane - N_EXPERTS, big), axis=-1, keepdims=True)
    g_p = 1.0 / jnp.sum(jnp.exp(gl - gmax), axis=-1, keepdims=True)
    lane_group = (lane_i // EXPERTS_PER_GROUP).astype(F32)
    valid = (lane_i < N_EXPERTS) & (lane_group == gidx)
    el = jnp.where(valid, logits, NEG_INF)
    ee = jnp.exp(el - jnp.max(el, axis=-1, keepdims=True))
    within = jnp.where(valid, ee / jnp.sum(ee, axis=-1, keepdims=True), -1.0)
    p1 = jnp.max(within, axis=-1, keepdims=True)
    i1 = jnp.min(jnp.where(within == p1, lane, big), axis=-1, keepdims=True)
    within2 = jnp.where(lane == i1, -1.0, within)
    p2 = jnp.max(within2, axis=-1, keepdims=True)
    i2 = jnp.min(jnp.where(within2 == p2, lane, big), axis=-1, keepdims=True)
    psum = p1 + p2
    w1 = g_p * (p1 / psum)
    w2 = g_p * (p2 / psum)
    route = jnp.where(lane_i == 0, i1, 0.0)
    route = jnp.where(lane_i == 1, i2, route)
    route = jnp.where(lane_i == 2, w1, route)
    route = jnp.where(lane_i == 3, w2, route)
    route_out[...] = route


def _merge(attn, u, gates, x2, w, t, tm):
    n = x2.shape[0]
    bps = t // tm
    row = lambda width: pl.BlockSpec((tm, width), lambda i: (i, 0))
    hpb = tm // CONV_HALO
    halo = pl.BlockSpec((CONV_HALO, CONV_W), lambda i: (jnp.maximum(i * hpb - 1, 0), 0))
    weights = [w["wo"], w["w_dw"], w["b_dw"], w["g_conv_ln"], w["b_conv_ln"], w["wco"], w["wout"],
               w["g_ffn"], w["wr"], w["br"]]
    return pl.pallas_call(
        functools.partial(_merge_kernel, bps),
        grid=(n // tm,),
        in_specs=[row(N_HEADS * V_DIM), row(CONV_W), halo, row(2 * D_MODEL), row(D_MODEL)]
        + [_full(a.shape) for a in weights],
        out_specs=[row(D_MODEL), row(D_MODEL // 2), row(LANES)],
        out_shape=[jax.ShapeDtypeStruct((n, D_MODEL), F32), jax.ShapeDtypeStruct((n, D_MODEL // 2), jnp.uint32),
                   jax.ShapeDtypeStruct((n, LANES), F32)],
        scratch_shapes=[pltpu.VMEM((tm + CONV_HALO, CONV_W), F32)],
        compiler_params=_params("parallel"),
        name="merge_router",
    )(attn, u, u, gates, x2, *weights)


def _rank_kernel(route_ref, pos_out, meta_out, r1_sc, r2_sc):
    tb = route_ref.shape[0]
    lane = lax.broadcasted_iota(jnp.int32, (RANK_CHUNK, LANES), 1).astype(F32)
    rr = lax.broadcasted_iota(jnp.int32, (RANK_CHUNK, RANK_CHUNK), 0)
    cc = lax.broadcasted_iota(jnp.int32, (RANK_CHUNK, RANK_CHUNK), 1)
    lower = jnp.where(rr > cc, 1.0, 0.0).astype(BF16)
    run = jnp.zeros((1, LANES), F32)
    for choice, sc in ((0, r1_sc), (1, r2_sc)):
        for ci in range(tb // RANK_CHUNK):
            rows = slice(ci * RANK_CHUNK, (ci + 1) * RANK_CHUNK)
            oh = jnp.where(lane == route_ref[rows, choice:choice + 1], 1.0, 0.0)
            sc[rows, :] = run + _dot(lower, oh.astype(BF16))
            run = run + jnp.sum(oh, axis=0, keepdims=True)
    cnt = run
    cnt8 = jnp.ceil(cnt * (1.0 / SUBLANES)) * SUBLANES
    ur = lax.broadcasted_iota(jnp.int32, (LANES, LANES), 0)
    uc = lax.broadcasted_iota(jnp.int32, (LANES, LANES), 1)
    upper = jnp.where(ur < uc, 1.0, 0.0)
    off = _dot(jnp.broadcast_to(cnt8, (SUBLANES, LANES)), upper, precision=lax.Precision.HIGHEST)[0:1, :]
    lane_t = lax.broadcasted_iota(jnp.int32, (tb, LANES), 1)
    lane_tf = lane_t.astype(F32)
    p1 = jnp.sum(jnp.where(lane_tf == route_ref[:, 0:1], off + r1_sc[...], 0.0), axis=-1, keepdims=True)
    p2 = jnp.sum(jnp.where(lane_tf == route_ref[:, 1:2], off + r2_sc[...], 0.0), axis=-1, keepdims=True)
    pos = jnp.where(lane_t == 0, p1, jnp.where(lane_t == 1, p2, 0.0))
    pos_out[...] = pos.astype(jnp.int32)
    row8 = lax.broadcasted_iota(jnp.int32, (SUBLANES, LANES), 0)
    meta = jnp.where(row8 == 0, jnp.broadcast_to(off, (SUBLANES, LANES)),
                     jnp.where(row8 == 1, jnp.broadcast_to(cnt, (SUBLANES, LANES)), 0.0))
    meta_out[...] = meta.astype(jnp.int32)


def _rank(route, b, t):
    return pl.pallas_call(
        _rank_kernel,
        grid=(b,),
        in_specs=[pl.BlockSpec((t, LANES), lambda i: (i, 0))],
        out_specs=[pl.BlockSpec((t, LANES), lambda i: (i, 0)), pl.BlockSpec((SUBLANES, LANES), lambda i: (i, 0))],
        out_shape=[jax.ShapeDtypeStruct((b * t, LANES), jnp.int32),
                   jax.ShapeDtypeStruct((b * SUBLANES, LANES), jnp.int32)],
        scratch_shapes=[pltpu.VMEM((t, LANES), F32), pltpu.VMEM((t, LANES), F32)],
        compiler_params=_params("parallel"),
        name="rank",
    )(route)


def _sorted_rows(t):
    return 2 * t + N_EXPERTS * SUBLANES + ROW_TILE


def _dispatch_kernel(pos_ref, h2p_ref, xs_out):
    t = h2p_ref.shape[0]
    xs_out[...] = jnp.zeros(xs_out.shape, xs_out.dtype)

    def body(tok, carry):
        row = h2p_ref[pl.ds(tok, 1), :]
        xs_out[0, pl.ds(pos_ref[2 * tok], 1), :] = row
        xs_out[0, pl.ds(pos_ref[2 * tok + 1], 1), :] = row
        return carry

    lax.fori_loop(0, t, body, 0, unroll=8)


def _dispatch(pos_flat, h2p, b, t):
    rows = _sorted_rows(t)
    half = D_MODEL // 2
    return pl.pallas_call(
        _dispatch_kernel,
        grid=(b,),
        in_specs=[pl.BlockSpec((2 * t,), lambda i: (i,), memory_space=pltpu.SMEM),
                  pl.BlockSpec((t, half), lambda i: (i, 0))],
        out_specs=pl.BlockSpec((1, rows, half), lambda i: (i, 0, 0)),
        out_shape=jax.ShapeDtypeStruct((b, rows, half), jnp.uint32),
        compiler_params=_params("parallel"),
        name="dispatch",
    )(pos_flat, h2p)


def _unpack_words(wd):
    lo = lax.bitcast_convert_type(wd << 16, F32)
    hi = lax.bitcast_convert_type(wd & jnp.uint32(0xFFFF0000), F32)
    return lo, hi


def _pack_words(lo, hi):
    lo_b = lax.bitcast_convert_type(lo.astype(BF16).astype(F32), jnp.uint32)
    hi_b = lax.bitcast_convert_type(hi.astype(BF16).astype(F32), jnp.uint32)
    return (lo_b >> 16) | hi_b


def _expert_kernel(off_ref, cnt_ref, xs_ref, wgl_ref, wgh_ref, wul_ref, wuh_ref, wd_ref, ys_out):
    bi = pl.program_id(0)
    e = pl.program_id(1)

    @pl.when(e == 0)
    def _():
        ys_out[...] = jnp.zeros(ys_out.shape, ys_out.dtype)

    off = off_ref[bi * N_EXPERTS + e]
    cnt = cnt_ref[bi * N_EXPERTS + e]
    n_tiles = (cnt + ROW_TILE - 1) // ROW_TILE
    half = D_MODEL // 2

    def tile(i, carry):
        start = pl.multiple_of(off + i * ROW_TILE, SUBLANES)
        lo, hi = _unpack_words(xs_ref[0, pl.ds(start, ROW_TILE), :])
        lo = lo.astype(BF16)
        hi = hi.astype(BF16)
        a = _dot(lo, wgl_ref[0]) + _dot(hi, wgh_ref[0])
        u = _dot(lo, wul_ref[0]) + _dot(hi, wuh_ref[0])
        mid = (a * _sigmoid(a) * u).astype(BF16)
        y = _dot(mid, wd_ref[0])
        ys_out[0, pl.ds(start, ROW_TILE), :] = _pack_words(y[:, :half], y[:, half:])
        return carry

    lax.fori_loop(0, n_tiles, tile, 0)


def _expert_ffn(off, cnt, xs, w, b, t):
    rows = _sorted_rows(t)
    half = D_MODEL // 2
    wspec = lambda shape: pl.BlockSpec((1,) + shape, lambda bi, e, *_: (e, 0, 0))
    grid_spec = pltpu.PrefetchScalarGridSpec(
        num_scalar_prefetch=2,
        grid=(b, N_EXPERTS),
        in_specs=[pl.BlockSpec((1, rows, half), lambda bi, e, *_: (bi, 0, 0)),
                  wspec((half, EXPERT_FF)), wspec((half, EXPERT_FF)), wspec((half, EXPERT_FF)),
                  wspec((half, EXPERT_FF)), wspec((EXPERT_FF, D_MODEL))],
        out_specs=pl.BlockSpec((1, rows, half), lambda bi, e, *_: (bi, 0, 0)),
    )
    return pl.pallas_call(
        _expert_kernel,
        grid_spec=grid_spec,
        out_shape=jax.ShapeDtypeStruct((b, rows, half), jnp.uint32),
        compiler_params=_params("parallel", "arbitrary"),
        name="expert_ffn",
    )(off, cnt, xs, w["wg_lo"], w["wg_hi"], w["wu_lo"], w["wu_hi"], w["wd"])


def _combine_kernel(pos_ref, ys_ref, x1_ref, route_ref, out_ref, g1, g2):
    tc = x1_ref.shape[0]
    base = pl.program_id(1) * tc

    def body(r, carry):
        tok = base + r
        g1[pl.ds(r, 1), :] = ys_ref[0, pl.ds(pos_ref[2 * tok], 1), :]
        g2[pl.ds(r, 1), :] = ys_ref[0, pl.ds(pos_ref[2 * tok + 1], 1), :]
        return carry

    lax.fori_loop(0, tc, body, 0, unroll=8)
    w1 = route_ref[:, 2:3]
    w2 = route_ref[:, 3:4]
    lo1, hi1 = _unpack_words(g1[...])
    lo2, hi2 = _unpack_words(g2[...])
    half = D_MODEL // 2
    out_ref[:, :half] = x1_ref[:, :half] + (w1 * lo1 + w2 * lo2)
    out_ref[:, half:] = x1_ref[:, half:] + (w1 * hi1 + w2 * hi2)


def _combine(pos_flat, ys, x1, route, b, t):
    rows = _sorted_rows(t)
    half = D_MODEL // 2
    tc = COMBINE_TILE
    nt = t // tc
    return pl.pallas_call(
        _combine_kernel,
        grid=(b, nt),
        in_specs=[pl.BlockSpec((2 * t,), lambda bi, j: (bi,), memory_space=pltpu.SMEM),
                  pl.BlockSpec((1, rows, half), lambda bi, j: (bi, 0, 0)),
                  pl.BlockSpec((tc, D_MODEL), lambda bi, j: (bi * nt + j, 0)),
                  pl.BlockSpec((tc, LANES), lambda bi, j: (bi * nt + j, 0))],
        out_specs=pl.BlockSpec((tc, D_MODEL), lambda bi, j: (bi * nt + j, 0)),
        out_shape=jax.ShapeDtypeStruct((b * t, D_MODEL), F32),
        scratch_shapes=[pltpu.VMEM((tc, half), jnp.uint32), pltpu.VMEM((tc, half), jnp.uint32)],
        compiler_params=_params("parallel", "arbitrary"),
        name="combine",
    )(pos_flat, ys, x1, route)


def _head_slots(wmat, width):
    k = wmat.shape[0]
    w3 = wmat.reshape(k, N_HEADS, width)
    return jnp.pad(w3, ((0, 0), (0, 0), (0, HEAD_SLOT - width))).reshape(k, N_HEADS * HEAD_SLOT)


def _gain3(g):
    gp = jnp.pad(g.astype(F32), (0, HEAD_SLOT - QK_DIM))
    return jnp.stack([gp, jnp.roll(gp, ROPE_HALF), jnp.roll(gp, -ROPE_HALF)])


def _prepare_weights(g_mix, w_in, g_q_lora, w_uq, g_kv_lora, w_ukv, g_qk_q, g_qk_k, w_o_mla, w_dw, b_dw,
                     g_conv_ln, b_conv_ln, w_conv_out, b_gates, w_out, g_ffn, w_group_router, b_group_router,
                     w_expert_router, b_expert_router, w_e_gate, w_e_up, w_e_down):
    row = lambda a: a.astype(F32).reshape(1, -1)
    s0, s1, s2, s3 = Q_LORA, Q_LORA + KV_LORA, Q_LORA + KV_LORA + ROPE, Q_LORA + KV_LORA + ROPE + 2 * CONV_W
    w = {}
    w["g_mix"] = row(g_mix)
    w["wa"] = jnp.concatenate([w_in[:, :s1], jnp.pad(w_in[:, s1:s2], ((0, 0), (0, LANES - ROPE)))], axis=1).astype(BF16)
    w["wc"] = w_in[:, s2:s3].astype(BF16)
    w["wg"] = w_in[:, s3:].astype(BF16)
    w["b_gates"] = row(b_gates)
    w["g_q_lora"] = row(g_q_lora)
    w["wuq"] = _head_slots(w_uq, QK_DIM).astype(BF16)
    w["g_kv_lora"] = row(g_kv_lora)
    kv3 = w_ukv.reshape(KV_LORA, N_HEADS, NOPE + V_DIM)
    w["wk"] = _head_slots(kv3[:, :, :NOPE].reshape(KV_LORA, N_HEADS * NOPE), NOPE).astype(BF16)
    v3 = kv3[:, :, NOPE:]
    zeros = jnp.zeros_like(v3)
    odd = (jnp.arange(N_HEADS) % 2 == 1)[None, :, None]
    wv = jnp.concatenate([jnp.where(odd, zeros, v3), jnp.where(odd, v3, zeros)], axis=-1)
    w["wv"] = wv.reshape(KV_LORA, N_HEADS * HEAD_SLOT).astype(BF16)
    src = jnp.arange(LANES)[:, None]
    dst = jnp.arange(N_HEADS * HEAD_SLOT)[None, :]
    w["rrep"] = ((src < ROPE) & ((dst % HEAD_SLOT) == (src + NOPE))).astype(BF16)
    w["gq3"] = _gain3(g_qk_q)
    w["gk3"] = _gain3(g_qk_k)
    w["wo"] = w_o_mla.astype(BF16)
    w["w_dw"] = jnp.pad(w_dw.astype(F32), ((0, 32 - CONV_K), (0, 0)))
    w["b_dw"] = row(b_dw)
    w["g_conv_ln"] = row(g_conv_ln)
    w["b_conv_ln"] = row(b_conv_ln)
    w["wco"] = w_conv_out.astype(BF16)
    w["wout"] = w_out.astype(BF16)
    w["g_ffn"] = row(g_ffn)
    pad_r = LANES - N_EXPERTS - N_GROUPS
    w["wr"] = jnp.concatenate([w_expert_router, w_group_router, jnp.zeros((D_MODEL, pad_r), F32)], axis=1).astype(F32)
    w["br"] = jnp.concatenate([b_expert_router, b_group_router, jnp.zeros((pad_r,), F32)]).astype(F32).reshape(1, LANES)
    half = D_MODEL // 2
    w["wg_lo"] = w_e_gate[:, :half].astype(BF16)
    w["wg_hi"] = w_e_gate[:, half:].astype(BF16)
    w["wu_lo"] = w_e_up[:, :half].astype(BF16)
    w["wu_hi"] = w_e_up[:, half:].astype(BF16)
    w["wd"] = w_e_down.astype(BF16)
    return w


def kernel(x, positions, g_mix, w_in, g_q_lora, w_uq, g_kv_lora, w_ukv, g_qk_q, g_qk_k, w_o_mla, w_dw, b_dw,
           g_conv_ln, b_conv_ln, w_conv_out, b_gates, w_out, g_ffn, w_group_router, b_group_router,
           w_expert_router, b_expert_router, w_e_gate, w_e_up, w_e_down):
    b, t, d = x.shape
    assert d == D_MODEL and g_mix.shape[0] == 1
    assert t % ATTN_TILE == 0 and t % RANK_CHUNK == 0 and t % COMBINE_TILE == 0
    tm = min(TOKEN_BLOCK, t)
    assert t % tm == 0 and tm % CONV_HALO == 0
    n = b * t
    w = _prepare_weights(g_mix[0], w_in[0], g_q_lora[0], w_uq[0], g_kv_lora[0], w_ukv[0], g_qk_q[0], g_qk_k[0],
                         w_o_mla[0], w_dw[0], b_dw[0], g_conv_ln[0], b_conv_ln[0], w_conv_out[0], b_gates[0],
                         w_out[0], g_ffn[0], w_group_router[0], b_group_router[0], w_expert_router[0],
                         b_expert_router[0], w_e_gate[0], w_e_up[0], w_e_down[0])
    x2 = x.reshape(n, d)
    cos, sin = _rope_tables(positions)
    q, k, v, u, gates = _pre_attention(x2, cos, sin, w, tm)
    attn = _attention(q, k, v, b, t)
    x1, h2p, route = _merge(attn, u, gates, x2, w, t, tm)
    pos, meta = _rank(route, b, t)
    pos_flat = pos[:, :2].reshape(n * 2)
    meta3 = meta.reshape(b, SUBLANES, LANES)
    off = meta3[:, 0, :N_EXPERTS].reshape(b * N_EXPERTS)
    cnt = meta3[:, 1, :N_EXPERTS].reshape(b * N_EXPERTS)
    xs = _dispatch(pos_flat, h2p, b, t)
    ys = _expert_ffn(off, cnt, xs, w, b, t)
    out = _combine(pos_flat, ys, x1, route, b, t)
    return out.reshape(b, t, d)
```

```python
import functools

import jax
import jax.numpy as jnp
import numpy as np
from jax import lax
from jax.experimental import pallas as pl
from jax.experimental.pallas import tpu as pltpu

D_MODEL = 1024
N_HEADS = 8
NOPE = 64
ROPE = 32
ROPE_HALF = ROPE // 2
QK_DIM = NOPE + ROPE
V_DIM = 64
Q_LORA = 256
KV_LORA = 128
CONV_W = 512
CONV_K = 31
N_GROUPS = 4
EXPERTS_PER_GROUP = 8
N_EXPERTS = N_GROUPS * EXPERTS_PER_GROUP
EXPERT_FF = 256
EPS = 1e-6
ROPE_THETA = 10000.0
CHUNK = 64

LANES = 128
SUBLANES = 8
HEAD_SLOT = LANES
VMEM_LIMIT_BYTES = 56 * 1024 * 1024

TOKEN_BLOCK = 512
ATTN_TILE = 256
CONV_HALO = 32
ROW_TILE = 128
COMBINE_TILE = 256
RANK_CHUNK = 256
SUB_ROWS = 128
CONV_CHUNK = 32

F32 = jnp.float32
BF16 = jnp.bfloat16
NEG_INF = float("-inf")
LOG2E = 1.4426950408889634


def _dot(a, b, **kw):
    return jnp.dot(a, b, preferred_element_type=F32, **kw)


def _rms(x, g):
    return x * lax.rsqrt(jnp.mean(x * x, axis=-1, keepdims=True) + EPS) * g


def _sigmoid(x):
    return 1.0 / (1.0 + jnp.exp(-x))


def _pack_words(lo, hi):
    return pltpu.pack_elementwise([lo, hi], packed_dtype=BF16)


def _unpack_words(wd):
    lo = pltpu.unpack_elementwise(wd, index=0, packed_dtype=BF16, unpacked_dtype=F32)
    hi = pltpu.unpack_elementwise(wd, index=1, packed_dtype=BF16, unpacked_dtype=F32)
    return lo, hi


def _params(*sem):
    return pltpu.CompilerParams(dimension_semantics=sem, vmem_limit_bytes=VMEM_LIMIT_BYTES)


def _full(shape):
    nd = len(shape)
    return pl.BlockSpec(shape, lambda *_: (0,) * nd)


def _rope_table_kernel(pos_ref, freq_ref, cos_ref, sin_ref):
    ang = pos_ref[...].astype(F32) * freq_ref[...]
    cos_ref[...] = jnp.cos(ang)
    sin_ref[...] = jnp.sin(ang)


def _rope_tables(positions):
    n = positions.size
    per_row = LANES // ROPE_HALF
    rows = n // per_row
    pos_rep = jnp.repeat(positions.reshape(rows, per_row), ROPE_HALF, axis=1)
    inv_freq = ROPE_THETA ** (-jnp.arange(ROPE_HALF, dtype=F32) / ROPE_HALF)
    freq = jnp.tile(inv_freq, per_row).reshape(1, LANES)
    rb = min(rows, 512)
    cos, sin = pl.pallas_call(
        _rope_table_kernel,
        grid=(rows // rb,),
        in_specs=[pl.BlockSpec((rb, LANES), lambda i: (i, 0)), _full((1, LANES))],
        out_specs=[pl.BlockSpec((rb, LANES), lambda i: (i, 0))] * 2,
        out_shape=[jax.ShapeDtypeStruct((rows, LANES), F32)] * 2,
        compiler_params=_params("parallel"),
        name="rope_tables",
    )(pos_rep, freq)
    return cos.reshape(n, ROPE_HALF), sin.reshape(n, ROPE_HALF)


def _pre_attention_kernel(x_ref, cos_ref, sin_ref, gmix_ref, wa_ref, wc_ref, wg_ref, bg_ref,
                          gql_ref, wuq_ref, gkv_ref, wk_ref, wv_ref, rrep_ref, gq_ref, gk_ref,
                          q_out, k_out, v_out, u_out, gate_out):
    tm = x_ref.shape[0]
    hb = _rms(x_ref[...], gmix_ref[...]).astype(BF16)

    a = _dot(hb, wa_ref[...])
    qdn = _rms(a[:, :Q_LORA], gql_ref[...]).astype(BF16)
    kvn = _rms(a[:, Q_LORA:Q_LORA + KV_LORA], gkv_ref[...]).astype(BF16)
    kr = a[:, Q_LORA + KV_LORA:].astype(BF16)
    qf = _dot(qdn, wuq_ref[...])
    kf = _dot(kvn, wk_ref[...]) + _dot(kr, rrep_ref[...])
    v_out[...] = _dot(kvn, wv_ref[...]).astype(BF16)

    cos = cos_ref[...]
    sin = sin_ref[...]
    z = lambda w: jnp.zeros((tm, w), F32)
    c_tab = jnp.concatenate([jnp.ones((tm, NOPE), F32), cos, cos, z(HEAD_SLOT - QK_DIM)], axis=-1)
    s_up = jnp.concatenate([z(NOPE + ROPE_HALF), sin, z(HEAD_SLOT - QK_DIM)], axis=-1)
    s_dn = jnp.concatenate([z(NOPE), -sin, z(HEAD_SLOT - NOPE - ROPE_HALF)], axis=-1)

    def norm_rope(f, g_ref, out, scale):
        ta = c_tab * g_ref[0:1, :]
        tu = s_up * g_ref[1:2, :]
        td = s_dn * g_ref[2:3, :]
        for h in range(N_HEADS):
            xs = f[:, h * HEAD_SLOT:(h + 1) * HEAD_SLOT]
            ss = jnp.sum(xs * xs, axis=-1, keepdims=True)
            s = lax.rsqrt(ss * (1.0 / QK_DIM) + EPS) * scale
            y = xs * ta + pltpu.roll(xs, ROPE_HALF, 1) * tu + pltpu.roll(xs, HEAD_SLOT - ROPE_HALF, 1) * td
            out[:, h * HEAD_SLOT:(h + 1) * HEAD_SLOT] = (y * s).astype(BF16)

    norm_rope(qf, gq_ref, q_out, LOG2E * QK_DIM ** -0.5)
    norm_rope(kf, gk_ref, k_out, 1.0)

    c = _dot(hb, wc_ref[...])
    u_out[...] = c[:, :CONV_W] * _sigmoid(c[:, CONV_W:])
    gate_out[...] = _sigmoid(_dot(hb, wg_ref[...]) + bg_ref[...]).astype(BF16)


def _pre_attention(x2, cos, sin, w, tm):
    n = x2.shape[0]
    row = lambda width: pl.BlockSpec((tm, width), lambda i: (i, 0))
    hs = N_HEADS * HEAD_SLOT
    weights = [w["g_mix"], w["wa"], w["wc"], w["wg"], w["b_gates"], w["g_q_lora"], w["wuq"],
               w["g_kv_lora"], w["wk"], w["wv"], w["rrep"], w["gq3"], w["gk3"]]
    return pl.pallas_call(
        _pre_attention_kernel,
        grid=(n // tm,),
        in_specs=[row(D_MODEL), row(ROPE_HALF), row(ROPE_HALF)] + [_full(a.shape) for a in weights],
        out_specs=[row(hs), row(hs), row(hs), row(CONV_W), row(2 * D_MODEL)],
        out_shape=[jax.ShapeDtypeStruct((n, hs), BF16), jax.ShapeDtypeStruct((n, hs), BF16),
                   jax.ShapeDtypeStruct((n, hs), BF16), jax.ShapeDtypeStruct((n, CONV_W), F32),
                   jax.ShapeDtypeStruct((n, 2 * D_MODEL), BF16)],
        compiler_params=_params("parallel"),
        name="pre_attention",
    )(x2, cos, sin, *weights)


def _attention_kernel(q_ref, k_ref, v_ref, o_ref):
    t = ATTN_TILE
    r = lax.broadcasted_iota(jnp.int32, (t, t), 0) // CHUNK
    c = lax.broadcasted_iota(jnp.int32, (t, t), 1) // CHUNK
    diag_mask = c <= r
    qk = lambda a, b: lax.dot_general(a, b, (((1,), (1,)), ((), ())), preferred_element_type=F32)
    for i in range(q_ref.shape[0] // t):
        rows = slice(i * t, (i + 1) * t)
        past = slice(0, i * t)
        out = None
        for h in range(2):
            sl = slice(h * HEAD_SLOT, (h + 1) * HEAD_SLOT)
            q = q_ref[rows, sl]
            s_d = jnp.where(diag_mask, qk(q, k_ref[rows, sl]), NEG_INF)
            m = jnp.max(s_d, axis=-1, keepdims=True)
            if i > 0:
                s_p = qk(q, k_ref[past, sl])
                m = jnp.maximum(m, jnp.max(s_p, axis=-1, keepdims=True))
            p_d = jnp.exp2(s_d - m)
            l = jnp.sum(p_d, axis=-1, keepdims=True)
            o = _dot(p_d.astype(BF16), v_ref[rows, sl])
            if i > 0:
                p_p = jnp.exp2(s_p - m)
                l = l + jnp.sum(p_p, axis=-1, keepdims=True)
                o = o + _dot(p_p.astype(BF16), v_ref[past, sl])
            o = o * (1.0 / l)
            out = o if out is None else out + o
        o_ref[rows, :] = out.astype(BF16)


def _attention(q, k, v, b, t):
    hs2 = 2 * HEAD_SLOT
    spec = pl.BlockSpec((t, hs2), lambda bi, p: (bi, p))
    return pl.pallas_call(
        _attention_kernel,
        grid=(b, N_HEADS // 2),
        in_specs=[spec, spec, spec],
        out_specs=pl.BlockSpec((t, HEAD_SLOT), lambda bi, p: (bi, p)),
        out_shape=jax.ShapeDtypeStruct((b * t, N_HEADS * V_DIM), BF16),
        compiler_params=_params("parallel", "parallel"),
        name="attention",
    )(q, k, v)


def _merge_kernel(blocks_per_seq, attn_ref, u_ref, halo_ref, gate_ref, x_ref, wo_ref, wdw_ref, bdw_ref,
                  gln_ref, bln_ref, wco_ref, wout_ref, gffn_ref, wr_ref, br_ref,
                  x1_out, h2p_out, route_out, sh_ref):
    tm = x_ref.shape[0]
    first = (pl.program_id(0) % blocks_per_seq) == 0
    halo = halo_ref[...]
    sh_ref[0, 0:CONV_HALO, :] = jnp.where(first, jnp.zeros_like(halo), halo)
    sh_ref[0, CONV_HALO:, :] = u_ref[...]
    span = tm + CONV_HALO - SUBLANES
    for s in range(1, SUBLANES):
        sh_ref[s, 0:span, :] = sh_ref[0, s:s + span, :]

    base = CONV_HALO - (CONV_K - 1)
    half = D_MODEL // 2
    for sb in range(tm // SUB_ROWS):
        rows = slice(sb * SUB_ROWS, (sb + 1) * SUB_ROWS)
        parts = []
        for ck in range(SUB_ROWS // CONV_CHUNK):
            c0 = sb * SUB_ROWS + ck * CONV_CHUNK
            acc = jnp.broadcast_to(bdw_ref[...], (CONV_CHUNK, CONV_W))
            for j in range(CONV_K):
                s, a = (base + j) % SUBLANES, (base + j) // SUBLANES * SUBLANES
                acc = acc + sh_ref[s, c0 + a:c0 + a + CONV_CHUNK, :] * wdw_ref[j:j + 1, :]
            parts.append(acc)
        conv = jnp.concatenate(parts, axis=0)
        mu = jnp.mean(conv, axis=-1, keepdims=True)
        cen = conv - mu
        var = jnp.mean(cen * cen, axis=-1, keepdims=True)
        ln = cen * lax.rsqrt(var + EPS) * gln_ref[...] + bln_ref[...]
        ub = (ln * _sigmoid(ln)).astype(BF16)
        y_b = _dot(ub, wco_ref[...])
        y_a = _dot(attn_ref[rows, :], wo_ref[...])
        merged = gate_ref[rows, :D_MODEL].astype(F32) * y_a + gate_ref[rows, D_MODEL:].astype(F32) * y_b
        x1 = x_ref[rows, :] + _dot(merged.astype(BF16), wout_ref[...])
        x1_out[rows, :] = x1
        h2 = _rms(x1, gffn_ref[...])
        h2p_out[rows, :] = _pack_words(h2[:, :half], h2[:, half:])
        logits = _dot(h2, wr_ref[...], precision=lax.Precision.HIGHEST) + br_ref[...]
        route_out[rows, :] = _route(logits)


def _route(logits):
    lane_i = lax.broadcasted_iota(jnp.int32, logits.shape, 1)
    lane = lane_i.astype(F32)
    big = float(LANES)
    is_g = (lane_i >= N_EXPERTS) & (lane_i < N_EXPERTS + N_GROUPS)
    gl = jnp.where(is_g, logits, NEG_INF)
    gmax = jnp.max(gl, axis=-1, keepdims=True)
    gidx = jnp.min(jnp.where(gl == gmax, lane - N_EXPERTS, big), axis=-1, keepdims=True)
    g_p = 1.0 / jnp.sum(jnp.exp(gl - gmax), axis=-1, keepdims=True)
    lane_group = (lane_i // EXPERTS_PER_GROUP).astype(F32)
    valid = (lane_i < N_EXPERTS) & (lane_group == gidx)
    el = jnp.where(valid, logits, NEG_INF)
    ee = jnp.exp(el - jnp.max(el, axis=-1, keepdims=True))
    within = jnp.where(valid, ee / jnp.sum(ee, axis=-1, keepdims=True), -1.0)
    p1 = jnp.max(within, axis=-1, keepdims=True)
    i1 = jnp.min(jnp.where(within == p1, lane, big), axis=-1, keepdims=True)
    within2 = jnp.where(lane == i1, -1.0, within)
    p2 = jnp.max(within2, axis=-1, keepdims=True)
    i2 = jnp.min(jnp.where(within2 == p2, lane, big), axis=-1, keepdims=True)
    psum = p1 + p2
    w1 = g_p * (p1 / psum)
    w2 = g_p * (p2 / psum)
    route = jnp.where(lane_i == 0, i1, 0.0)
    route = jnp.where(lane_i == 1, i2, route)
    route = jnp.where(lane_i == 2, w1, route)
    return jnp.where(lane_i == 3, w2, route)


def _merge(attn, u, gates, x2, w, t, tm):
    n = x2.shape[0]
    bps = t // tm
    row = lambda width: pl.BlockSpec((tm, width), lambda i: (i, 0))
    hpb = tm // CONV_HALO
    halo = pl.BlockSpec((CONV_HALO, CONV_W), lambda i: (jnp.maximum(i * hpb - 1, 0), 0))
    weights = [w["wo"], w["w_dw"], w["b_dw"], w["g_conv_ln"], w["b_conv_ln"], w["wco"], w["wout"],
               w["g_ffn"], w["wr"], w["br"]]
    return pl.pallas_call(
        functools.partial(_merge_kernel, bps),
        grid=(n // tm,),
        in_specs=[row(N_HEADS * V_DIM), row(CONV_W), halo, row(2 * D_MODEL), row(D_MODEL)]
        + [_full(a.shape) for a in weights],
        out_specs=[row(D_MODEL), row(D_MODEL // 2), row(LANES)],
        out_shape=[jax.ShapeDtypeStruct((n, D_MODEL), F32), jax.ShapeDtypeStruct((n, D_MODEL // 2), jnp.uint32),
                   jax.ShapeDtypeStruct((n, LANES), F32)],
        scratch_shapes=[pltpu.VMEM((SUBLANES, tm + CONV_HALO, CONV_W), F32)],
        compiler_params=_params("parallel"),
        name="merge_router",
    )(attn, u, u, gates, x2, *weights)


def _rank_kernel(route_ref, pos_out, meta_out, r1_sc, r2_sc):
    tb = route_ref.shape[0]
    lane = lax.broadcasted_iota(jnp.int32, (RANK_CHUNK, LANES), 1).astype(F32)
    rr = lax.broadcasted_iota(jnp.int32, (RANK_CHUNK, RANK_CHUNK), 0)
    cc = lax.broadcasted_iota(jnp.int32, (RANK_CHUNK, RANK_CHUNK), 1)
    lower = jnp.where(rr > cc, 1.0, 0.0).astype(BF16)
    run = jnp.zeros((1, LANES), F32)
    for choice, sc in ((0, r1_sc), (1, r2_sc)):
        for ci in range(tb // RANK_CHUNK):
            rows = slice(ci * RANK_CHUNK, (ci + 1) * RANK_CHUNK)
            oh = jnp.where(lane == route_ref[rows, choice:choice + 1], 1.0, 0.0)
            sc[rows, :] = run + _dot(lower, oh.astype(BF16))
            run = run + jnp.sum(oh, axis=0, keepdims=True)
    cnt = run
    cnt8 = jnp.ceil(cnt * (1.0 / SUBLANES)) * SUBLANES
    ur = lax.broadcasted_iota(jnp.int32, (LANES, LANES), 0)
    uc = lax.broadcasted_iota(jnp.int32, (LANES, LANES), 1)
    upper = jnp.where(ur < uc, 1.0, 0.0)
    off = _dot(jnp.broadcast_to(cnt8, (SUBLANES, LANES)), upper, precision=lax.Precision.HIGHEST)[0:1, :]
    lane_t = lax.broadcasted_iota(jnp.int32, (tb, LANES), 1)
    lane_tf = lane_t.astype(F32)
    p1 = jnp.sum(jnp.where(lane_tf == route_ref[:, 0:1], off + r1_sc[...], 0.0), axis=-1, keepdims=True)
    p2 = jnp.sum(jnp.where(lane_tf == route_ref[:, 1:2], off + r2_sc[...], 0.0), axis=-1, keepdims=True)
    pos = jnp.where(lane_t == 0, p1, jnp.where(lane_t == 1, p2, 0.0))
    pos_out[...] = pos.astype(jnp.int32)
    row8 = lax.broadcasted_iota(jnp.int32, (SUBLANES, LANES), 0)
    meta = jnp.where(row8 == 0, jnp.broadcast_to(off, (SUBLANES, LANES)),
                     jnp.where(row8 == 1, jnp.broadcast_to(cnt, (SUBLANES, LANES)), 0.0))
    meta_out[...] = meta.astype(jnp.int32)


def _rank(route, b, t):
    return pl.pallas_call(
        _rank_kernel,
        grid=(b,),
        in_specs=[pl.BlockSpec((t, LANES), lambda i: (i, 0))],
        out_specs=[pl.BlockSpec((t, LANES), lambda i: (i, 0)), pl.BlockSpec((SUBLANES, LANES), lambda i: (i, 0))],
        out_shape=[jax.ShapeDtypeStruct((b * t, LANES), jnp.int32),
                   jax.ShapeDtypeStruct((b * SUBLANES, LANES), jnp.int32)],
        scratch_shapes=[pltpu.VMEM((t, LANES), F32), pltpu.VMEM((t, LANES), F32)],
        compiler_params=_params("parallel"),
        name="rank",
    )(route)


def _sorted_rows(t):
    return 2 * t + N_EXPERTS * SUBLANES + ROW_TILE


def _dispatch_kernel(pos_ref, h2p_ref, xs_out):
    t = h2p_ref.shape[0]
    xs_out[...] = jnp.zeros(xs_out.shape, xs_out.dtype)

    def body(tok, carry):
        row = h2p_ref[pl.ds(tok, 1), :]
        xs_out[0, pl.ds(pos_ref[2 * tok], 1), :] = row
        xs_out[0, pl.ds(pos_ref[2 * tok + 1], 1), :] = row
        return carry

    lax.fori_loop(0, t, body, 0, unroll=8)


def _dispatch(pos_flat, h2p, b, t):
    rows = _sorted_rows(t)
    half = D_MODEL // 2
    return pl.pallas_call(
        _dispatch_kernel,
        grid=(b,),
        in_specs=[pl.BlockSpec((2 * t,), lambda i: (i,), memory_space=pltpu.SMEM),
                  pl.BlockSpec((t, half), lambda i: (i, 0))],
        out_specs=pl.BlockSpec((1, rows, half), lambda i: (i, 0, 0)),
        out_shape=jax.ShapeDtypeStruct((b, rows, half), jnp.uint32),
        compiler_params=_params("parallel"),
        name="dispatch",
    )(pos_flat, h2p)


def _expert_kernel(off_ref, cnt_ref, xs_ref, wgl_ref, wgh_ref, wul_ref, wuh_ref, wd_ref, ys_out):
    bi = pl.program_id(0)
    e = pl.program_id(1)

    @pl.when(e == 0)
    def _():
        ys_out[...] = jnp.zeros(ys_out.shape, ys_out.dtype)

    off = off_ref[bi * N_EXPERTS + e]
    cnt = cnt_ref[bi * N_EXPERTS + e]
    n_tiles = (cnt + ROW_TILE - 1) // ROW_TILE
    half = D_MODEL // 2

    def tile(i, carry):
        start = pl.multiple_of(off + i * ROW_TILE, SUBLANES)
        lo, hi = _unpack_words(xs_ref[0, pl.ds(start, ROW_TILE), :])
        lo = lo.astype(BF16)
        hi = hi.astype(BF16)
        a = _dot(lo, wgl_ref[0]) + _dot(hi, wgh_ref[0])
        u = _dot(lo, wul_ref[0]) + _dot(hi, wuh_ref[0])
        mid = (a * _sigmoid(a) * u).astype(BF16)
        y = _dot(mid, wd_ref[0])
        ys_out[0, pl.ds(start, ROW_TILE), :] = _pack_words(y[:, :half], y[:, half:])
        return carry

    lax.fori_loop(0, n_tiles, tile, 0)


def _expert_ffn(off, cnt, xs, w, b, t):
    rows = _sorted_rows(t)
    half = D_MODEL // 2
    wspec = lambda shape: pl.BlockSpec((1,) + shape, lambda bi, e, *_: (e, 0, 0))
    grid_spec = pltpu.PrefetchScalarGridSpec(
        num_scalar_prefetch=2,
        grid=(b, N_EXPERTS),
        in_specs=[pl.BlockSpec((1, rows, half), lambda bi, e, *_: (bi, 0, 0)),
                  wspec((half, EXPERT_FF)), wspec((half, EXPERT_FF)), wspec((half, EXPERT_FF)),
                  wspec((half, EXPERT_FF)), wspec((EXPERT_FF, D_MODEL))],
        out_specs=pl.BlockSpec((1, rows, half), lambda bi, e, *_: (bi, 0, 0)),
    )
    return pl.pallas_call(
        _expert_kernel,
        grid_spec=grid_spec,
        out_shape=jax.ShapeDtypeStruct((b, rows, half), jnp.uint32),
        compiler_params=_params("parallel", "arbitrary"),
        name="expert_ffn",
    )(off, cnt, xs, w["wg_lo"], w["wg_hi"], w["wu_lo"], w["wu_hi"], w["wd"])


def _combine_kernel(pos_ref, ys_ref, x1_ref, route_ref, out_ref, g1, g2):
    tc = x1_ref.shape[0]
    base = pl.program_id(1) * tc

    def body(r, carry):
        tok = base + r
        g1[pl.ds(r, 1), :] = ys_ref[0, pl.ds(pos_ref[2 * tok], 1), :]
        g2[pl.ds(r, 1), :] = ys_ref[0, pl.ds(pos_ref[2 * tok + 1], 1), :]
        return carry

    lax.fori_loop(0, tc, body, 0, unroll=8)
    w1 = route_ref[:, 2:3]
    w2 = route_ref[:, 3:4]
    lo1, hi1 = _unpack_words(g1[...])
    lo2, hi2 = _unpack_words(g2[...])
    half = D_MODEL // 2
    out_ref[:, :half] = x1_ref[:, :half] + (w1 * lo1 + w2 * lo2)
    out_ref[:, half:] = x1_ref[:, half:] + (w1 * hi1 + w2 * hi2)


def _combine(pos_flat, ys, x1, route, b, t):
    rows = _sorted_rows(t)
    half = D_MODEL // 2
    tc = COMBINE_TILE
    nt = t // tc
    return pl.pallas_call(
        _combine_kernel,
        grid=(b, nt),
        in_specs=[pl.BlockSpec((2 * t,), lambda bi, j: (bi,), memory_space=pltpu.SMEM),
                  pl.BlockSpec((1, rows, half), lambda bi, j: (bi, 0, 0)),
                  pl.BlockSpec((tc, D_MODEL), lambda bi, j: (bi * nt + j, 0)),
                  pl.BlockSpec((tc, LANES), lambda bi, j: (bi * nt + j, 0))],
        out_specs=pl.BlockSpec((tc, D_MODEL), lambda bi, j: (bi * nt + j, 0)),
        out_shape=jax.ShapeDtypeStruct((b * t, D_MODEL), F32),
        scratch_shapes=[pltpu.VMEM((tc, half), jnp.uint32), pltpu.VMEM((tc, half), jnp.uint32)],
        compiler_params=_params("parallel", "arbitrary"),
        name="combine",
    )(pos_flat, ys, x1, route)


def _head_slots(wmat, width):
    k = wmat.shape[0]
    w3 = wmat.reshape(k, N_HEADS, width)
    return jnp.pad(w3, ((0, 0), (0, 0), (0, HEAD_SLOT - width))).reshape(k, N_HEADS * HEAD_SLOT)


def _gain3(g):
    gp = jnp.pad(g.astype(F32), (0, HEAD_SLOT - QK_DIM))
    return jnp.stack([gp, jnp.roll(gp, ROPE_HALF), jnp.roll(gp, -ROPE_HALF)])


def _prepare_weights(g_mix, w_in, g_q_lora, w_uq, g_kv_lora, w_ukv, g_qk_q, g_qk_k, w_o_mla, w_dw, b_dw,
                     g_conv_ln, b_conv_ln, w_conv_out, b_gates, w_out, g_ffn, w_group_router, b_group_router,
                     w_expert_router, b_expert_router, w_e_gate, w_e_up, w_e_down):
    row = lambda a: a.astype(F32).reshape(1, -1)
    s0, s1, s2, s3 = Q_LORA, Q_LORA + KV_LORA, Q_LORA + KV_LORA + ROPE, Q_LORA + KV_LORA + ROPE + 2 * CONV_W
    w = {}
    w["g_mix"] = row(g_mix)
    w["wa"] = jnp.concatenate([w_in[:, :s1], jnp.pad(w_in[:, s1:s2], ((0, 0), (0, LANES - ROPE)))], axis=1).astype(BF16)
    w["wc"] = w_in[:, s2:s3].astype(BF16)
    w["wg"] = w_in[:, s3:].astype(BF16)
    w["b_gates"] = row(b_gates)
    w["g_q_lora"] = row(g_q_lora)
    w["wuq"] = _head_slots(w_uq, QK_DIM).astype(BF16)
    w["g_kv_lora"] = row(g_kv_lora)
    kv3 = w_ukv.reshape(KV_LORA, N_HEADS, NOPE + V_DIM)
    w["wk"] = _head_slots(kv3[:, :, :NOPE].reshape(KV_LORA, N_HEADS * NOPE), NOPE).astype(BF16)
    v3 = kv3[:, :, NOPE:]
    zeros = jnp.zeros_like(v3)
    odd = (jnp.arange(N_HEADS) % 2 == 1)[None, :, None]
    wv = jnp.concatenate([jnp.where(odd, zeros, v3), jnp.where(odd, v3, zeros)], axis=-1)
    w["wv"] = wv.reshape(KV_LORA, N_HEADS * HEAD_SLOT).astype(BF16)
    src = jnp.arange(LANES)[:, None]
    dst = jnp.arange(N_HEADS * HEAD_SLOT)[None, :]
    w["rrep"] = ((src < ROPE) & ((dst % HEAD_SLOT) == (src + NOPE))).astype(BF16)
    w["gq3"] = _gain3(g_qk_q)
    w["gk3"] = _gain3(g_qk_k)
    w["wo"] = w_o_mla.astype(BF16)
    w["w_dw"] = jnp.pad(w_dw.astype(F32), ((0, 32 - CONV_K), (0, 0)))
    w["b_dw"] = row(b_dw)
    w["g_conv_ln"] = row(g_conv_ln)
    w["b_conv_ln"] = row(b_conv_ln)
    w["wco"] = w_conv_out.astype(BF16)
    w["wout"] = w_out.astype(BF16)
    w["g_ffn"] = row(g_ffn)
    pad_r = LANES - N_EXPERTS - N_GROUPS
    w["wr"] = jnp.concatenate([w_expert_router, w_group_router, jnp.zeros((D_MODEL, pad_r), F32)], axis=1).astype(F32)
    w["br"] = jnp.concatenate([b_expert_router, b_group_router, jnp.zeros((pad_r,), F32)]).astype(F32).reshape(1, LANES)
    half = D_MODEL // 2
    w["wg_lo"] = w_e_gate[:, :half].astype(BF16)
    w["wg_hi"] = w_e_gate[:, half:].astype(BF16)
    w["wu_lo"] = w_e_up[:, :half].astype(BF16)
    w["wu_hi"] = w_e_up[:, half:].astype(BF16)
    w["wd"] = w_e_down.astype(BF16)
    return w


def kernel(x, positions, g_mix, w_in, g_q_lora, w_uq, g_kv_lora, w_ukv, g_qk_q, g_qk_k, w_o_mla, w_dw, b_dw,
           g_conv_ln, b_conv_ln, w_conv_out, b_gates, w_out, g_ffn, w_group_router, b_group_router,
           w_expert_router, b_expert_router, w_e_gate, w_e_up, w_e_down):
    b, t, d = x.shape
    assert d == D_MODEL and g_mix.shape[0] == 1
    assert t % ATTN_TILE == 0 and t % RANK_CHUNK == 0 and t % COMBINE_TILE == 0
    tm = min(TOKEN_BLOCK, t)
    assert t % tm == 0 and tm % CONV_HALO == 0
    n = b * t
    w = _prepare_weights(g_mix[0], w_in[0], g_q_lora[0], w_uq[0], g_kv_lora[0], w_ukv[0], g_qk_q[0], g_qk_k[0],
                         w_o_mla[0], w_dw[0], b_dw[0], g_conv_ln[0], b_conv_ln[0], w_conv_out[0], b_gates[0],
                         w_out[0], g_ffn[0], w_group_router[0], b_group_router[0], w_expert_router[0],
                         b_expert_router[0], w_e_gate[0], w_e_up[0], w_e_down[0])
    x2 = x.reshape(n, d)
    cos, sin = _rope_tables(positions)
    q, k, v, u, gates = _pre_attention(x2, cos, sin, w, tm)
    attn = _attention(q, k, v, b, t)
    x1, h2p, route = _merge(attn, u, gates, x2, w, t, tm)
    pos, meta = _rank(route, b, t)
    pos_flat = pos[:, :2].reshape(n * 2)
    meta3 = meta.reshape(b, SUBLANES, LANES)
    off = meta3[:, 0, :N_EXPERTS].reshape(b * N_EXPERTS)
    cnt = meta3[:, 1, :N_EXPERTS].reshape(b * N_EXPERTS)
    xs = _dispatch(pos_flat, h2p, b, t)
    ys = _expert_ffn(off, cnt, xs, w, b, t)
    out = _combine(pos_flat, ys, x1, route, b, t)
    return out.reshape(b, t, d)
```

```python
import functools

import jax
import jax.numpy as jnp
import numpy as np
from jax import lax
from jax.experimental import pallas as pl
from jax.experimental.pallas import tpu as pltpu

D_MODEL = 1024
N_HEADS = 8
NOPE = 64
ROPE = 32
ROPE_HALF = ROPE // 2
QK_DIM = NOPE + ROPE
V_DIM = 64
Q_LORA = 256
KV_LORA = 128
CONV_W = 512
CONV_K = 31
N_GROUPS = 4
EXPERTS_PER_GROUP = 8
N_EXPERTS = N_GROUPS * EXPERTS_PER_GROUP
EXPERT_FF = 256
EPS = 1e-6
ROPE_THETA = 10000.0
CHUNK = 64

LANES = 128
SUBLANES = 8
HEAD_SLOT = LANES
VMEM_LIMIT_BYTES = 56 * 1024 * 1024

TOKEN_BLOCK = 512
ATTN_TILE = 256
CONV_HALO = 32
ROW_TILE = 256
EXPERTS_PER_STEP = 2
COMBINE_TILE = 256
RANK_CHUNK = 256
SUB_ROWS = 256
CONV_CHUNK = 32

F32 = jnp.float32
BF16 = jnp.bfloat16
NEG_INF = float("-inf")
LOG2E = 1.4426950408889634


def _dot(a, b, **kw):
    return jnp.dot(a, b, preferred_element_type=F32, **kw)


def _rms(x, g):
    return x * lax.rsqrt(jnp.mean(x * x, axis=-1, keepdims=True) + EPS) * g


def _sigmoid(x):
    return 1.0 / (1.0 + jnp.exp(-x))


def _pack_words(lo, hi):
    return pltpu.pack_elementwise([lo, hi], packed_dtype=BF16)


def _unpack_words(wd):
    lo = pltpu.unpack_elementwise(wd, index=0, packed_dtype=BF16, unpacked_dtype=F32)
    hi = pltpu.unpack_elementwise(wd, index=1, packed_dtype=BF16, unpacked_dtype=F32)
    return lo, hi


PACKED_TILE = (D_MODEL // 2 // LANES, SUBLANES, LANES)
ROW_PIECES = PACKED_TILE[0]


def _store_tile_rows(ref, g0, words, lead=()):
    groups = words.shape[0] // SUBLANES
    for c in range(ROW_PIECES):
        piece = words[:, c * LANES:(c + 1) * LANES].reshape(groups, SUBLANES, LANES)
        ref[lead + (pl.ds(g0, groups), c)] = piece


def _load_tile_rows(ref, g0, rows, lead=()):
    groups = rows // SUBLANES
    pieces = [ref[lead + (pl.ds(g0, groups), c)].reshape(rows, LANES) for c in range(ROW_PIECES)]
    return jnp.concatenate(pieces, axis=1)


def _row_ds(q):
    return pl.ds(q, ROW_PIECES, stride=SUBLANES)


def _params(*sem):
    return pltpu.CompilerParams(dimension_semantics=sem, vmem_limit_bytes=VMEM_LIMIT_BYTES)


def _full(shape):
    nd = len(shape)
    return pl.BlockSpec(shape, lambda *_: (0,) * nd)


def _rope_table_kernel(pos_ref, freq_ref, cos_ref, sin_ref):
    ang = pos_ref[...].astype(F32) * freq_ref[...]
    cos_ref[...] = jnp.cos(ang)
    sin_ref[...] = jnp.sin(ang)


def _rope_tables(positions):
    n = positions.size
    per_row = LANES // ROPE_HALF
    rows = n // per_row
    pos_rep = jnp.repeat(positions.reshape(rows, per_row), ROPE_HALF, axis=1)
    inv_freq = ROPE_THETA ** (-jnp.arange(ROPE_HALF, dtype=F32) / ROPE_HALF)
    freq = jnp.tile(inv_freq, per_row).reshape(1, LANES)
    rb = min(rows, 512)
    cos, sin = pl.pallas_call(
        _rope_table_kernel,
        grid=(rows // rb,),
        in_specs=[pl.BlockSpec((rb, LANES), lambda i: (i, 0)), _full((1, LANES))],
        out_specs=[pl.BlockSpec((rb, LANES), lambda i: (i, 0))] * 2,
        out_shape=[jax.ShapeDtypeStruct((rows, LANES), F32)] * 2,
        compiler_params=_params("parallel"),
        name="rope_tables",
    )(pos_rep, freq)
    return cos.reshape(n, ROPE_HALF), sin.reshape(n, ROPE_HALF)


def _pre_attention_kernel(x_ref, cos_ref, sin_ref, gmix_ref, wa_ref, wc_ref, wg_ref, bg_ref,
                          gql_ref, wuq_ref, gkv_ref, wk_ref, wv_ref, rrep_ref, gq_ref, gk_ref,
                          q_out, k_out, v_out, u_out, gate_out):
    tm = x_ref.shape[0]
    hb = _rms(x_ref[...], gmix_ref[...]).astype(BF16)

    a = _dot(hb, wa_ref[...])
    qdn = _rms(a[:, :Q_LORA], gql_ref[...]).astype(BF16)
    kvn = _rms(a[:, Q_LORA:Q_LORA + KV_LORA], gkv_ref[...]).astype(BF16)
    kr = a[:, Q_LORA + KV_LORA:].astype(BF16)
    qf = _dot(qdn, wuq_ref[...])
    kf = _dot(kvn, wk_ref[...]) + _dot(kr, rrep_ref[...])
    v_out[...] = _dot(kvn, wv_ref[...]).astype(BF16)

    cos = cos_ref[...]
    sin = sin_ref[...]
    z = lambda w: jnp.zeros((tm, w), F32)
    c_tab = jnp.concatenate([jnp.ones((tm, NOPE), F32), cos, cos, z(HEAD_SLOT - QK_DIM)], axis=-1)
    s_up = jnp.concatenate([z(NOPE + ROPE_HALF), sin, z(HEAD_SLOT - QK_DIM)], axis=-1)
    s_dn = jnp.concatenate([z(NOPE), -sin, z(HEAD_SLOT - NOPE - ROPE_HALF)], axis=-1)

    def norm_rope(f, g_ref, out, scale):
        ta = c_tab * g_ref[0:1, :]
        tu = s_up * g_ref[1:2, :]
        td = s_dn * g_ref[2:3, :]
        for h in range(N_HEADS):
            xs = f[:, h * HEAD_SLOT:(h + 1) * HEAD_SLOT]
            ss = jnp.sum(xs * xs, axis=-1, keepdims=True)
            s = lax.rsqrt(ss * (1.0 / QK_DIM) + EPS) * scale
            y = xs * ta + pltpu.roll(xs, ROPE_HALF, 1) * tu + pltpu.roll(xs, HEAD_SLOT - ROPE_HALF, 1) * td
            out[:, h * HEAD_SLOT:(h + 1) * HEAD_SLOT] = (y * s).astype(BF16)

    norm_rope(qf, gq_ref, q_out, LOG2E * QK_DIM ** -0.5)
    norm_rope(kf, gk_ref, k_out, 1.0)

    c = _dot(hb, wc_ref[...])
    u_out[...] = c[:, :CONV_W] * _sigmoid(c[:, CONV_W:])
    gate_out[...] = _sigmoid(_dot(hb, wg_ref[...]) + bg_ref[...]).astype(BF16)


def _pre_attention(x2, cos, sin, w, tm):
    n = x2.shape[0]
    row = lambda width: pl.BlockSpec((tm, width), lambda i: (i, 0))
    hs = N_HEADS * HEAD_SLOT
    weights = [w["g_mix"], w["wa"], w["wc"], w["wg"], w["b_gates"], w["g_q_lora"], w["wuq"],
               w["g_kv_lora"], w["wk"], w["wv"], w["rrep"], w["gq3"], w["gk3"]]
    return pl.pallas_call(
        _pre_attention_kernel,
        grid=(n // tm,),
        in_specs=[row(D_MODEL), row(ROPE_HALF), row(ROPE_HALF)] + [_full(a.shape) for a in weights],
        out_specs=[row(hs), row(hs), row(hs), row(CONV_W), row(2 * D_MODEL)],
        out_shape=[jax.ShapeDtypeStruct((n, hs), BF16), jax.ShapeDtypeStruct((n, hs), BF16),
                   jax.ShapeDtypeStruct((n, hs), BF16), jax.ShapeDtypeStruct((n, CONV_W), F32),
                   jax.ShapeDtypeStruct((n, 2 * D_MODEL), BF16)],
        compiler_params=_params("parallel"),
        name="pre_attention",
    )(x2, cos, sin, *weights)


def _attention_kernel(q_ref, k_ref, v_ref, o_ref):
    t = ATTN_TILE
    r = lax.broadcasted_iota(jnp.int32, (t, t), 0) // CHUNK
    c = lax.broadcasted_iota(jnp.int32, (t, t), 1) // CHUNK
    diag_mask = c <= r
    qk = lambda a, b: lax.dot_general(a, b, (((1,), (1,)), ((), ())), preferred_element_type=F32)
    for i in range(q_ref.shape[0] // t):
        rows = slice(i * t, (i + 1) * t)
        past = slice(0, i * t)
        out = None
        for h in range(2):
            sl = slice(h * HEAD_SLOT, (h + 1) * HEAD_SLOT)
            q = q_ref[rows, sl]
            s_d = jnp.where(diag_mask, qk(q, k_ref[rows, sl]), NEG_INF)
            m = jnp.max(s_d, axis=-1, keepdims=True)
            if i > 0:
                s_p = qk(q, k_ref[past, sl])
                m = jnp.maximum(m, jnp.max(s_p, axis=-1, keepdims=True))
            p_d = jnp.exp2(s_d - m)
            l = jnp.sum(p_d, axis=-1, keepdims=True)
            o = _dot(p_d.astype(BF16), v_ref[rows, sl])
            if i > 0:
                p_p = jnp.exp2(s_p - m)
                l = l + jnp.sum(p_p, axis=-1, keepdims=True)
                o = o + _dot(p_p.astype(BF16), v_ref[past, sl])
            o = o * (1.0 / l)
            out = o if out is None else out + o
        o_ref[rows, :] = out.astype(BF16)


def _attention(q, k, v, b, t):
    hs2 = 2 * HEAD_SLOT
    spec = pl.BlockSpec((t, hs2), lambda bi, p: (bi, p))
    return pl.pallas_call(
        _attention_kernel,
        grid=(b, N_HEADS // 2),
        in_specs=[spec, spec, spec],
        out_specs=pl.BlockSpec((t, HEAD_SLOT), lambda bi, p: (bi, p)),
        out_shape=jax.ShapeDtypeStruct((b * t, N_HEADS * V_DIM), BF16),
        compiler_params=_params("parallel", "parallel"),
        name="attention",
    )(q, k, v)


def _merge_kernel(blocks_per_seq, attn_ref, u_ref, halo_ref, gate_ref, x_ref, wo_ref, wdw_ref, bdw_ref,
                  gln_ref, bln_ref, wco_ref, wout_ref, gffn_ref, wrh_ref, wrl_ref, br_ref,
                  x1_out, h2p_out, route_out, sh_ref):
    tm = x_ref.shape[0]
    first = (pl.program_id(0) % blocks_per_seq) == 0
    halo = halo_ref[...]
    sh_ref[0, 0:CONV_HALO, :] = jnp.where(first, jnp.zeros_like(halo), halo)
    sh_ref[0, CONV_HALO:, :] = u_ref[...]
    span = tm + CONV_HALO - SUBLANES
    for s in range(1, SUBLANES):
        sh_ref[s, 0:span, :] = sh_ref[0, s:s + span, :]

    base = CONV_HALO - (CONV_K - 1)
    half = D_MODEL // 2
    for sb in range(tm // SUB_ROWS):
        rows = slice(sb * SUB_ROWS, (sb + 1) * SUB_ROWS)
        parts = []
        for ck in range(SUB_ROWS // CONV_CHUNK):
            c0 = sb * SUB_ROWS + ck * CONV_CHUNK
            acc = jnp.broadcast_to(bdw_ref[...], (CONV_CHUNK, CONV_W))
            for j in range(CONV_K):
                s, a = (base + j) % SUBLANES, (base + j) // SUBLANES * SUBLANES
                acc = acc + sh_ref[s, c0 + a:c0 + a + CONV_CHUNK, :] * wdw_ref[j:j + 1, :]
            parts.append(acc)
        conv = jnp.concatenate(parts, axis=0)
        mu = jnp.mean(conv, axis=-1, keepdims=True)
        cen = conv - mu
        var = jnp.mean(cen * cen, axis=-1, keepdims=True)
        ln = cen * lax.rsqrt(var + EPS) * gln_ref[...] + bln_ref[...]
        ub = (ln * _sigmoid(ln)).astype(BF16)
        y_b = _dot(ub, wco_ref[...])
        y_a = _dot(attn_ref[rows, :], wo_ref[...])
        merged = gate_ref[rows, :D_MODEL].astype(F32) * y_a + gate_ref[rows, D_MODEL:].astype(F32) * y_b
        x1 = x_ref[rows, :] + _dot(merged.astype(BF16), wout_ref[...])
        x1_out[rows, :] = x1
        h2 = _rms(x1, gffn_ref[...])
        _store_tile_rows(h2p_out, sb * (SUB_ROWS // SUBLANES), _pack_words(h2[:, :half], h2[:, half:]))
        h2_hi = h2.astype(BF16)
        h2_lo = (h2 - h2_hi.astype(F32)).astype(BF16)
        logits = (_dot(h2_hi, wrh_ref[...]) + _dot(h2_lo, wrh_ref[...]) + _dot(h2_hi, wrl_ref[...])
                  + br_ref[...])
        route_out[rows, :] = _route(logits)


def _route(logits):
    lane_i = lax.broadcasted_iota(jnp.int32, logits.shape, 1)
    lane = lane_i.astype(F32)
    big = float(LANES)
    is_g = (lane_i >= N_EXPERTS) & (lane_i < N_EXPERTS + N_GROUPS)
    gl = jnp.where(is_g, logits, NEG_INF)
    gmax = jnp.max(gl, axis=-1, keepdims=True)
    gidx = jnp.min(jnp.where(gl == gmax, lane - N_EXPERTS, big), axis=-1, keepdims=True)
    g_p = 1.0 / jnp.sum(jnp.exp(gl - gmax), axis=-1, keepdims=True)
    lane_group = (lane_i // EXPERTS_PER_GROUP).astype(F32)
    valid = (lane_i < N_EXPERTS) & (lane_group == gidx)
    el = jnp.where(valid, logits, NEG_INF)
    ee = jnp.exp(el - jnp.max(el, axis=-1, keepdims=True))
    within = jnp.where(valid, ee / jnp.sum(ee, axis=-1, keepdims=True), -1.0)
    p1 = jnp.max(within, axis=-1, keepdims=True)
    i1 = jnp.min(jnp.where(within == p1, lane, big), axis=-1, keepdims=True)
    within2 = jnp.where(lane == i1, -1.0, within)
    p2 = jnp.max(within2, axis=-1, keepdims=True)
    i2 = jnp.min(jnp.where(within2 == p2, lane, big), axis=-1, keepdims=True)
    psum = p1 + p2
    w1 = g_p * (p1 / psum)
    w2 = g_p * (p2 / psum)
    route = jnp.where(lane_i == 0, i1, 0.0)
    route = jnp.where(lane_i == 1, i2, route)
    route = jnp.where(lane_i == 2, w1, route)
    return jnp.where(lane_i == 3, w2, route)


def _merge(attn, u, gates, x2, w, t, tm):
    n = x2.shape[0]
    bps = t // tm
    row = lambda width: pl.BlockSpec((tm, width), lambda i: (i, 0))
    hpb = tm // CONV_HALO
    halo = pl.BlockSpec((CONV_HALO, CONV_W), lambda i: (jnp.maximum(i * hpb - 1, 0), 0))
    weights = [w["wo"], w["w_dw"], w["b_dw"], w["g_conv_ln"], w["b_conv_ln"], w["wco"], w["wout"],
               w["g_ffn"], w["wr_hi"], w["wr_lo"], w["br"]]
    tile_rows = (tm // SUBLANES,) + PACKED_TILE
    return pl.pallas_call(
        functools.partial(_merge_kernel, bps),
        grid=(n // tm,),
        in_specs=[row(N_HEADS * V_DIM), row(CONV_W), halo, row(2 * D_MODEL), row(D_MODEL)]
        + [_full(a.shape) for a in weights],
        out_specs=[row(D_MODEL), pl.BlockSpec(tile_rows, lambda i: (i, 0, 0, 0)), row(LANES)],
        out_shape=[jax.ShapeDtypeStruct((n, D_MODEL), F32),
                   jax.ShapeDtypeStruct((n // SUBLANES,) + PACKED_TILE, jnp.uint32),
                   jax.ShapeDtypeStruct((n, LANES), F32)],
        scratch_shapes=[pltpu.VMEM((SUBLANES, tm + CONV_HALO, CONV_W), F32)],
        compiler_params=_params("parallel"),
        name="merge_router",
    )(attn, u, u, gates, x2, *weights)


def _rank_kernel(route_ref, pos_out, meta_out, r1_sc, r2_sc):
    tb = route_ref.shape[0]
    lane = lax.broadcasted_iota(jnp.int32, (RANK_CHUNK, LANES), 1).astype(F32)
    rr = lax.broadcasted_iota(jnp.int32, (RANK_CHUNK, RANK_CHUNK), 0)
    cc = lax.broadcasted_iota(jnp.int32, (RANK_CHUNK, RANK_CHUNK), 1)
    lower = jnp.where(rr > cc, 1.0, 0.0).astype(BF16)
    run = jnp.zeros((1, LANES), F32)
    for choice, sc in ((0, r1_sc), (1, r2_sc)):
        for ci in range(tb // RANK_CHUNK):
            rows = slice(ci * RANK_CHUNK, (ci + 1) * RANK_CHUNK)
            oh = jnp.where(lane == route_ref[rows, choice:choice + 1], 1.0, 0.0)
            sc[rows, :] = run + _dot(lower, oh.astype(BF16))
            run = run + jnp.sum(oh, axis=0, keepdims=True)
    cnt = run
    cnt8 = jnp.ceil(cnt * (1.0 / SUBLANES)) * SUBLANES
    ur = lax.broadcasted_iota(jnp.int32, (LANES, LANES), 0)
    uc = lax.broadcasted_iota(jnp.int32, (LANES, LANES), 1)
    upper = jnp.where(ur < uc, 1.0, 0.0)
    off = _dot(jnp.broadcast_to(cnt8, (SUBLANES, LANES)), upper, precision=lax.Precision.HIGHEST)[0:1, :]
    lane_t = lax.broadcasted_iota(jnp.int32, (tb, LANES), 1)
    lane_tf = lane_t.astype(F32)
    p1 = jnp.sum(jnp.where(lane_tf == route_ref[:, 0:1], off + r1_sc[...], 0.0), axis=-1, keepdims=True)
    p2 = jnp.sum(jnp.where(lane_tf == route_ref[:, 1:2], off + r2_sc[...], 0.0), axis=-1, keepdims=True)
    q1 = p1 + (SUBLANES * ROW_PIECES - SUBLANES) * jnp.floor(p1 * (1.0 / SUBLANES))
    q2 = p2 + (SUBLANES * ROW_PIECES - SUBLANES) * jnp.floor(p2 * (1.0 / SUBLANES))
    pos = jnp.where(lane_t == 0, q1, jnp.where(lane_t == 1, q2, 0.0))
    pos_out[...] = pos.astype(jnp.int32)
    row8 = lax.broadcasted_iota(jnp.int32, (SUBLANES, LANES), 0)
    meta = jnp.where(row8 == 0, jnp.broadcast_to(off, (SUBLANES, LANES)),
                     jnp.where(row8 == 1, jnp.broadcast_to(cnt, (SUBLANES, LANES)), 0.0))
    meta_out[...] = meta.astype(jnp.int32)


def _rank(route, b, t):
    return pl.pallas_call(
        _rank_kernel,
        grid=(b,),
        in_specs=[pl.BlockSpec((t, LANES), lambda i: (i, 0))],
        out_specs=[pl.BlockSpec((t, LANES), lambda i: (i, 0)), pl.BlockSpec((SUBLANES, LANES), lambda i: (i, 0))],
        out_shape=[jax.ShapeDtypeStruct((b * t, LANES), jnp.int32),
                   jax.ShapeDtypeStruct((b * SUBLANES, LANES), jnp.int32)],
        scratch_shapes=[pltpu.VMEM((t, LANES), F32), pltpu.VMEM((t, LANES), F32)],
        compiler_params=_params("parallel"),
        name="rank",
    )(route)


def _sorted_rows(t):
    return 2 * t + N_EXPERTS * SUBLANES + ROW_TILE


def _dispatch_kernel(q_ref, h2p_ref, xs_out):
    t = h2p_ref.shape[0] // ROW_PIECES
    xs_out[...] = jnp.zeros(xs_out.shape, xs_out.dtype)

    def body(g, carry):
        src = g * (SUBLANES * ROW_PIECES)
        qi = g * (2 * SUBLANES)
        for s in range(SUBLANES):
            row = h2p_ref[_row_ds(src + s), :]
            xs_out[_row_ds(q_ref[qi + 2 * s]), :] = row
            xs_out[_row_ds(q_ref[qi + 2 * s + 1]), :] = row
        return carry

    lax.fori_loop(0, t // SUBLANES, body, 0)


def _dispatch(q_flat, h2p_flat, b, t):
    flat_rows = _sorted_rows(t) * ROW_PIECES
    return pl.pallas_call(
        _dispatch_kernel,
        grid=(b,),
        in_specs=[pl.BlockSpec((2 * t,), lambda i: (i,), memory_space=pltpu.SMEM),
                  pl.BlockSpec((t * ROW_PIECES, LANES), lambda i: (i, 0))],
        out_specs=pl.BlockSpec((flat_rows, LANES), lambda i: (i, 0)),
        out_shape=jax.ShapeDtypeStruct((b * flat_rows, LANES), jnp.uint32),
        compiler_params=_params("parallel"),
        name="dispatch",
    )(q_flat, h2p_flat)


def _expert_kernel(off_ref, cnt_ref, xs_ref, wgu_ref, wd_ref, ys_out):
    bi = pl.program_id(0)
    ep = pl.program_id(1)

    @pl.when(ep == 0)
    def _():
        ys_out[...] = jnp.zeros(ys_out.shape, ys_out.dtype)

    tile_groups = ROW_TILE // SUBLANES
    spare_g = xs_ref.shape[1] - tile_groups
    g_offs, tiles = [], []
    for k in range(EXPERTS_PER_STEP):
        idx = bi * N_EXPERTS + ep * EXPERTS_PER_STEP + k
        g_offs.append(lax.shift_right_logical(off_ref[idx], SUBLANES.bit_length() - 1))
        tiles.append(lax.shift_right_logical(cnt_ref[idx] + (ROW_TILE - 1), ROW_TILE.bit_length() - 1))
    half = D_MODEL // 2

    def step(j, carry):
        for k in range(EXPERTS_PER_STEP):
            i = tiles[k] - 1 - j
            g0 = jnp.where(i >= 0, g_offs[k] + i * tile_groups, spare_g)
            lo, hi = _unpack_words(_load_tile_rows(xs_ref, g0, ROW_TILE, lead=(0,)))
            xb = jnp.concatenate([lo.astype(BF16), hi.astype(BF16)], axis=1)
            au = _dot(xb, wgu_ref[k])
            a = au[:, :EXPERT_FF]
            mid = (a * _sigmoid(a) * au[:, EXPERT_FF:]).astype(BF16)
            y = _dot(mid, wd_ref[k])
            _store_tile_rows(ys_out, g0, _pack_words(y[:, :half], y[:, half:]), lead=(0,))
        return carry

    lax.fori_loop(0, functools.reduce(jnp.maximum, tiles), step, 0)


def _expert_ffn(off, cnt, xs, w, b, t):
    groups = _sorted_rows(t) // SUBLANES
    rows_spec = pl.BlockSpec((1, groups) + PACKED_TILE, lambda bi, e, *_: (bi, 0, 0, 0, 0))
    grid_spec = pltpu.PrefetchScalarGridSpec(
        num_scalar_prefetch=2,
        grid=(b, N_EXPERTS // EXPERTS_PER_STEP),
        in_specs=[rows_spec,
                  pl.BlockSpec((EXPERTS_PER_STEP, D_MODEL, 2 * EXPERT_FF), lambda bi, e, *_: (e, 0, 0)),
                  pl.BlockSpec((EXPERTS_PER_STEP, EXPERT_FF, D_MODEL), lambda bi, e, *_: (e, 0, 0))],
        out_specs=rows_spec,
    )
    return pl.pallas_call(
        _expert_kernel,
        grid_spec=grid_spec,
        out_shape=jax.ShapeDtypeStruct((b, groups) + PACKED_TILE, jnp.uint32),
        compiler_params=_params("parallel", "arbitrary"),
        name="expert_ffn",
    )(off, cnt, xs, w["wgu"], w["wd"])


def _combine_kernel(q_ref, ys_ref, x1_ref, route_ref, out_ref, g1, g2):
    tc = x1_ref.shape[0]
    groups = tc // SUBLANES
    base = pl.program_id(1) * (2 * tc)

    def body(g, carry):
        dst = g * (SUBLANES * ROW_PIECES)
        qi = base + g * (2 * SUBLANES)
        for s in range(SUBLANES):
            g1[_row_ds(dst + s), :] = ys_ref[_row_ds(q_ref[qi + 2 * s]), :]
            g2[_row_ds(dst + s), :] = ys_ref[_row_ds(q_ref[qi + 2 * s + 1]), :]
        return carry

    lax.fori_loop(0, groups, body, 0)
    w1 = route_ref[:, 2:3]
    w2 = route_ref[:, 3:4]
    half = D_MODEL // 2
    for c in range(ROW_PIECES):
        tile_of = lambda ref: jnp.concatenate(
            [ref[(g * ROW_PIECES + c) * SUBLANES:(g * ROW_PIECES + c + 1) * SUBLANES, :] for g in range(groups)], axis=0)
        lo1, hi1 = _unpack_words(tile_of(g1))
        lo2, hi2 = _unpack_words(tile_of(g2))
        cols = slice(c * LANES, (c + 1) * LANES)
        cols_hi = slice(half + c * LANES, half + (c + 1) * LANES)
        out_ref[:, cols] = x1_ref[:, cols] + (w1 * lo1 + w2 * lo2)
        out_ref[:, cols_hi] = x1_ref[:, cols_hi] + (w1 * hi1 + w2 * hi2)


def _combine(q_flat, ys_flat, x1, route, b, t):
    flat_rows = _sorted_rows(t) * ROW_PIECES
    tc = COMBINE_TILE
    nt = t // tc
    return pl.pallas_call(
        _combine_kernel,
        grid=(b, nt),
        in_specs=[pl.BlockSpec((2 * t,), lambda bi, j: (bi,), memory_space=pltpu.SMEM),
                  pl.BlockSpec((flat_rows, LANES), lambda bi, j: (bi, 0)),
                  pl.BlockSpec((tc, D_MODEL), lambda bi, j: (bi * nt + j, 0)),
                  pl.BlockSpec((tc, LANES), lambda bi, j: (bi * nt + j, 0))],
        out_specs=pl.BlockSpec((tc, D_MODEL), lambda bi, j: (bi * nt + j, 0)),
        out_shape=jax.ShapeDtypeStruct((b * t, D_MODEL), F32),
        scratch_shapes=[pltpu.VMEM((tc * ROW_PIECES, LANES), jnp.uint32),
                        pltpu.VMEM((tc * ROW_PIECES, LANES), jnp.uint32)],
        compiler_params=_params("parallel", "arbitrary"),
        name="combine",
    )(q_flat, ys_flat, x1, route)


def _head_slots(wmat, width):
    k = wmat.shape[0]
    w3 = wmat.reshape(k, N_HEADS, width)
    return jnp.pad(w3, ((0, 0), (0, 0), (0, HEAD_SLOT - width))).reshape(k, N_HEADS * HEAD_SLOT)


def _gain3(g):
    gp = jnp.pad(g.astype(F32), (0, HEAD_SLOT - QK_DIM))
    return jnp.stack([gp, jnp.roll(gp, ROPE_HALF), jnp.roll(gp, -ROPE_HALF)])


def _prepare_weights(g_mix, w_in, g_q_lora, w_uq, g_kv_lora, w_ukv, g_qk_q, g_qk_k, w_o_mla, w_dw, b_dw,
                     g_conv_ln, b_conv_ln, w_conv_out, b_gates, w_out, g_ffn, w_group_router, b_group_router,
                     w_expert_router, b_expert_router, w_e_gate, w_e_up, w_e_down):
    row = lambda a: a.astype(F32).reshape(1, -1)
    s0, s1, s2, s3 = Q_LORA, Q_LORA + KV_LORA, Q_LORA + KV_LORA + ROPE, Q_LORA + KV_LORA + ROPE + 2 * CONV_W
    w = {}
    w["g_mix"] = row(g_mix)
    w["wa"] = jnp.concatenate([w_in[:, :s1], jnp.pad(w_in[:, s1:s2], ((0, 0), (0, LANES - ROPE)))], axis=1).astype(BF16)
    w["wc"] = w_in[:, s2:s3].astype(BF16)
    w["wg"] = w_in[:, s3:].astype(BF16)
    w["b_gates"] = row(b_gates)
    w["g_q_lora"] = row(g_q_lora)
    w["wuq"] = _head_slots(w_uq, QK_DIM).astype(BF16)
    w["g_kv_lora"] = row(g_kv_lora)
    kv3 = w_ukv.reshape(KV_LORA, N_HEADS, NOPE + V_DIM)
    w["wk"] = _head_slots(kv3[:, :, :NOPE].reshape(KV_LORA, N_HEADS * NOPE), NOPE).astype(BF16)
    v3 = kv3[:, :, NOPE:]
    zeros = jnp.zeros_like(v3)
    odd = (jnp.arange(N_HEADS) % 2 == 1)[None, :, None]
    wv = jnp.concatenate([jnp.where(odd, zeros, v3), jnp.where(odd, v3, zeros)], axis=-1)
    w["wv"] = wv.reshape(KV_LORA, N_HEADS * HEAD_SLOT).astype(BF16)
    src = jnp.arange(LANES)[:, None]
    dst = jnp.arange(N_HEADS * HEAD_SLOT)[None, :]
    w["rrep"] = ((src < ROPE) & ((dst % HEAD_SLOT) == (src + NOPE))).astype(BF16)
    w["gq3"] = _gain3(g_qk_q)
    w["gk3"] = _gain3(g_qk_k)
    w["wo"] = w_o_mla.astype(BF16)
    w["w_dw"] = jnp.pad(w_dw.astype(F32), ((0, 32 - CONV_K), (0, 0)))
    w["b_dw"] = row(b_dw)
    w["g_conv_ln"] = row(g_conv_ln)
    w["b_conv_ln"] = row(b_conv_ln)
    w["wco"] = w_conv_out.astype(BF16)
    w["wout"] = w_out.astype(BF16)
    w["g_ffn"] = row(g_ffn)
    pad_r = LANES - N_EXPERTS - N_GROUPS
    wr = jnp.concatenate([w_expert_router, w_group_router, jnp.zeros((D_MODEL, pad_r), F32)], axis=1).astype(F32)
    w["wr_hi"] = wr.astype(BF16)
    w["wr_lo"] = (wr - w["wr_hi"].astype(F32)).astype(BF16)
    w["br"] = jnp.concatenate([b_expert_router, b_group_router, jnp.zeros((pad_r,), F32)]).astype(F32).reshape(1, LANES)
    w["wgu"] = jnp.concatenate([w_e_gate.astype(BF16), w_e_up.astype(BF16)], axis=-1)
    w["wd"] = w_e_down.astype(BF16)
    return w


def kernel(x, positions, g_mix, w_in, g_q_lora, w_uq, g_kv_lora, w_ukv, g_qk_q, g_qk_k, w_o_mla, w_dw, b_dw,
           g_conv_ln, b_conv_ln, w_conv_out, b_gates, w_out, g_ffn, w_group_router, b_group_router,
           w_expert_router, b_expert_router, w_e_gate, w_e_up, w_e_down):
    b, t, d = x.shape
    assert d == D_MODEL and g_mix.shape[0] == 1
    assert t % ATTN_TILE == 0 and t % RANK_CHUNK == 0 and t % COMBINE_TILE == 0
    tm = min(TOKEN_BLOCK, t)
    assert t % tm == 0 and tm % CONV_HALO == 0
    n = b * t
    w = _prepare_weights(g_mix[0], w_in[0], g_q_lora[0], w_uq[0], g_kv_lora[0], w_ukv[0], g_qk_q[0], g_qk_k[0],
                         w_o_mla[0], w_dw[0], b_dw[0], g_conv_ln[0], b_conv_ln[0], w_conv_out[0], b_gates[0],
                         w_out[0], g_ffn[0], w_group_router[0], b_group_router[0], w_expert_router[0],
                         b_expert_router[0], w_e_gate[0], w_e_up[0], w_e_down[0])
    x2 = x.reshape(n, d)
    cos, sin = _rope_tables(positions)
    q, k, v, u, gates = _pre_attention(x2, cos, sin, w, tm)
    attn = _attention(q, k, v, b, t)
    x1, h2p, route = _merge(attn, u, gates, x2, w, t, tm)
    pos, meta = _rank(route, b, t)
    q_flat = pos[:, :2].reshape(n * 2)
    meta3 = meta.reshape(b, SUBLANES, LANES)
    off = meta3[:, 0, :N_EXPERTS].reshape(b * N_EXPERTS)
    cnt = meta3[:, 1, :N_EXPERTS].reshape(b * N_EXPERTS)
    groups = _sorted_rows(t) // SUBLANES
    xs = _dispatch(q_flat, h2p.reshape(n * ROW_PIECES, LANES), b, t)
    ys = _expert_ffn(off, cnt, xs.reshape((b, groups) + PACKED_TILE), w, b, t)
    out = _combine(q_flat, ys.reshape(b * groups * SUBLANES * ROW_PIECES, LANES), x1, route, b, t)
    return out.reshape(b, t, d)
```

```python
import functools

import jax
import jax.numpy as jnp
import numpy as np
from jax import lax
from jax.experimental import pallas as pl
from jax.experimental.pallas import tpu as pltpu

D_MODEL = 1024
N_HEADS = 8
NOPE = 64
ROPE = 32
ROPE_HALF = ROPE // 2
QK_DIM = NOPE + ROPE
V_DIM = 64
Q_LORA = 256
KV_LORA = 128
CONV_W = 512
CONV_K = 31
N_GROUPS = 4
EXPERTS_PER_GROUP = 8
N_EXPERTS = N_GROUPS * EXPERTS_PER_GROUP
EXPERT_FF = 256
EPS = 1e-6
ROPE_THETA = 10000.0
CHUNK = 64

LANES = 128
SUBLANES = 8
HEAD_SLOT = LANES
VMEM_LIMIT_BYTES = 56 * 1024 * 1024

TOKEN_BLOCK = 512
ATTN_TILE = 256
CONV_HALO = 32
ROW_TILE = 256
EXPERTS_PER_STEP = 2
COMBINE_TILE = 256
RANK_CHUNK = 256
SUB_ROWS = 256
CONV_CHUNK = 32

F32 = jnp.float32
BF16 = jnp.bfloat16
NEG_INF = float("-inf")
LOG2E = 1.4426950408889634


def _dot(a, b, **kw):
    return jnp.dot(a, b, preferred_element_type=F32, **kw)


def _rms(x, g):
    return x * lax.rsqrt(jnp.mean(x * x, axis=-1, keepdims=True) + EPS) * g


def _sigmoid(x):
    return 1.0 / (1.0 + jnp.exp(-x))


def _pack_words(lo, hi):
    return pltpu.pack_elementwise([lo, hi], packed_dtype=BF16)


def _unpack_words(wd):
    lo = pltpu.unpack_elementwise(wd, index=0, packed_dtype=BF16, unpacked_dtype=F32)
    hi = pltpu.unpack_elementwise(wd, index=1, packed_dtype=BF16, unpacked_dtype=F32)
    return lo, hi


PACKED_TILE = (D_MODEL // 2 // LANES, SUBLANES, LANES)
ROW_PIECES = PACKED_TILE[0]


def _store_tile_rows(ref, g0, words, lead=()):
    groups = words.shape[0] // SUBLANES
    for c in range(ROW_PIECES):
        piece = words[:, c * LANES:(c + 1) * LANES].reshape(groups, SUBLANES, LANES)
        ref[lead + (pl.ds(g0, groups), c)] = piece


def _load_tile_rows(ref, g0, rows, lead=()):
    groups = rows // SUBLANES
    pieces = [ref[lead + (pl.ds(g0, groups), c)].reshape(rows, LANES) for c in range(ROW_PIECES)]
    return jnp.concatenate(pieces, axis=1)


def _row_ds(q):
    return pl.ds(q, ROW_PIECES, stride=SUBLANES)


def _params(*sem):
    return pltpu.CompilerParams(dimension_semantics=sem, vmem_limit_bytes=VMEM_LIMIT_BYTES)


def _full(shape):
    nd = len(shape)
    return pl.BlockSpec(shape, lambda *_: (0,) * nd)


def _rope_table_kernel(pos_ref, freq_ref, cos_ref, sin_ref):
    ang = pos_ref[...].astype(F32) * freq_ref[...]
    cos_ref[...] = jnp.cos(ang)
    sin_ref[...] = jnp.sin(ang)


def _rope_tables(positions):
    n = positions.size
    per_row = LANES // ROPE_HALF
    rows = n // per_row
    pos_rep = jnp.repeat(positions.reshape(rows, per_row), ROPE_HALF, axis=1)
    inv_freq = ROPE_THETA ** (-jnp.arange(ROPE_HALF, dtype=F32) / ROPE_HALF)
    freq = jnp.tile(inv_freq, per_row).reshape(1, LANES)
    rb = min(rows, 512)
    cos, sin = pl.pallas_call(
        _rope_table_kernel,
        grid=(rows // rb,),
        in_specs=[pl.BlockSpec((rb, LANES), lambda i: (i, 0)), _full((1, LANES))],
        out_specs=[pl.BlockSpec((rb, LANES), lambda i: (i, 0))] * 2,
        out_shape=[jax.ShapeDtypeStruct((rows, LANES), F32)] * 2,
        compiler_params=_params("parallel"),
        name="rope_tables",
    )(pos_rep, freq)
    return cos.reshape(n, ROPE_HALF), sin.reshape(n, ROPE_HALF)


def _pre_attention_kernel(x_ref, cos_ref, sin_ref, gmix_ref, wa_ref, wc_ref, wg_ref, bg_ref,
                          gql_ref, wuq_ref, gkv_ref, wk_ref, wv_ref, rrep_ref, gq_ref, gk_ref,
                          q_out, k_out, v_out, u_out, gate_out):
    tm = x_ref.shape[0]
    hb = _rms(x_ref[...], gmix_ref[...]).astype(BF16)

    a = _dot(hb, wa_ref[...])
    qdn = _rms(a[:, :Q_LORA], gql_ref[...]).astype(BF16)
    kvn = _rms(a[:, Q_LORA:Q_LORA + KV_LORA], gkv_ref[...]).astype(BF16)
    kr = a[:, Q_LORA + KV_LORA:].astype(BF16)
    qf = _dot(qdn, wuq_ref[...])
    kf = _dot(kvn, wk_ref[...]) + _dot(kr, rrep_ref[...])
    v_out[...] = _dot(kvn, wv_ref[...]).astype(BF16)

    cos = cos_ref[...]
    sin = sin_ref[...]
    z = lambda w: jnp.zeros((tm, w), F32)
    c_tab = jnp.concatenate([jnp.ones((tm, NOPE), F32), cos, cos, z(HEAD_SLOT - QK_DIM)], axis=-1)
    s_up = jnp.concatenate([z(NOPE + ROPE_HALF), sin, z(HEAD_SLOT - QK_DIM)], axis=-1)
    s_dn = jnp.concatenate([z(NOPE), -sin, z(HEAD_SLOT - NOPE - ROPE_HALF)], axis=-1)

    def norm_rope(f, g_ref, out, scale):
        ta = c_tab * g_ref[0:1, :]
        tu = s_up * g_ref[1:2, :]
        td = s_dn * g_ref[2:3, :]
        for h in range(N_HEADS):
            xs = f[:, h * HEAD_SLOT:(h + 1) * HEAD_SLOT]
            ss = jnp.sum(xs * xs, axis=-1, keepdims=True)
            s = lax.rsqrt(ss * (1.0 / QK_DIM) + EPS) * scale
            y = xs * ta + pltpu.roll(xs, ROPE_HALF, 1) * tu + pltpu.roll(xs, HEAD_SLOT - ROPE_HALF, 1) * td
            out[:, h * HEAD_SLOT:(h + 1) * HEAD_SLOT] = (y * s).astype(BF16)

    norm_rope(qf, gq_ref, q_out, LOG2E * QK_DIM ** -0.5)
    norm_rope(kf, gk_ref, k_out, 1.0)

    c = _dot(hb, wc_ref[...])
    u_out[...] = c[:, :CONV_W] * _sigmoid(c[:, CONV_W:])
    gate_out[...] = _sigmoid(_dot(hb, wg_ref[...]) + bg_ref[...]).astype(BF16)


def _pre_attention(x2, cos, sin, w, tm):
    n = x2.shape[0]
    row = lambda width: pl.BlockSpec((tm, width), lambda i: (i, 0))
    hs = N_HEADS * HEAD_SLOT
    weights = [w["g_mix"], w["wa"], w["wc"], w["wg"], w["b_gates"], w["g_q_lora"], w["wuq"],
               w["g_kv_lora"], w["wk"], w["wv"], w["rrep"], w["gq3"], w["gk3"]]
    return pl.pallas_call(
        _pre_attention_kernel,
        grid=(n // tm,),
        in_specs=[row(D_MODEL), row(ROPE_HALF), row(ROPE_HALF)] + [_full(a.shape) for a in weights],
        out_specs=[row(hs), row(hs), row(N_HEADS * V_DIM), row(CONV_W), row(2 * D_MODEL)],
        out_shape=[jax.ShapeDtypeStruct((n, hs), BF16), jax.ShapeDtypeStruct((n, hs), BF16),
                   jax.ShapeDtypeStruct((n, N_HEADS * V_DIM), BF16), jax.ShapeDtypeStruct((n, CONV_W), F32),
                   jax.ShapeDtypeStruct((n, 2 * D_MODEL), BF16)],
        compiler_params=_params("parallel"),
        name="pre_attention",
    )(x2, cos, sin, *weights)


def _attention_kernel(q_ref, k_ref, v_ref, o_ref, kbd, vbd):
    t = ATTN_TILE
    hk = HEAD_SLOT
    seq = q_ref.shape[0]
    lane = lax.broadcasted_iota(jnp.int32, (hk, HEAD_SLOT), 1)
    zero = jnp.zeros((hk, HEAD_SLOT), BF16)
    for c in range(seq // hk):
        rows = slice(c * hk, (c + 1) * hk)
        top = slice(2 * c * hk, (2 * c + 1) * hk)
        bot = slice((2 * c + 1) * hk, (2 * c + 2) * hk)
        kbd[top, :HEAD_SLOT] = k_ref[rows, :HEAD_SLOT]
        kbd[top, HEAD_SLOT:] = zero
        kbd[bot, :HEAD_SLOT] = zero
        kbd[bot, HEAD_SLOT:] = k_ref[rows, HEAD_SLOT:]
        vp = v_ref[rows, :]
        vbd[top, :] = jnp.where(lane < V_DIM, vp, zero)
        vbd[bot, :] = jnp.where(lane >= V_DIM, vp, zero)

    qrow = lax.broadcasted_iota(jnp.int32, (t, hk), 0) // CHUNK
    kcol = lax.broadcasted_iota(jnp.int32, (t, hk), 1)
    diag_masks = [(kcol + h2 * hk) // CHUNK <= qrow for h2 in range(t // hk)]
    out_lane = lax.broadcasted_iota(jnp.int32, (t, HEAD_SLOT), 1)
    for i in range(seq // t):
        rows = slice(i * t, (i + 1) * t)
        n_tiles = 2 * (i + 1) * t // hk
        s = lax.dot_general(q_ref[rows, :], kbd[0:n_tiles * hk, :], (((1,), (1,)), ((), ())),
                            preferred_element_type=F32)
        tiles = [s[:, j * hk:(j + 1) * hk] for j in range(n_tiles)]
        first_diag = n_tiles - 2 * (t // hk)
        for j in range(first_diag, n_tiles):
            tiles[j] = jnp.where(diag_masks[(j - first_diag) // 2], tiles[j], NEG_INF)
        probs = [None] * n_tiles
        inv_l = []
        for h in range(2):
            mine = range(h, n_tiles, 2)
            m = jnp.max(functools.reduce(jnp.maximum, [tiles[j] for j in mine]), axis=-1, keepdims=True)
            for j in mine:
                probs[j] = jnp.exp2(tiles[j] - m)
            l = jnp.sum(functools.reduce(jnp.add, [probs[j] for j in mine]), axis=-1, keepdims=True)
            inv_l.append(1.0 / l)
        p = jnp.concatenate(probs, axis=1).astype(BF16)
        o = _dot(p, vbd[0:n_tiles * hk, :])
        o_ref[rows, :] = (o * jnp.where(out_lane < V_DIM, inv_l[0], inv_l[1])).astype(BF16)


def _attention(q, k, v, b, t):
    hs2 = 2 * HEAD_SLOT
    spec = pl.BlockSpec((t, hs2), lambda bi, p: (bi, p))
    vspec = pl.BlockSpec((t, 2 * V_DIM), lambda bi, p: (bi, p))
    return pl.pallas_call(
        _attention_kernel,
        grid=(b, N_HEADS // 2),
        in_specs=[spec, spec, vspec],
        out_specs=pl.BlockSpec((t, 2 * V_DIM), lambda bi, p: (bi, p)),
        out_shape=jax.ShapeDtypeStruct((b * t, N_HEADS * V_DIM), BF16),
        scratch_shapes=[pltpu.VMEM((2 * t, hs2), BF16), pltpu.VMEM((2 * t, 2 * V_DIM), BF16)],
        compiler_params=_params("parallel", "parallel"),
        name="attention",
    )(q, k, v)


def _merge_kernel(blocks_per_seq, attn_ref, u_ref, halo_ref, gate_ref, x_ref, wo_ref, wdw_ref, bdw_ref,
                  gln_ref, bln_ref, wco_ref, wout_ref, gffn_ref, wrh_ref, wrl_ref, br_ref,
                  x1_out, h2p_out, route_out, sh_ref):
    tm = x_ref.shape[0]
    first = (pl.program_id(0) % blocks_per_seq) == 0
    halo = halo_ref[...]
    sh_ref[0, 0:CONV_HALO, :] = jnp.where(first, jnp.zeros_like(halo), halo)
    sh_ref[0, CONV_HALO:, :] = u_ref[...]
    span = tm + CONV_HALO - SUBLANES
    for s in range(1, SUBLANES):
        sh_ref[s, 0:span, :] = sh_ref[0, s:s + span, :]

    base = CONV_HALO - (CONV_K - 1)
    half = D_MODEL // 2
    for sb in range(tm // SUB_ROWS):
        rows = slice(sb * SUB_ROWS, (sb + 1) * SUB_ROWS)
        parts = []
        for ck in range(SUB_ROWS // CONV_CHUNK):
            c0 = sb * SUB_ROWS + ck * CONV_CHUNK
            acc = jnp.broadcast_to(bdw_ref[...], (CONV_CHUNK, CONV_W))
            for j in range(CONV_K):
                s, a = (base + j) % SUBLANES, (base + j) // SUBLANES * SUBLANES
                acc = acc + sh_ref[s, c0 + a:c0 + a + CONV_CHUNK, :] * wdw_ref[j:j + 1, :]
            parts.append(acc)
        conv = jnp.concatenate(parts, axis=0)
        mu = jnp.mean(conv, axis=-1, keepdims=True)
        cen = conv - mu
        var = jnp.mean(cen * cen, axis=-1, keepdims=True)
        ln = cen * lax.rsqrt(var + EPS) * gln_ref[...] + bln_ref[...]
        ub = (ln * _sigmoid(ln)).astype(BF16)
        y_b = _dot(ub, wco_ref[...])
        y_a = _dot(attn_ref[rows, :], wo_ref[...])
        merged = gate_ref[rows, :D_MODEL].astype(F32) * y_a + gate_ref[rows, D_MODEL:].astype(F32) * y_b
        x1 = x_ref[rows, :] + _dot(merged.astype(BF16), wout_ref[...])
        x1_out[rows, :] = x1
        h2 = _rms(x1, gffn_ref[...])
        _store_tile_rows(h2p_out, sb * (SUB_ROWS // SUBLANES), _pack_words(h2[:, :half], h2[:, half:]))
        h2_hi = h2.astype(BF16)
        h2_lo = (h2 - h2_hi.astype(F32)).astype(BF16)
        logits = (_dot(h2_hi, wrh_ref[...]) + _dot(h2_lo, wrh_ref[...]) + _dot(h2_hi, wrl_ref[...])
                  + br_ref[...])
        route_out[rows, :] = _route(logits)


def _route(logits):
    lane_i = lax.broadcasted_iota(jnp.int32, logits.shape, 1)
    lane = lane_i.astype(F32)
    big = float(LANES)
    is_g = (lane_i >= N_EXPERTS) & (lane_i < N_EXPERTS + N_GROUPS)
    gl = jnp.where(is_g, logits, NEG_INF)
    gmax = jnp.max(gl, axis=-1, keepdims=True)
    gidx = jnp.min(jnp.where(gl == gmax, lane - N_EXPERTS, big), axis=-1, keepdims=True)
    g_p = 1.0 / jnp.sum(jnp.exp(gl - gmax), axis=-1, keepdims=True)
    lane_group = (lane_i // EXPERTS_PER_GROUP).astype(F32)
    valid = (lane_i < N_EXPERTS) & (lane_group == gidx)
    el = jnp.where(valid, logits, NEG_INF)
    ee = jnp.exp(el - jnp.max(el, axis=-1, keepdims=True))
    within = jnp.where(valid, ee / jnp.sum(ee, axis=-1, keepdims=True), -1.0)
    p1 = jnp.max(within, axis=-1, keepdims=True)
    i1 = jnp.min(jnp.where(within == p1, lane, big), axis=-1, keepdims=True)
    within2 = jnp.where(lane == i1, -1.0, within)
    p2 = jnp.max(within2, axis=-1, keepdims=True)
    i2 = jnp.min(jnp.where(within2 == p2, lane, big), axis=-1, keepdims=True)
    psum = p1 + p2
    w1 = g_p * (p1 / psum)
    w2 = g_p * (p2 / psum)
    route = jnp.where(lane_i == 0, i1, 0.0)
    route = jnp.where(lane_i == 1, i2, route)
    route = jnp.where(lane_i == 2, w1, route)
    return jnp.where(lane_i == 3, w2, route)


def _merge(attn, u, gates, x2, w, t, tm):
    n = x2.shape[0]
    bps = t // tm
    row = lambda width: pl.BlockSpec((tm, width), lambda i: (i, 0))
    hpb = tm // CONV_HALO
    halo = pl.BlockSpec((CONV_HALO, CONV_W), lambda i: (jnp.maximum(i * hpb - 1, 0), 0))
    weights = [w["wo"], w["w_dw"], w["b_dw"], w["g_conv_ln"], w["b_conv_ln"], w["wco"], w["wout"],
               w["g_ffn"], w["wr_hi"], w["wr_lo"], w["br"]]
    tile_rows = (tm // SUBLANES,) + PACKED_TILE
    return pl.pallas_call(
        functools.partial(_merge_kernel, bps),
        grid=(n // tm,),
        in_specs=[row(N_HEADS * V_DIM), row(CONV_W), halo, row(2 * D_MODEL), row(D_MODEL)]
        + [_full(a.shape) for a in weights],
        out_specs=[row(D_MODEL), pl.BlockSpec(tile_rows, lambda i: (i, 0, 0, 0)), row(LANES)],
        out_shape=[jax.ShapeDtypeStruct((n, D_MODEL), F32),
                   jax.ShapeDtypeStruct((n // SUBLANES,) + PACKED_TILE, jnp.uint32),
                   jax.ShapeDtypeStruct((n, LANES), F32)],
        scratch_shapes=[pltpu.VMEM((SUBLANES, tm + CONV_HALO, CONV_W), F32)],
        compiler_params=_params("parallel"),
        name="merge_router",
    )(attn, u, u, gates, x2, *weights)


def _rank_kernel(route_ref, pos_out, meta_out, r1_sc, r2_sc):
    tb = route_ref.shape[0]
    lane = lax.broadcasted_iota(jnp.int32, (RANK_CHUNK, LANES), 1).astype(F32)
    rr = lax.broadcasted_iota(jnp.int32, (RANK_CHUNK, RANK_CHUNK), 0)
    cc = lax.broadcasted_iota(jnp.int32, (RANK_CHUNK, RANK_CHUNK), 1)
    lower = jnp.where(rr > cc, 1.0, 0.0).astype(BF16)
    run = jnp.zeros((1, LANES), F32)
    for choice, sc in ((0, r1_sc), (1, r2_sc)):
        for ci in range(tb // RANK_CHUNK):
            rows = slice(ci * RANK_CHUNK, (ci + 1) * RANK_CHUNK)
            oh = jnp.where(lane == route_ref[rows, choice:choice + 1], 1.0, 0.0)
            sc[rows, :] = run + _dot(lower, oh.astype(BF16))
            run = run + jnp.sum(oh, axis=0, keepdims=True)
    cnt = run
    cnt8 = jnp.ceil(cnt * (1.0 / SUBLANES)) * SUBLANES
    ur = lax.broadcasted_iota(jnp.int32, (LANES, LANES), 0)
    uc = lax.broadcasted_iota(jnp.int32, (LANES, LANES), 1)
    upper = jnp.where(ur < uc, 1.0, 0.0)
    off = _dot(jnp.broadcast_to(cnt8, (SUBLANES, LANES)), upper, precision=lax.Precision.HIGHEST)[0:1, :]
    lane_t = lax.broadcasted_iota(jnp.int32, (tb, LANES), 1)
    lane_tf = lane_t.astype(F32)
    p1 = jnp.sum(jnp.where(lane_tf == route_ref[:, 0:1], off + r1_sc[...], 0.0), axis=-1, keepdims=True)
    p2 = jnp.sum(jnp.where(lane_tf == route_ref[:, 1:2], off + r2_sc[...], 0.0), axis=-1, keepdims=True)
    q1 = p1 + (SUBLANES * ROW_PIECES - SUBLANES) * jnp.floor(p1 * (1.0 / SUBLANES))
    q2 = p2 + (SUBLANES * ROW_PIECES - SUBLANES) * jnp.floor(p2 * (1.0 / SUBLANES))
    pos = jnp.where(lane_t == 0, q1, jnp.where(lane_t == 1, q2, 0.0))
    pos_out[...] = pos.astype(jnp.int32)
    row8 = lax.broadcasted_iota(jnp.int32, (SUBLANES, LANES), 0)
    meta = jnp.where(row8 == 0, jnp.broadcast_to(off, (SUBLANES, LANES)),
                     jnp.where(row8 == 1, jnp.broadcast_to(cnt, (SUBLANES, LANES)), 0.0))
    meta_out[...] = meta.astype(jnp.int32)


def _rank(route, b, t):
    return pl.pallas_call(
        _rank_kernel,
        grid=(b,),
        in_specs=[pl.BlockSpec((t, LANES), lambda i: (i, 0))],
        out_specs=[pl.BlockSpec((t, LANES), lambda i: (i, 0)), pl.BlockSpec((SUBLANES, LANES), lambda i: (i, 0))],
        out_shape=[jax.ShapeDtypeStruct((b * t, LANES), jnp.int32),
                   jax.ShapeDtypeStruct((b * SUBLANES, LANES), jnp.int32)],
        scratch_shapes=[pltpu.VMEM((t, LANES), F32), pltpu.VMEM((t, LANES), F32)],
        compiler_params=_params("parallel"),
        name="rank",
    )(route)


def _sorted_rows(t):
    return 2 * t + N_EXPERTS * SUBLANES + ROW_TILE


def _dispatch_kernel(q_ref, h2p_ref, xs_out):
    t = h2p_ref.shape[0] // ROW_PIECES
    xs_out[...] = jnp.zeros(xs_out.shape, xs_out.dtype)

    def body(g, carry):
        src = g * (SUBLANES * ROW_PIECES)
        qi = g * (2 * SUBLANES)
        for s in range(SUBLANES):
            row = h2p_ref[_row_ds(src + s), :]
            xs_out[_row_ds(q_ref[qi + 2 * s]), :] = row
            xs_out[_row_ds(q_ref[qi + 2 * s + 1]), :] = row
        return carry

    lax.fori_loop(0, t // SUBLANES, body, 0)


def _dispatch(q_flat, h2p_flat, b, t):
    flat_rows = _sorted_rows(t) * ROW_PIECES
    return pl.pallas_call(
        _dispatch_kernel,
        grid=(b,),
        in_specs=[pl.BlockSpec((2 * t,), lambda i: (i,), memory_space=pltpu.SMEM),
                  pl.BlockSpec((t * ROW_PIECES, LANES), lambda i: (i, 0))],
        out_specs=pl.BlockSpec((flat_rows, LANES), lambda i: (i, 0)),
        out_shape=jax.ShapeDtypeStruct((b * flat_rows, LANES), jnp.uint32),
        compiler_params=_params("parallel"),
        name="dispatch",
    )(q_flat, h2p_flat)


def _expert_kernel(off_ref, cnt_ref, xs_ref, wgu_ref, wd_ref, ys_out):
    bi = pl.program_id(0)
    ep = pl.program_id(1)

    @pl.when(ep == 0)
    def _():
        ys_out[...] = jnp.zeros(ys_out.shape, ys_out.dtype)

    tile_groups = ROW_TILE // SUBLANES
    spare_g = xs_ref.shape[1] - tile_groups
    g_offs, tiles = [], []
    for k in range(EXPERTS_PER_STEP):
        idx = bi * N_EXPERTS + ep * EXPERTS_PER_STEP + k
        g_offs.append(lax.shift_right_logical(off_ref[idx], SUBLANES.bit_length() - 1))
        tiles.append(lax.shift_right_logical(cnt_ref[idx] + (ROW_TILE - 1), ROW_TILE.bit_length() - 1))
    half = D_MODEL // 2

    def step(j, carry):
        for k in range(EXPERTS_PER_STEP):
            i = tiles[k] - 1 - j
            g0 = jnp.where(i >= 0, g_offs[k] + i * tile_groups, spare_g)
            lo, hi = _unpack_words(_load_tile_rows(xs_ref, g0, ROW_TILE, lead=(0,)))
            xb = jnp.concatenate([lo.astype(BF16), hi.astype(BF16)], axis=1)
            au = _dot(xb, wgu_ref[k])
            a = au[:, :EXPERT_FF]
            mid = (a * _sigmoid(a) * au[:, EXPERT_FF:]).astype(BF16)
            y = _dot(mid, wd_ref[k])
            _store_tile_rows(ys_out, g0, _pack_words(y[:, :half], y[:, half:]), lead=(0,))
        return carry

    lax.fori_loop(0, functools.reduce(jnp.maximum, tiles), step, 0)


def _expert_ffn(off, cnt, xs, w, b, t):
    groups = _sorted_rows(t) // SUBLANES
    rows_spec = pl.BlockSpec((1, groups) + PACKED_TILE, lambda bi, e, *_: (bi, 0, 0, 0, 0))
    grid_spec = pltpu.PrefetchScalarGridSpec(
        num_scalar_prefetch=2,
        grid=(b, N_EXPERTS // EXPERTS_PER_STEP),
        in_specs=[rows_spec,
                  pl.BlockSpec((EXPERTS_PER_STEP, D_MODEL, 2 * EXPERT_FF), lambda bi, e, *_: (e, 0, 0)),
                  pl.BlockSpec((EXPERTS_PER_STEP, EXPERT_FF, D_MODEL), lambda bi, e, *_: (e, 0, 0))],
        out_specs=rows_spec,
    )
    return pl.pallas_call(
        _expert_kernel,
        grid_spec=grid_spec,
        out_shape=jax.ShapeDtypeStruct((b, groups) + PACKED_TILE, jnp.uint32),
        compiler_params=_params("parallel", "arbitrary"),
        name="expert_ffn",
    )(off, cnt, xs, w["wgu"], w["wd"])


def _combine_kernel(q_ref, ys_ref, x1_ref, route_ref, out_ref, g1, g2):
    tc = x1_ref.shape[0]
    groups = tc // SUBLANES
    base = pl.program_id(1) * (2 * tc)

    def body(g, carry):
        dst = g * (SUBLANES * ROW_PIECES)
        qi = base + g * (2 * SUBLANES)
        for s in range(SUBLANES):
            g1[_row_ds(dst + s), :] = ys_ref[_row_ds(q_ref[qi + 2 * s]), :]
            g2[_row_ds(dst + s), :] = ys_ref[_row_ds(q_ref[qi + 2 * s + 1]), :]
        return carry

    lax.fori_loop(0, groups, body, 0)
    w1 = route_ref[:, 2:3]
    w2 = route_ref[:, 3:4]
    half = D_MODEL // 2
    for c in range(ROW_PIECES):
        tile_of = lambda ref: jnp.concatenate(
            [ref[(g * ROW_PIECES + c) * SUBLANES:(g * ROW_PIECES + c + 1) * SUBLANES, :] for g in range(groups)], axis=0)
        lo1, hi1 = _unpack_words(tile_of(g1))
        lo2, hi2 = _unpack_words(tile_of(g2))
        cols = slice(c * LANES, (c + 1) * LANES)
        cols_hi = slice(half + c * LANES, half + (c + 1) * LANES)
        out_ref[:, cols] = x1_ref[:, cols] + (w1 * lo1 + w2 * lo2)
        out_ref[:, cols_hi] = x1_ref[:, cols_hi] + (w1 * hi1 + w2 * hi2)


def _combine(q_flat, ys_flat, x1, route, b, t):
    flat_rows = _sorted_rows(t) * ROW_PIECES
    tc = COMBINE_TILE
    nt = t // tc
    return pl.pallas_call(
        _combine_kernel,
        grid=(b, nt),
        in_specs=[pl.BlockSpec((2 * t,), lambda bi, j: (bi,), memory_space=pltpu.SMEM),
                  pl.BlockSpec((flat_rows, LANES), lambda bi, j: (bi, 0)),
                  pl.BlockSpec((tc, D_MODEL), lambda bi, j: (bi * nt + j, 0)),
                  pl.BlockSpec((tc, LANES), lambda bi, j: (bi * nt + j, 0))],
        out_specs=pl.BlockSpec((tc, D_MODEL), lambda bi, j: (bi * nt + j, 0)),
        out_shape=jax.ShapeDtypeStruct((b * t, D_MODEL), F32),
        scratch_shapes=[pltpu.VMEM((tc * ROW_PIECES, LANES), jnp.uint32),
                        pltpu.VMEM((tc * ROW_PIECES, LANES), jnp.uint32)],
        compiler_params=_params("parallel", "arbitrary"),
        name="combine",
    )(q_flat, ys_flat, x1, route)


def _head_slots(wmat, width):
    k = wmat.shape[0]
    w3 = wmat.reshape(k, N_HEADS, width)
    return jnp.pad(w3, ((0, 0), (0, 0), (0, HEAD_SLOT - width))).reshape(k, N_HEADS * HEAD_SLOT)


def _gain3(g):
    gp = jnp.pad(g.astype(F32), (0, HEAD_SLOT - QK_DIM))
    return jnp.stack([gp, jnp.roll(gp, ROPE_HALF), jnp.roll(gp, -ROPE_HALF)])


def _prepare_weights(g_mix, w_in, g_q_lora, w_uq, g_kv_lora, w_ukv, g_qk_q, g_qk_k, w_o_mla, w_dw, b_dw,
                     g_conv_ln, b_conv_ln, w_conv_out, b_gates, w_out, g_ffn, w_group_router, b_group_router,
                     w_expert_router, b_expert_router, w_e_gate, w_e_up, w_e_down):
    row = lambda a: a.astype(F32).reshape(1, -1)
    s0, s1, s2, s3 = Q_LORA, Q_LORA + KV_LORA, Q_LORA + KV_LORA + ROPE, Q_LORA + KV_LORA + ROPE + 2 * CONV_W
    w = {}
    w["g_mix"] = row(g_mix)
    w["wa"] = jnp.concatenate([w_in[:, :s1], jnp.pad(w_in[:, s1:s2], ((0, 0), (0, LANES - ROPE)))], axis=1).astype(BF16)
    w["wc"] = w_in[:, s2:s3].astype(BF16)
    w["wg"] = w_in[:, s3:].astype(BF16)
    w["b_gates"] = row(b_gates)
    w["g_q_lora"] = row(g_q_lora)
    w["wuq"] = _head_slots(w_uq, QK_DIM).astype(BF16)
    w["g_kv_lora"] = row(g_kv_lora)
    kv3 = w_ukv.reshape(KV_LORA, N_HEADS, NOPE + V_DIM)
    w["wk"] = _head_slots(kv3[:, :, :NOPE].reshape(KV_LORA, N_HEADS * NOPE), NOPE).astype(BF16)
    w["wv"] = kv3[:, :, NOPE:].reshape(KV_LORA, N_HEADS * V_DIM).astype(BF16)
    src = jnp.arange(LANES)[:, None]
    dst = jnp.arange(N_HEADS * HEAD_SLOT)[None, :]
    w["rrep"] = ((src < ROPE) & ((dst % HEAD_SLOT) == (src + NOPE))).astype(BF16)
    w["gq3"] = _gain3(g_qk_q)
    w["gk3"] = _gain3(g_qk_k)
    w["wo"] = w_o_mla.astype(BF16)
    w["w_dw"] = jnp.pad(w_dw.astype(F32), ((0, 32 - CONV_K), (0, 0)))
    w["b_dw"] = row(b_dw)
    w["g_conv_ln"] = row(g_conv_ln)
    w["b_conv_ln"] = row(b_conv_ln)
    w["wco"] = w_conv_out.astype(BF16)
    w["wout"] = w_out.astype(BF16)
    w["g_ffn"] = row(g_ffn)
    pad_r = LANES - N_EXPERTS - N_GROUPS
    wr = jnp.concatenate([w_expert_router, w_group_router, jnp.zeros((D_MODEL, pad_r), F32)], axis=1).astype(F32)
    w["wr_hi"] = wr.astype(BF16)
    w["wr_lo"] = (wr - w["wr_hi"].astype(F32)).astype(BF16)
    w["br"] = jnp.concatenate([b_expert_router, b_group_router, jnp.zeros((pad_r,), F32)]).astype(F32).reshape(1, LANES)
    w["wgu"] = jnp.concatenate([w_e_gate.astype(BF16), w_e_up.astype(BF16)], axis=-1)
    w["wd"] = w_e_down.astype(BF16)
    return w


def kernel(x, positions, g_mix, w_in, g_q_lora, w_uq, g_kv_lora, w_ukv, g_qk_q, g_qk_k, w_o_mla, w_dw, b_dw,
           g_conv_ln, b_conv_ln, w_conv_out, b_gates, w_out, g_ffn, w_group_router, b_group_router,
           w_expert_router, b_expert_router, w_e_gate, w_e_up, w_e_down):
    b, t, d = x.shape
    assert d == D_MODEL and g_mix.shape[0] == 1
    assert t % ATTN_TILE == 0 and t % RANK_CHUNK == 0 and t % COMBINE_TILE == 0
    tm = min(TOKEN_BLOCK, t)
    assert t % tm == 0 and tm % CONV_HALO == 0
    n = b * t
    w = _prepare_weights(g_mix[0], w_in[0], g_q_lora[0], w_uq[0], g_kv_lora[0], w_ukv[0], g_qk_q[0], g_qk_k[0],
                         w_o_mla[0], w_dw[0], b_dw[0], g_conv_ln[0], b_conv_ln[0], w_conv_out[0], b_gates[0],
                         w_out[0], g_ffn[0], w_group_router[0], b_group_router[0], w_expert_router[0],
                         b_expert_router[0], w_e_gate[0], w_e_up[0], w_e_down[0])
    x2 = x.reshape(n, d)
    cos, sin = _rope_tables(positions)
    q, k, v, u, gates = _pre_attention(x2, cos, sin, w, tm)
    attn = _attention(q, k, v, b, t)
    x1, h2p, route = _merge(attn, u, gates, x2, w, t, tm)
    pos, meta = _rank(route, b, t)
    q_flat = pos[:, :2].reshape(n * 2)
    meta3 = meta.reshape(b, SUBLANES, LANES)
    off = meta3[:, 0, :N_EXPERTS].reshape(b * N_EXPERTS)
    cnt = meta3[:, 1, :N_EXPERTS].reshape(b * N_EXPERTS)
    groups = _sorted_rows(t) // SUBLANES
    xs = _dispatch(q_flat, h2p.reshape(n * ROW_PIECES, LANES), b, t)
    ys = _expert_ffn(off, cnt, xs.reshape((b, groups) + PACKED_TILE), w, b, t)
    out = _combine(q_flat, ys.reshape(b * groups * SUBLANES * ROW_PIECES, LANES), x1, route, b, t)
    return out.reshape(b, t, d)
```

```python
import functools

import jax
import jax.numpy as jnp
import numpy as np
from jax import lax
from jax.experimental import pallas as pl
from jax.experimental.pallas import tpu as pltpu

D_MODEL = 1024
N_HEADS = 8
NOPE = 64
ROPE = 32
ROPE_HALF = ROPE // 2
QK_DIM = NOPE + ROPE
V_DIM = 64
Q_LORA = 256
KV_LORA = 128
CONV_W = 512
CONV_K = 31
N_GROUPS = 4
EXPERTS_PER_GROUP = 8
N_EXPERTS = N_GROUPS * EXPERTS_PER_GROUP
EXPERT_FF = 256
EPS = 1e-6
ROPE_THETA = 10000.0
CHUNK = 64

LANES = 128
SUBLANES = 8
HEAD_SLOT = LANES
VMEM_LIMIT_BYTES = 56 * 1024 * 1024

TOKEN_BLOCK = 512
ATTN_TILE = 256
CONV_HALO = 32
ROW_TILE = 256
EXPERTS_PER_STEP = 2
COMBINE_TILE = 256
RANK_CHUNK = 256
SUB_ROWS = 256
CONV_CHUNK = 32

F32 = jnp.float32
BF16 = jnp.bfloat16
NEG_INF = float("-inf")
LOG2E = 1.4426950408889634


def _dot(a, b, **kw):
    return jnp.dot(a, b, preferred_element_type=F32, **kw)


def _rms(x, g):
    return x * lax.rsqrt(jnp.mean(x * x, axis=-1, keepdims=True) + EPS) * g


def _sigmoid(x):
    return 1.0 / (1.0 + jnp.exp(-x))


def _pack_words(lo, hi):
    return pltpu.pack_elementwise([lo, hi], packed_dtype=BF16)


def _unpack_words(wd):
    lo = pltpu.unpack_elementwise(wd, index=0, packed_dtype=BF16, unpacked_dtype=F32)
    hi = pltpu.unpack_elementwise(wd, index=1, packed_dtype=BF16, unpacked_dtype=F32)
    return lo, hi


PACKED_TILE = (D_MODEL // 2 // LANES, SUBLANES, LANES)
ROW_PIECES = PACKED_TILE[0]


def _store_tile_rows(ref, g0, words, lead=()):
    groups = words.shape[0] // SUBLANES
    for c in range(ROW_PIECES):
        piece = words[:, c * LANES:(c + 1) * LANES].reshape(groups, SUBLANES, LANES)
        ref[lead + (pl.ds(g0, groups), c)] = piece


def _load_tile_rows(ref, g0, rows, lead=()):
    groups = rows // SUBLANES
    pieces = [ref[lead + (pl.ds(g0, groups), c)].reshape(rows, LANES) for c in range(ROW_PIECES)]
    return jnp.concatenate(pieces, axis=1)


def _row_ds(q):
    return pl.ds(q, ROW_PIECES, stride=SUBLANES)


def _params(*sem):
    return pltpu.CompilerParams(dimension_semantics=sem, vmem_limit_bytes=VMEM_LIMIT_BYTES)


def _full(shape):
    nd = len(shape)
    return pl.BlockSpec(shape, lambda *_: (0,) * nd)


def _rope_table_kernel(pos_ref, freq_ref, cos_ref, sin_ref):
    ang = pos_ref[...].astype(F32) * freq_ref[...]
    cos_ref[...] = jnp.cos(ang)
    sin_ref[...] = jnp.sin(ang)


def _rope_tables(positions):
    n = positions.size
    per_row = LANES // ROPE_HALF
    rows = n // per_row
    pos_rep = jnp.repeat(positions.reshape(rows, per_row), ROPE_HALF, axis=1)
    inv_freq = ROPE_THETA ** (-jnp.arange(ROPE_HALF, dtype=F32) / ROPE_HALF)
    freq = jnp.tile(inv_freq, per_row).reshape(1, LANES)
    rb = min(rows, 512)
    cos, sin = pl.pallas_call(
        _rope_table_kernel,
        grid=(rows // rb,),
        in_specs=[pl.BlockSpec((rb, LANES), lambda i: (i, 0)), _full((1, LANES))],
        out_specs=[pl.BlockSpec((rb, LANES), lambda i: (i, 0))] * 2,
        out_shape=[jax.ShapeDtypeStruct((rows, LANES), F32)] * 2,
        compiler_params=_params("parallel"),
        name="rope_tables",
    )(pos_rep, freq)
    return cos.reshape(n, ROPE_HALF), sin.reshape(n, ROPE_HALF)


def _pre_attention_kernel(x_ref, cos_ref, sin_ref, gmix_ref, wa_ref, wc_ref, wg_ref, bg_ref,
                          gql_ref, wuq_ref, gkv_ref, wk_ref, wv_ref, rrep_ref, gq_ref, gk_ref,
                          q_out, k_out, v_out, u_out, gate_out):
    tm = x_ref.shape[0]
    hb = _rms(x_ref[...], gmix_ref[...]).astype(BF16)

    a = _dot(hb, wa_ref[...])
    qdn = _rms(a[:, :Q_LORA], gql_ref[...]).astype(BF16)
    kvn = _rms(a[:, Q_LORA:Q_LORA + KV_LORA], gkv_ref[...]).astype(BF16)
    kr = a[:, Q_LORA + KV_LORA:].astype(BF16)
    qf = _dot(qdn, wuq_ref[...])
    kf = _dot(kvn, wk_ref[...]) + _dot(kr, rrep_ref[...])
    v_out[...] = _dot(kvn, wv_ref[...]).astype(BF16)

    cos = cos_ref[...]
    sin = sin_ref[...]
    z = lambda w: jnp.zeros((tm, w), F32)
    c_tab = jnp.concatenate([jnp.ones((tm, NOPE), F32), cos, cos, z(HEAD_SLOT - QK_DIM)], axis=-1)
    s_up = jnp.concatenate([z(NOPE + ROPE_HALF), sin, z(HEAD_SLOT - QK_DIM)], axis=-1)
    s_dn = jnp.concatenate([z(NOPE), -sin, z(HEAD_SLOT - NOPE - ROPE_HALF)], axis=-1)

    def norm_rope(f, g_ref, out, scale):
        ta = c_tab * g_ref[0:1, :]
        tu = s_up * g_ref[1:2, :]
        td = s_dn * g_ref[2:3, :]
        for h in range(N_HEADS):
            xs = f[:, h * HEAD_SLOT:(h + 1) * HEAD_SLOT]
            ss = jnp.sum(xs * xs, axis=-1, keepdims=True)
            s = lax.rsqrt(ss * (1.0 / QK_DIM) + EPS) * scale
            y = xs * ta + pltpu.roll(xs, ROPE_HALF, 1) * tu + pltpu.roll(xs, HEAD_SLOT - ROPE_HALF, 1) * td
            out[:, h * HEAD_SLOT:(h + 1) * HEAD_SLOT] = (y * s).astype(BF16)

    norm_rope(qf, gq_ref, q_out, LOG2E * QK_DIM ** -0.5)
    norm_rope(kf, gk_ref, k_out, 1.0)

    c = _dot(hb, wc_ref[...])
    u_out[...] = c[:, :CONV_W] * _sigmoid(c[:, CONV_W:])
    gate_out[...] = _sigmoid(_dot(hb, wg_ref[...]) + bg_ref[...]).astype(BF16)


def _pre_attention(x2, cos, sin, w, tm):
    n = x2.shape[0]
    row = lambda width: pl.BlockSpec((tm, width), lambda i: (i, 0))
    hs = N_HEADS * HEAD_SLOT
    weights = [w["g_mix"], w["wa"], w["wc"], w["wg"], w["b_gates"], w["g_q_lora"], w["wuq"],
               w["g_kv_lora"], w["wk"], w["wv"], w["rrep"], w["gq3"], w["gk3"]]
    return pl.pallas_call(
        _pre_attention_kernel,
        grid=(n // tm,),
        in_specs=[row(D_MODEL), row(ROPE_HALF), row(ROPE_HALF)] + [_full(a.shape) for a in weights],
        out_specs=[row(hs), row(hs), row(N_HEADS * V_DIM), row(CONV_W), row(2 * D_MODEL)],
        out_shape=[jax.ShapeDtypeStruct((n, hs), BF16), jax.ShapeDtypeStruct((n, hs), BF16),
                   jax.ShapeDtypeStruct((n, N_HEADS * V_DIM), BF16), jax.ShapeDtypeStruct((n, CONV_W), F32),
                   jax.ShapeDtypeStruct((n, 2 * D_MODEL), BF16)],
        compiler_params=_params("parallel"),
        name="pre_attention",
    )(x2, cos, sin, *weights)


def _attention_kernel(q_ref, k_ref, v_ref, o_ref, kbd, vbd):
    t = ATTN_TILE
    hk = HEAD_SLOT
    seq = q_ref.shape[0]
    lane = lax.broadcasted_iota(jnp.int32, (hk, HEAD_SLOT), 1)
    zero = jnp.zeros((hk, HEAD_SLOT), BF16)
    for c in range(seq // hk):
        rows = slice(c * hk, (c + 1) * hk)
        top = slice(2 * c * hk, (2 * c + 1) * hk)
        bot = slice((2 * c + 1) * hk, (2 * c + 2) * hk)
        kbd[top, :HEAD_SLOT] = k_ref[rows, :HEAD_SLOT]
        kbd[top, HEAD_SLOT:] = zero
        kbd[bot, :HEAD_SLOT] = zero
        kbd[bot, HEAD_SLOT:] = k_ref[rows, HEAD_SLOT:]
        vp = v_ref[rows, :]
        vbd[top, :] = jnp.where(lane < V_DIM, vp, zero)
        vbd[bot, :] = jnp.where(lane >= V_DIM, vp, zero)

    qrow = lax.broadcasted_iota(jnp.int32, (t, hk), 0) // CHUNK
    kcol = lax.broadcasted_iota(jnp.int32, (t, hk), 1)
    diag_masks = [(kcol + h2 * hk) // CHUNK <= qrow for h2 in range(t // hk)]
    out_lane = lax.broadcasted_iota(jnp.int32, (t, HEAD_SLOT), 1)
    for i in range(seq // t):
        rows = slice(i * t, (i + 1) * t)
        n_tiles = 2 * (i + 1) * t // hk
        s = lax.dot_general(q_ref[rows, :], kbd[0:n_tiles * hk, :], (((1,), (1,)), ((), ())),
                            preferred_element_type=F32)
        tiles = [s[:, j * hk:(j + 1) * hk] for j in range(n_tiles)]
        first_diag = n_tiles - 2 * (t // hk)
        for j in range(first_diag, n_tiles):
            tiles[j] = jnp.where(diag_masks[(j - first_diag) // 2], tiles[j], NEG_INF)
        probs = [None] * n_tiles
        inv_l = []
        for h in range(2):
            mine = range(h, n_tiles, 2)
            m = jnp.max(functools.reduce(jnp.maximum, [tiles[j] for j in mine]), axis=-1, keepdims=True)
            for j in mine:
                probs[j] = jnp.exp2(tiles[j] - m)
            l = jnp.sum(functools.reduce(jnp.add, [probs[j] for j in mine]), axis=-1, keepdims=True)
            inv_l.append(1.0 / l)
        p = jnp.concatenate(probs, axis=1).astype(BF16)
        o = _dot(p, vbd[0:n_tiles * hk, :])
        o_ref[rows, :] = (o * jnp.where(out_lane < V_DIM, inv_l[0], inv_l[1])).astype(BF16)


def _attention(q, k, v, b, t):
    hs2 = 2 * HEAD_SLOT
    spec = pl.BlockSpec((t, hs2), lambda bi, p: (bi, p))
    vspec = pl.BlockSpec((t, 2 * V_DIM), lambda bi, p: (bi, p))
    return pl.pallas_call(
        _attention_kernel,
        grid=(b, N_HEADS // 2),
        in_specs=[spec, spec, vspec],
        out_specs=pl.BlockSpec((t, 2 * V_DIM), lambda bi, p: (bi, p)),
        out_shape=jax.ShapeDtypeStruct((b * t, N_HEADS * V_DIM), BF16),
        scratch_shapes=[pltpu.VMEM((2 * t, hs2), BF16), pltpu.VMEM((2 * t, 2 * V_DIM), BF16)],
        compiler_params=_params("parallel", "parallel"),
        name="attention",
    )(q, k, v)


def _merge_kernel(blocks_per_seq, attn_ref, u_ref, halo_ref, gate_ref, x_ref, wo_ref, wdw_ref, bdw_ref,
                  gln_ref, bln_ref, wco_ref, wout_ref, gffn_ref, wrh_ref, wrl_ref, br_ref,
                  x1_out, h2p_out, route_out, sh_ref):
    tm = x_ref.shape[0]
    first = (pl.program_id(0) % blocks_per_seq) == 0
    halo = halo_ref[...]
    sh_ref[0, 0:CONV_HALO, :] = jnp.where(first, jnp.zeros_like(halo), halo)
    sh_ref[0, CONV_HALO:, :] = u_ref[...]
    span = tm + CONV_HALO - SUBLANES
    for s in range(1, SUBLANES):
        sh_ref[s, 0:span, :] = sh_ref[0, s:s + span, :]

    base = CONV_HALO - (CONV_K - 1)
    half = D_MODEL // 2
    for sb in range(tm // SUB_ROWS):
        rows = slice(sb * SUB_ROWS, (sb + 1) * SUB_ROWS)
        parts = []
        for ck in range(SUB_ROWS // CONV_CHUNK):
            c0 = sb * SUB_ROWS + ck * CONV_CHUNK
            acc = jnp.broadcast_to(bdw_ref[...], (CONV_CHUNK, CONV_W))
            for j in range(CONV_K):
                s, a = (base + j) % SUBLANES, (base + j) // SUBLANES * SUBLANES
                acc = acc + sh_ref[s, c0 + a:c0 + a + CONV_CHUNK, :] * wdw_ref[j:j + 1, :]
            parts.append(acc)
        conv = jnp.concatenate(parts, axis=0)
        mu = jnp.mean(conv, axis=-1, keepdims=True)
        cen = conv - mu
        var = jnp.mean(cen * cen, axis=-1, keepdims=True)
        ln = cen * lax.rsqrt(var + EPS) * gln_ref[...] + bln_ref[...]
        ub = (ln * _sigmoid(ln)).astype(BF16)
        y_b = _dot(ub, wco_ref[...])
        y_a = _dot(attn_ref[rows, :], wo_ref[...])
        merged = gate_ref[rows, :D_MODEL].astype(F32) * y_a + gate_ref[rows, D_MODEL:].astype(F32) * y_b
        x1 = x_ref[rows, :] + _dot(merged.astype(BF16), wout_ref[...])
        x1_out[rows, :] = x1
        h2 = _rms(x1, gffn_ref[...])
        _store_tile_rows(h2p_out, sb * (SUB_ROWS // SUBLANES), _pack_words(h2[:, :half], h2[:, half:]))
        h2_hi = h2.astype(BF16)
        h2_lo = (h2 - h2_hi.astype(F32)).astype(BF16)
        logits = (_dot(h2_hi, wrh_ref[...]) + _dot(h2_lo, wrh_ref[...]) + _dot(h2_hi, wrl_ref[...])
                  + br_ref[...])
        route_out[rows, :] = _route(logits)


def _route(logits):
    lane_i = lax.broadcasted_iota(jnp.int32, logits.shape, 1)
    lane = lane_i.astype(F32)
    big = float(LANES)
    is_g = (lane_i >= N_EXPERTS) & (lane_i < N_EXPERTS + N_GROUPS)
    gl = jnp.where(is_g, logits, NEG_INF)
    gmax = jnp.max(gl, axis=-1, keepdims=True)
    gidx = jnp.min(jnp.where(gl == gmax, lane - N_EXPERTS, big), axis=-1, keepdims=True)
    g_p = 1.0 / jnp.sum(jnp.exp(gl - gmax), axis=-1, keepdims=True)
    lane_group = (lane_i // EXPERTS_PER_GROUP).astype(F32)
    valid = (lane_i < N_EXPERTS) & (lane_group == gidx)
    el = jnp.where(valid, logits, NEG_INF)
    ee = jnp.exp(el - jnp.max(el, axis=-1, keepdims=True))
    within = jnp.where(valid, ee / jnp.sum(ee, axis=-1, keepdims=True), -1.0)
    p1 = jnp.max(within, axis=-1, keepdims=True)
    i1 = jnp.min(jnp.where(within == p1, lane, big), axis=-1, keepdims=True)
    within2 = jnp.where(lane == i1, -1.0, within)
    p2 = jnp.max(within2, axis=-1, keepdims=True)
    i2 = jnp.min(jnp.where(within2 == p2, lane, big), axis=-1, keepdims=True)
    psum = p1 + p2
    w1 = g_p * (p1 / psum)
    w2 = g_p * (p2 / psum)
    route = jnp.where(lane_i == 0, i1, 0.0)
    route = jnp.where(lane_i == 1, i2, route)
    route = jnp.where(lane_i == 2, w1, route)
    return jnp.where(lane_i == 3, w2, route)


def _merge(attn, u, gates, x2, w, t, tm):
    n = x2.shape[0]
    bps = t // tm
    row = lambda width: pl.BlockSpec((tm, width), lambda i: (i, 0))
    hpb = tm // CONV_HALO
    halo = pl.BlockSpec((CONV_HALO, CONV_W), lambda i: (jnp.maximum(i * hpb - 1, 0), 0))
    weights = [w["wo"], w["w_dw"], w["b_dw"], w["g_conv_ln"], w["b_conv_ln"], w["wco"], w["wout"],
               w["g_ffn"], w["wr_hi"], w["wr_lo"], w["br"]]
    tile_rows = (tm // SUBLANES,) + PACKED_TILE
    return pl.pallas_call(
        functools.partial(_merge_kernel, bps),
        grid=(n // tm,),
        in_specs=[row(N_HEADS * V_DIM), row(CONV_W), halo, row(2 * D_MODEL), row(D_MODEL)]
        + [_full(a.shape) for a in weights],
        out_specs=[row(D_MODEL), pl.BlockSpec(tile_rows, lambda i: (i, 0, 0, 0)), row(LANES)],
        out_shape=[jax.ShapeDtypeStruct((n, D_MODEL), F32),
                   jax.ShapeDtypeStruct((n // SUBLANES,) + PACKED_TILE, jnp.uint32),
                   jax.ShapeDtypeStruct((n, LANES), F32)],
        scratch_shapes=[pltpu.VMEM((SUBLANES, tm + CONV_HALO, CONV_W), F32)],
        compiler_params=_params("parallel"),
        name="merge_router",
    )(attn, u, u, gates, x2, *weights)


def _rank_kernel(route_ref, pos_out, meta_out, r1_sc, r2_sc):
    tb = route_ref.shape[0]
    lane = lax.broadcasted_iota(jnp.int32, (RANK_CHUNK, LANES), 1).astype(F32)
    rr = lax.broadcasted_iota(jnp.int32, (RANK_CHUNK, RANK_CHUNK), 0)
    cc = lax.broadcasted_iota(jnp.int32, (RANK_CHUNK, RANK_CHUNK), 1)
    lower = jnp.where(rr > cc, 1.0, 0.0).astype(BF16)
    run = jnp.zeros((1, LANES), F32)
    for choice, sc in ((0, r1_sc), (1, r2_sc)):
        for ci in range(tb // RANK_CHUNK):
            rows = slice(ci * RANK_CHUNK, (ci + 1) * RANK_CHUNK)
            oh = jnp.where(lane == route_ref[rows, choice:choice + 1], 1.0, 0.0)
            sc[rows, :] = run + _dot(lower, oh.astype(BF16))
            run = run + jnp.sum(oh, axis=0, keepdims=True)
    cnt = run
    cnt8 = jnp.ceil(cnt * (1.0 / SUBLANES)) * SUBLANES
    ur = lax.broadcasted_iota(jnp.int32, (LANES, LANES), 0)
    uc = lax.broadcasted_iota(jnp.int32, (LANES, LANES), 1)
    upper = jnp.where(ur < uc, 1.0, 0.0)
    off = _dot(jnp.broadcast_to(cnt8, (SUBLANES, LANES)), upper, precision=lax.Precision.HIGHEST)[0:1, :]
    lane_t = lax.broadcasted_iota(jnp.int32, (tb, LANES), 1)
    lane_tf = lane_t.astype(F32)
    p1 = jnp.sum(jnp.where(lane_tf == route_ref[:, 0:1], off + r1_sc[...], 0.0), axis=-1, keepdims=True)
    p2 = jnp.sum(jnp.where(lane_tf == route_ref[:, 1:2], off + r2_sc[...], 0.0), axis=-1, keepdims=True)
    q1 = p1 + (SUBLANES * ROW_PIECES - SUBLANES) * jnp.floor(p1 * (1.0 / SUBLANES))
    q2 = p2 + (SUBLANES * ROW_PIECES - SUBLANES) * jnp.floor(p2 * (1.0 / SUBLANES))
    pos = jnp.where(lane_t == 0, q1, jnp.where(lane_t == 1, q2, 0.0))
    pos_out[...] = pos.astype(jnp.int32)
    row8 = lax.broadcasted_iota(jnp.int32, (SUBLANES, LANES), 0)
    meta = jnp.where(row8 == 0, jnp.broadcast_to(off, (SUBLANES, LANES)),
                     jnp.where(row8 == 1, jnp.broadcast_to(cnt, (SUBLANES, LANES)), 0.0))
    meta_out[...] = meta.astype(jnp.int32)


def _rank(route, b, t):
    return pl.pallas_call(
        _rank_kernel,
        grid=(b,),
        in_specs=[pl.BlockSpec((t, LANES), lambda i: (i, 0))],
        out_specs=[pl.BlockSpec((t, LANES), lambda i: (i, 0)), pl.BlockSpec((SUBLANES, LANES), lambda i: (i, 0))],
        out_shape=[jax.ShapeDtypeStruct((b * t, LANES), jnp.int32),
                   jax.ShapeDtypeStruct((b * SUBLANES, LANES), jnp.int32)],
        scratch_shapes=[pltpu.VMEM((t, LANES), F32), pltpu.VMEM((t, LANES), F32)],
        compiler_params=_params("parallel"),
        name="rank",
    )(route)


def _sorted_rows(t):
    return 2 * t + N_EXPERTS * SUBLANES + ROW_TILE


GROUP_ROWS = SUBLANES * ROW_PIECES


def _load_flat_tile(ref, base, rows):
    cols = []
    for c in range(ROW_PIECES):
        cols.append(jnp.concatenate(
            [ref[pl.ds(base + g * GROUP_ROWS + c * SUBLANES, SUBLANES), :] for g in range(rows // SUBLANES)], axis=0))
    return jnp.concatenate(cols, axis=1)


def _store_flat_tile(ref, base, words):
    for c in range(ROW_PIECES):
        for g in range(words.shape[0] // SUBLANES):
            ref[pl.ds(base + g * GROUP_ROWS + c * SUBLANES, SUBLANES), :] = (
                words[g * SUBLANES:(g + 1) * SUBLANES, c * LANES:(c + 1) * LANES])


def _moe_kernel(steps_per_tile, off_ref, cnt_ref, q_cur_ref, q_prev_ref, h2p_ref, wgu_ref, wd_ref, x1_ref,
                route_ref, out_ref, xs, ys, g1, g2):
    b = pl.program_id(0)
    s = pl.program_id(1)
    n_seq = pl.num_programs(0) - 1
    t = h2p_ref.shape[0] // ROW_PIECES
    half = D_MODEL // 2

    @pl.when((b < n_seq) & (s == 0))
    def _dispatch():
        xs[...] = jnp.zeros(xs.shape, xs.dtype)

        def body(g, carry):
            src = g * GROUP_ROWS
            qi = g * (2 * SUBLANES)
            for r in range(SUBLANES):
                row = h2p_ref[_row_ds(src + r), :]
                xs[_row_ds(q_cur_ref[qi + 2 * r]), :] = row
                xs[_row_ds(q_cur_ref[qi + 2 * r + 1]), :] = row
            return carry

        lax.fori_loop(0, t // SUBLANES, body, 0)

    @pl.when(b < n_seq)
    def _experts():
        ys_cur = ys.at[b % 2]
        spare = xs.shape[0] - ROW_TILE * ROW_PIECES
        bases, tiles = [], []
        for k in range(EXPERTS_PER_STEP):
            idx = b * N_EXPERTS + s * EXPERTS_PER_STEP + k
            bases.append(off_ref[idx] * ROW_PIECES)
            tiles.append(lax.shift_right_logical(cnt_ref[idx] + (ROW_TILE - 1), ROW_TILE.bit_length() - 1))

        def step(j, carry):
            for k in range(EXPERTS_PER_STEP):
                i = tiles[k] - 1 - j
                base = pl.multiple_of(jnp.where(i >= 0, bases[k] + i * (ROW_TILE * ROW_PIECES), spare), GROUP_ROWS)
                lo, hi = _unpack_words(_load_flat_tile(xs, base, ROW_TILE))
                xb = jnp.concatenate([lo.astype(BF16), hi.astype(BF16)], axis=1)
                au = _dot(xb, wgu_ref[k])
                a = au[:, :EXPERT_FF]
                mid = (a * _sigmoid(a) * au[:, EXPERT_FF:]).astype(BF16)
                y = _dot(mid, wd_ref[k])
                _store_flat_tile(ys_cur, base, _pack_words(y[:, :half], y[:, half:]))
            return carry

        lax.fori_loop(0, functools.reduce(jnp.maximum, tiles), step, 0)

    @pl.when((b > 0) & (s % steps_per_tile == 0))
    def _combine():
        ys_prev = ys.at[(b + 1) % 2]
        tc = x1_ref.shape[0]
        groups = tc // SUBLANES
        qbase = (s // steps_per_tile) * (2 * tc)

        def body(g, carry):
            dst = g * GROUP_ROWS
            qi = qbase + g * (2 * SUBLANES)
            for r in range(SUBLANES):
                g1[_row_ds(dst + r), :] = ys_prev[_row_ds(q_prev_ref[qi + 2 * r]), :]
                g2[_row_ds(dst + r), :] = ys_prev[_row_ds(q_prev_ref[qi + 2 * r + 1]), :]
            return carry

        lax.fori_loop(0, groups, body, 0)
        w1 = route_ref[:, 2:3]
        w2 = route_ref[:, 3:4]
        for c in range(ROW_PIECES):
            tile_of = lambda ref: jnp.concatenate(
                [ref[g * GROUP_ROWS + c * SUBLANES:g * GROUP_ROWS + (c + 1) * SUBLANES, :] for g in range(groups)],
                axis=0)
            lo1, hi1 = _unpack_words(tile_of(g1))
            lo2, hi2 = _unpack_words(tile_of(g2))
            cols = slice(c * LANES, (c + 1) * LANES)
            cols_hi = slice(half + c * LANES, half + (c + 1) * LANES)
            out_ref[:, cols] = x1_ref[:, cols] + (w1 * lo1 + w2 * lo2)
            out_ref[:, cols_hi] = x1_ref[:, cols_hi] + (w1 * hi1 + w2 * hi2)


def _moe(off, cnt, q_flat, h2p_flat, x1, route, w, b, t):
    flat_rows = _sorted_rows(t) * ROW_PIECES
    tc = COMBINE_TILE
    tiles_per_seq = t // tc
    n_steps = N_EXPERTS // EXPERTS_PER_STEP
    assert n_steps % tiles_per_seq == 0
    spt = n_steps // tiles_per_seq
    last = b - 1
    cur = lambda bi, s, *_: (jnp.minimum(bi, last),)
    prev = lambda bi, s, *_: (jnp.maximum(bi - 1, 0),)
    tile = lambda bi, s, *_: (jnp.where(bi == 0, 0, (bi - 1) * tiles_per_seq + s // spt), 0)
    wmap = lambda bi, s, *_: (jnp.where(bi <= last, s, n_steps - 1), 0, 0)
    grid_spec = pltpu.PrefetchScalarGridSpec(
        num_scalar_prefetch=2,
        grid=(b + 1, n_steps),
        in_specs=[pl.BlockSpec((2 * t,), cur, memory_space=pltpu.SMEM),
                  pl.BlockSpec((2 * t,), prev, memory_space=pltpu.SMEM),
                  pl.BlockSpec((t * ROW_PIECES, LANES), lambda bi, s, *_: (jnp.minimum(bi, last), 0)),
                  pl.BlockSpec((EXPERTS_PER_STEP, D_MODEL, 2 * EXPERT_FF), wmap),
                  pl.BlockSpec((EXPERTS_PER_STEP, EXPERT_FF, D_MODEL), wmap),
                  pl.BlockSpec((tc, D_MODEL), tile),
                  pl.BlockSpec((tc, LANES), tile)],
        out_specs=pl.BlockSpec((tc, D_MODEL), tile),
        scratch_shapes=[pltpu.VMEM((flat_rows, LANES), jnp.uint32),
                        pltpu.VMEM((2, flat_rows, LANES), jnp.uint32),
                        pltpu.VMEM((tc * ROW_PIECES, LANES), jnp.uint32),
                        pltpu.VMEM((tc * ROW_PIECES, LANES), jnp.uint32)],
    )
    return pl.pallas_call(
        functools.partial(_moe_kernel, spt),
        grid_spec=grid_spec,
        out_shape=jax.ShapeDtypeStruct((b * t, D_MODEL), F32),
        compiler_params=_params("arbitrary", "arbitrary"),
        name="moe",
    )(off, cnt, q_flat, q_flat, h2p_flat, w["wgu"], w["wd"], x1, route)


def _head_slots(wmat, width):
    k = wmat.shape[0]
    w3 = wmat.reshape(k, N_HEADS, width)
    return jnp.pad(w3, ((0, 0), (0, 0), (0, HEAD_SLOT - width))).reshape(k, N_HEADS * HEAD_SLOT)


def _gain3(g):
    gp = jnp.pad(g.astype(F32), (0, HEAD_SLOT - QK_DIM))
    return jnp.stack([gp, jnp.roll(gp, ROPE_HALF), jnp.roll(gp, -ROPE_HALF)])


def _prepare_weights(g_mix, w_in, g_q_lora, w_uq, g_kv_lora, w_ukv, g_qk_q, g_qk_k, w_o_mla, w_dw, b_dw,
                     g_conv_ln, b_conv_ln, w_conv_out, b_gates, w_out, g_ffn, w_group_router, b_group_router,
                     w_expert_router, b_expert_router, w_e_gate, w_e_up, w_e_down):
    row = lambda a: a.astype(F32).reshape(1, -1)
    s0, s1, s2, s3 = Q_LORA, Q_LORA + KV_LORA, Q_LORA + KV_LORA + ROPE, Q_LORA + KV_LORA + ROPE + 2 * CONV_W
    w = {}
    w["g_mix"] = row(g_mix)
    w["wa"] = jnp.concatenate([w_in[:, :s1], jnp.pad(w_in[:, s1:s2], ((0, 0), (0, LANES - ROPE)))], axis=1).astype(BF16)
    w["wc"] = w_in[:, s2:s3].astype(BF16)
    w["wg"] = w_in[:, s3:].astype(BF16)
    w["b_gates"] = row(b_gates)
    w["g_q_lora"] = row(g_q_lora)
    w["wuq"] = _head_slots(w_uq, QK_DIM).astype(BF16)
    w["g_kv_lora"] = row(g_kv_lora)
    kv3 = w_ukv.reshape(KV_LORA, N_HEADS, NOPE + V_DIM)
    w["wk"] = _head_slots(kv3[:, :, :NOPE].reshape(KV_LORA, N_HEADS * NOPE), NOPE).astype(BF16)
    w["wv"] = kv3[:, :, NOPE:].reshape(KV_LORA, N_HEADS * V_DIM).astype(BF16)
    src = jnp.arange(LANES)[:, None]
    dst = jnp.arange(N_HEADS * HEAD_SLOT)[None, :]
    w["rrep"] = ((src < ROPE) & ((dst % HEAD_SLOT) == (src + NOPE))).astype(BF16)
    w["gq3"] = _gain3(g_qk_q)
    w["gk3"] = _gain3(g_qk_k)
    w["wo"] = w_o_mla.astype(BF16)
    w["w_dw"] = jnp.pad(w_dw.astype(F32), ((0, 32 - CONV_K), (0, 0)))
    w["b_dw"] = row(b_dw)
    w["g_conv_ln"] = row(g_conv_ln)
    w["b_conv_ln"] = row(b_conv_ln)
    w["wco"] = w_conv_out.astype(BF16)
    w["wout"] = w_out.astype(BF16)
    w["g_ffn"] = row(g_ffn)
    pad_r = LANES - N_EXPERTS - N_GROUPS
    wr = jnp.concatenate([w_expert_router, w_group_router, jnp.zeros((D_MODEL, pad_r), F32)], axis=1).astype(F32)
    w["wr_hi"] = wr.astype(BF16)
    w["wr_lo"] = (wr - w["wr_hi"].astype(F32)).astype(BF16)
    w["br"] = jnp.concatenate([b_expert_router, b_group_router, jnp.zeros((pad_r,), F32)]).astype(F32).reshape(1, LANES)
    w["wgu"] = jnp.concatenate([w_e_gate.astype(BF16), w_e_up.astype(BF16)], axis=-1)
    w["wd"] = w_e_down.astype(BF16)
    return w


def kernel(x, positions, g_mix, w_in, g_q_lora, w_uq, g_kv_lora, w_ukv, g_qk_q, g_qk_k, w_o_mla, w_dw, b_dw,
           g_conv_ln, b_conv_ln, w_conv_out, b_gates, w_out, g_ffn, w_group_router, b_group_router,
           w_expert_router, b_expert_router, w_e_gate, w_e_up, w_e_down):
    b, t, d = x.shape
    assert d == D_MODEL and g_mix.shape[0] == 1
    assert t % ATTN_TILE == 0 and t % RANK_CHUNK == 0 and t % COMBINE_TILE == 0
    tm = min(TOKEN_BLOCK, t)
    assert t % tm == 0 and tm % CONV_HALO == 0
    n = b * t
    w = _prepare_weights(g_mix[0], w_in[0], g_q_lora[0], w_uq[0], g_kv_lora[0], w_ukv[0], g_qk_q[0], g_qk_k[0],
                         w_o_mla[0], w_dw[0], b_dw[0], g_conv_ln[0], b_conv_ln[0], w_conv_out[0], b_gates[0],
                         w_out[0], g_ffn[0], w_group_router[0], b_group_router[0], w_expert_router[0],
                         b_expert_router[0], w_e_gate[0], w_e_up[0], w_e_down[0])
    x2 = x.reshape(n, d)
    cos, sin = _rope_tables(positions)
    q, k, v, u, gates = _pre_attention(x2, cos, sin, w, tm)
    attn = _attention(q, k, v, b, t)
    x1, h2p, route = _merge(attn, u, gates, x2, w, t, tm)
    pos, meta = _rank(route, b, t)
    q_flat = pos[:, :2].reshape(n * 2)
    meta3 = meta.reshape(b, SUBLANES, LANES)
    off = meta3[:, 0, :N_EXPERTS].reshape(b * N_EXPERTS)
    cnt = meta3[:, 1, :N_EXPERTS].reshape(b * N_EXPERTS)
    out = _moe(off, cnt, q_flat, h2p.reshape(n * ROW_PIECES, LANES), x1, route, w, b, t)
    return out.reshape(b, t, d)
```

```python
import functools

import jax
import jax.numpy as jnp
import numpy as np
from jax import lax
from jax.experimental import pallas as pl
from jax.experimental.pallas import tpu as pltpu

D_MODEL = 1024
N_HEADS = 8
NOPE = 64
ROPE = 32
ROPE_HALF = ROPE // 2
QK_DIM = NOPE + ROPE
V_DIM = 64
Q_LORA = 256
KV_LORA = 128
CONV_W = 512
CONV_K = 31
N_GROUPS = 4
EXPERTS_PER_GROUP = 8
N_EXPERTS = N_GROUPS * EXPERTS_PER_GROUP
EXPERT_FF = 256
EPS = 1e-6
ROPE_THETA = 10000.0
CHUNK = 64

LANES = 128
SUBLANES = 8
HEAD_SLOT = LANES
VMEM_LIMIT_BYTES = 56 * 1024 * 1024

TOKEN_BLOCK = 512
ATTN_TILE = 256
CONV_HALO = 32
ROW_TILE = 256
EXPERTS_PER_STEP = 2
COMBINE_TILE = 256
RANK_CHUNK = 256
SUB_ROWS = 256
CONV_CHUNK = 32

F32 = jnp.float32
BF16 = jnp.bfloat16
NEG_INF = float("-inf")
LOG2E = 1.4426950408889634


def _dot(a, b, **kw):
    return jnp.dot(a, b, preferred_element_type=F32, **kw)


def _rms(x, g):
    return x * lax.rsqrt(jnp.mean(x * x, axis=-1, keepdims=True) + EPS) * g


def _sigmoid(x):
    return 1.0 / (1.0 + jnp.exp(-x))


def _pack_words(lo, hi):
    return pltpu.pack_elementwise([lo, hi], packed_dtype=BF16)


def _unpack_words(wd):
    lo = pltpu.unpack_elementwise(wd, index=0, packed_dtype=BF16, unpacked_dtype=F32)
    hi = pltpu.unpack_elementwise(wd, index=1, packed_dtype=BF16, unpacked_dtype=F32)
    return lo, hi


PACKED_TILE = (D_MODEL // 2 // LANES, SUBLANES, LANES)
ROW_PIECES = PACKED_TILE[0]


def _store_tile_rows(ref, g0, words, lead=()):
    groups = words.shape[0] // SUBLANES
    for c in range(ROW_PIECES):
        piece = words[:, c * LANES:(c + 1) * LANES].reshape(groups, SUBLANES, LANES)
        ref[lead + (pl.ds(g0, groups), c)] = piece


def _load_tile_rows(ref, g0, rows, lead=()):
    groups = rows // SUBLANES
    pieces = [ref[lead + (pl.ds(g0, groups), c)].reshape(rows, LANES) for c in range(ROW_PIECES)]
    return jnp.concatenate(pieces, axis=1)


def _row_ds(q):
    return pl.ds(q, ROW_PIECES, stride=SUBLANES)


def _params(*sem):
    return pltpu.CompilerParams(dimension_semantics=sem, vmem_limit_bytes=VMEM_LIMIT_BYTES)


def _full(shape):
    nd = len(shape)
    return pl.BlockSpec(shape, lambda *_: (0,) * nd)


def _rope_table_kernel(pos_ref, freq_ref, cos_ref, sin_ref):
    ang = pos_ref[...].astype(F32) * freq_ref[...]
    cos_ref[...] = jnp.cos(ang)
    sin_ref[...] = jnp.sin(ang)


def _rope_tables(positions):
    n = positions.size
    per_row = LANES // ROPE_HALF
    rows = n // per_row
    pos_rep = jnp.repeat(positions.reshape(rows, per_row), ROPE_HALF, axis=1)
    inv_freq = ROPE_THETA ** (-jnp.arange(ROPE_HALF, dtype=F32) / ROPE_HALF)
    freq = jnp.tile(inv_freq, per_row).reshape(1, LANES)
    rb = min(rows, 512)
    cos, sin = pl.pallas_call(
        _rope_table_kernel,
        grid=(rows // rb,),
        in_specs=[pl.BlockSpec((rb, LANES), lambda i: (i, 0)), _full((1, LANES))],
        out_specs=[pl.BlockSpec((rb, LANES), lambda i: (i, 0))] * 2,
        out_shape=[jax.ShapeDtypeStruct((rows, LANES), F32)] * 2,
        compiler_params=_params("parallel"),
        name="rope_tables",
    )(pos_rep, freq)
    return cos.reshape(n, ROPE_HALF), sin.reshape(n, ROPE_HALF)


def _pre_attention_kernel(x_ref, cos_ref, sin_ref, gmix_ref, wa_ref, wc_ref, wg_ref, bg_ref,
                          gql_ref, wuq_ref, gkv_ref, wk_ref, wv_ref, rrep_ref, gq_ref, gk_ref,
                          q_out, k_out, v_out, u_out, gate_out):
    tm = x_ref.shape[0]
    hb = _rms(x_ref[...], gmix_ref[...]).astype(BF16)

    a = _dot(hb, wa_ref[...])
    qdn = _rms(a[:, :Q_LORA], gql_ref[...]).astype(BF16)
    kvn = _rms(a[:, Q_LORA:Q_LORA + KV_LORA], gkv_ref[...]).astype(BF16)
    kr = a[:, Q_LORA + KV_LORA:].astype(BF16)
    qf = _dot(qdn, wuq_ref[...])
    kf = _dot(kvn, wk_ref[...]) + _dot(kr, rrep_ref[...])
    v_out[...] = _dot(kvn, wv_ref[...]).astype(BF16)

    cos = cos_ref[...]
    sin = sin_ref[...]
    z = lambda w: jnp.zeros((tm, w), F32)
    c_tab = jnp.concatenate([jnp.ones((tm, NOPE), F32), cos, cos, z(HEAD_SLOT - QK_DIM)], axis=-1)
    s_up = jnp.concatenate([z(NOPE + ROPE_HALF), sin, z(HEAD_SLOT - QK_DIM)], axis=-1)
    s_dn = jnp.concatenate([z(NOPE), -sin, z(HEAD_SLOT - NOPE - ROPE_HALF)], axis=-1)

    def norm_rope(f, g_ref, out, scale):
        ta = c_tab * g_ref[0:1, :]
        tu = s_up * g_ref[1:2, :]
        td = s_dn * g_ref[2:3, :]
        for h in range(N_HEADS):
            xs = f[:, h * HEAD_SLOT:(h + 1) * HEAD_SLOT]
            ss = jnp.sum(xs * xs, axis=-1, keepdims=True)
            s = lax.rsqrt(ss * (1.0 / QK_DIM) + EPS) * scale
            y = xs * ta + pltpu.roll(xs, ROPE_HALF, 1) * tu + pltpu.roll(xs, HEAD_SLOT - ROPE_HALF, 1) * td
            out[:, h * HEAD_SLOT:(h + 1) * HEAD_SLOT] = (y * s).astype(BF16)

    norm_rope(qf, gq_ref, q_out, LOG2E * QK_DIM ** -0.5)
    norm_rope(kf, gk_ref, k_out, 1.0)

    c = _dot(hb, wc_ref[...])
    u_out[...] = c[:, :CONV_W] * _sigmoid(c[:, CONV_W:])
    gate_out[...] = _sigmoid(_dot(hb, wg_ref[...]) + bg_ref[...]).astype(BF16)


def _pre_attention(x2, cos, sin, w, tm):
    n = x2.shape[0]
    row = lambda width: pl.BlockSpec((tm, width), lambda i: (i, 0))
    hs = N_HEADS * HEAD_SLOT
    weights = [w["g_mix"], w["wa"], w["wc"], w["wg"], w["b_gates"], w["g_q_lora"], w["wuq"],
               w["g_kv_lora"], w["wk"], w["wv"], w["rrep"], w["gq3"], w["gk3"]]
    return pl.pallas_call(
        _pre_attention_kernel,
        grid=(n // tm,),
        in_specs=[row(D_MODEL), row(ROPE_HALF), row(ROPE_HALF)] + [_full(a.shape) for a in weights],
        out_specs=[row(hs), row(hs), row(N_HEADS * V_DIM), row(CONV_W), row(2 * D_MODEL)],
        out_shape=[jax.ShapeDtypeStruct((n, hs), BF16), jax.ShapeDtypeStruct((n, hs), BF16),
                   jax.ShapeDtypeStruct((n, N_HEADS * V_DIM), BF16), jax.ShapeDtypeStruct((n, CONV_W), F32),
                   jax.ShapeDtypeStruct((n, 2 * D_MODEL), BF16)],
        compiler_params=_params("parallel"),
        name="pre_attention",
    )(x2, cos, sin, *weights)


def _attention_kernel(q_ref, k_ref, v_ref, o_ref, kbd, vbd):
    t = ATTN_TILE
    hk = HEAD_SLOT
    seq = q_ref.shape[0]
    lane = lax.broadcasted_iota(jnp.int32, (hk, HEAD_SLOT), 1)
    zero = jnp.zeros((hk, HEAD_SLOT), BF16)
    for c in range(seq // hk):
        rows = slice(c * hk, (c + 1) * hk)
        top = slice(2 * c * hk, (2 * c + 1) * hk)
        bot = slice((2 * c + 1) * hk, (2 * c + 2) * hk)
        kbd[top, :HEAD_SLOT] = k_ref[rows, :HEAD_SLOT]
        kbd[top, HEAD_SLOT:] = zero
        kbd[bot, :HEAD_SLOT] = zero
        kbd[bot, HEAD_SLOT:] = k_ref[rows, HEAD_SLOT:]
        vp = v_ref[rows, :]
        vbd[top, :] = jnp.where(lane < V_DIM, vp, zero)
        vbd[bot, :] = jnp.where(lane >= V_DIM, vp, zero)

    qrow = lax.broadcasted_iota(jnp.int32, (t, hk), 0) // CHUNK
    kcol = lax.broadcasted_iota(jnp.int32, (t, hk), 1)
    diag_masks = [(kcol + h2 * hk) // CHUNK <= qrow for h2 in range(t // hk)]
    out_lane = lax.broadcasted_iota(jnp.int32, (t, HEAD_SLOT), 1)
    for i in range(seq // t):
        rows = slice(i * t, (i + 1) * t)
        n_tiles = 2 * (i + 1) * t // hk
        s = lax.dot_general(q_ref[rows, :], kbd[0:n_tiles * hk, :], (((1,), (1,)), ((), ())),
                            preferred_element_type=F32)
        tiles = [s[:, j * hk:(j + 1) * hk] for j in range(n_tiles)]
        first_diag = n_tiles - 2 * (t // hk)
        for j in range(first_diag, n_tiles):
            tiles[j] = jnp.where(diag_masks[(j - first_diag) // 2], tiles[j], NEG_INF)
        probs = [None] * n_tiles
        inv_l = []
        for h in range(2):
            mine = range(h, n_tiles, 2)
            m = jnp.max(functools.reduce(jnp.maximum, [tiles[j] for j in mine]), axis=-1, keepdims=True)
            for j in mine:
                probs[j] = jnp.exp2(tiles[j] - m)
            l = jnp.sum(functools.reduce(jnp.add, [probs[j] for j in mine]), axis=-1, keepdims=True)
            inv_l.append(1.0 / l)
        p = jnp.concatenate(probs, axis=1).astype(BF16)
        o = _dot(p, vbd[0:n_tiles * hk, :])
        o_ref[rows, :] = (o * jnp.where(out_lane < V_DIM, inv_l[0], inv_l[1])).astype(BF16)


def _attention(q, k, v, b, t):
    hs2 = 2 * HEAD_SLOT
    spec = pl.BlockSpec((t, hs2), lambda bi, p: (bi, p))
    vspec = pl.BlockSpec((t, 2 * V_DIM), lambda bi, p: (bi, p))
    return pl.pallas_call(
        _attention_kernel,
        grid=(b, N_HEADS // 2),
        in_specs=[spec, spec, vspec],
        out_specs=pl.BlockSpec((t, 2 * V_DIM), lambda bi, p: (bi, p)),
        out_shape=jax.ShapeDtypeStruct((b * t, N_HEADS * V_DIM), BF16),
        scratch_shapes=[pltpu.VMEM((2 * t, hs2), BF16), pltpu.VMEM((2 * t, 2 * V_DIM), BF16)],
        compiler_params=_params("parallel", "parallel"),
        name="attention",
    )(q, k, v)


def _merge_kernel(blocks_per_seq, attn_ref, u_ref, halo_ref, gate_ref, x_ref, wo_ref, wdw_ref, bdw_ref,
                  gln_ref, bln_ref, wco_ref, wout_ref, gffn_ref, wrh_ref, wrl_ref, br_ref,
                  x1_out, h2p_out, route_out, sh_ref):
    tm = x_ref.shape[0]
    first = (pl.program_id(0) % blocks_per_seq) == 0
    halo = halo_ref[...]
    sh_ref[0, 0:CONV_HALO, :] = jnp.where(first, jnp.zeros_like(halo), halo)
    sh_ref[0, CONV_HALO:, :] = u_ref[...]
    span = tm + CONV_HALO - SUBLANES
    for s in range(1, SUBLANES):
        sh_ref[s, 0:span, :] = sh_ref[0, s:s + span, :]

    base = CONV_HALO - (CONV_K - 1)
    half = D_MODEL // 2
    for sb in range(tm // SUB_ROWS):
        rows = slice(sb * SUB_ROWS, (sb + 1) * SUB_ROWS)
        parts = []
        for ck in range(SUB_ROWS // CONV_CHUNK):
            c0 = sb * SUB_ROWS + ck * CONV_CHUNK
            acc = jnp.broadcast_to(bdw_ref[...], (CONV_CHUNK, CONV_W))
            for j in range(CONV_K):
                s, a = (base + j) % SUBLANES, (base + j) // SUBLANES * SUBLANES
                acc = acc + sh_ref[s, c0 + a:c0 + a + CONV_CHUNK, :] * wdw_ref[j:j + 1, :]
            parts.append(acc)
        conv = jnp.concatenate(parts, axis=0)
        mu = jnp.mean(conv, axis=-1, keepdims=True)
        cen = conv - mu
        var = jnp.mean(cen * cen, axis=-1, keepdims=True)
        ln = cen * lax.rsqrt(var + EPS) * gln_ref[...] + bln_ref[...]
        ub = (ln * _sigmoid(ln)).astype(BF16)
        y_b = _dot(ub, wco_ref[...])
        y_a = _dot(attn_ref[rows, :], wo_ref[...])
        merged = gate_ref[rows, :D_MODEL].astype(F32) * y_a + gate_ref[rows, D_MODEL:].astype(F32) * y_b
        x1 = x_ref[rows, :] + _dot(merged.astype(BF16), wout_ref[...])
        x1_out[rows, :] = x1
        h2 = _rms(x1, gffn_ref[...])
        _store_tile_rows(h2p_out, sb * (SUB_ROWS // SUBLANES), _pack_words(h2[:, :half], h2[:, half:]))
        h2_hi = h2.astype(BF16)
        h2_lo = (h2 - h2_hi.astype(F32)).astype(BF16)
        logits = (_dot(h2_hi, wrh_ref[...]) + _dot(h2_lo, wrh_ref[...]) + _dot(h2_hi, wrl_ref[...])
                  + br_ref[...])
        route_out[rows, :] = _route(logits)


def _route(logits):
    lane_i = lax.broadcasted_iota(jnp.int32, logits.shape, 1)
    lane = lane_i.astype(F32)
    big = float(LANES)
    is_g = (lane_i >= N_EXPERTS) & (lane_i < N_EXPERTS + N_GROUPS)
    gl = jnp.where(is_g, logits, NEG_INF)
    gmax = jnp.max(gl, axis=-1, keepdims=True)
    gidx = jnp.min(jnp.where(gl == gmax, lane - N_EXPERTS, big), axis=-1, keepdims=True)
    g_p = 1.0 / jnp.sum(jnp.exp(gl - gmax), axis=-1, keepdims=True)
    lane_group = (lane_i // EXPERTS_PER_GROUP).astype(F32)
    valid = (lane_i < N_EXPERTS) & (lane_group == gidx)
    el = jnp.where(valid, logits, NEG_INF)
    ee = jnp.exp(el - jnp.max(el, axis=-1, keepdims=True))
    within = jnp.where(valid, ee / jnp.sum(ee, axis=-1, keepdims=True), -1.0)
    p1 = jnp.max(within, axis=-1, keepdims=True)
    i1 = jnp.min(jnp.where(within == p1, lane, big), axis=-1, keepdims=True)
    within2 = jnp.where(lane == i1, -1.0, within)
    p2 = jnp.max(within2, axis=-1, keepdims=True)
    i2 = jnp.min(jnp.where(within2 == p2, lane, big), axis=-1, keepdims=True)
    psum = p1 + p2
    w1 = g_p * (p1 / psum)
    w2 = g_p * (p2 / psum)
    route = jnp.where(lane_i == 0, i1, 0.0)
    route = jnp.where(lane_i == 1, i2, route)
    route = jnp.where(lane_i == 2, w1, route)
    return jnp.where(lane_i == 3, w2, route)


def _merge(attn, u, gates, x2, w, t, tm):
    n = x2.shape[0]
    bps = t // tm
    row = lambda width: pl.BlockSpec((tm, width), lambda i: (i, 0))
    hpb = tm // CONV_HALO
    halo = pl.BlockSpec((CONV_HALO, CONV_W), lambda i: (jnp.maximum(i * hpb - 1, 0), 0))
    weights = [w["wo"], w["w_dw"], w["b_dw"], w["g_conv_ln"], w["b_conv_ln"], w["wco"], w["wout"],
               w["g_ffn"], w["wr_hi"], w["wr_lo"], w["br"]]
    tile_rows = (tm // SUBLANES,) + PACKED_TILE
    return pl.pallas_call(
        functools.partial(_merge_kernel, bps),
        grid=(n // tm,),
        in_specs=[row(N_HEADS * V_DIM), row(CONV_W), halo, row(2 * D_MODEL), row(D_MODEL)]
        + [_full(a.shape) for a in weights],
        out_specs=[row(D_MODEL), pl.BlockSpec(tile_rows, lambda i: (i, 0, 0, 0)), row(LANES)],
        out_shape=[jax.ShapeDtypeStruct((n, D_MODEL), F32),
                   jax.ShapeDtypeStruct((n // SUBLANES,) + PACKED_TILE, jnp.uint32),
                   jax.ShapeDtypeStruct((n, LANES), F32)],
        scratch_shapes=[pltpu.VMEM((SUBLANES, tm + CONV_HALO, CONV_W), F32)],
        compiler_params=_params("parallel"),
        name="merge_router",
    )(attn, u, u, gates, x2, *weights)


def _rank_kernel(route_ref, pos_out, meta_out, r1_sc, r2_sc):
    tb = route_ref.shape[0]
    lane = lax.broadcasted_iota(jnp.int32, (RANK_CHUNK, LANES), 1).astype(F32)
    rr = lax.broadcasted_iota(jnp.int32, (RANK_CHUNK, RANK_CHUNK), 0)
    cc = lax.broadcasted_iota(jnp.int32, (RANK_CHUNK, RANK_CHUNK), 1)
    lower = jnp.where(rr > cc, 1.0, 0.0).astype(BF16)
    run = jnp.zeros((1, LANES), F32)
    for choice, sc in ((0, r1_sc), (1, r2_sc)):
        for ci in range(tb // RANK_CHUNK):
            rows = slice(ci * RANK_CHUNK, (ci + 1) * RANK_CHUNK)
            oh = jnp.where(lane == route_ref[rows, choice:choice + 1], 1.0, 0.0)
            sc[rows, :] = run + _dot(lower, oh.astype(BF16))
            run = run + jnp.sum(oh, axis=0, keepdims=True)
    cnt = run
    cnt8 = jnp.ceil(cnt * (1.0 / SUBLANES)) * SUBLANES
    ur = lax.broadcasted_iota(jnp.int32, (LANES, LANES), 0)
    uc = lax.broadcasted_iota(jnp.int32, (LANES, LANES), 1)
    upper = jnp.where(ur < uc, 1.0, 0.0)
    off = _dot(jnp.broadcast_to(cnt8, (SUBLANES, LANES)), upper, precision=lax.Precision.HIGHEST)[0:1, :]
    lane_t = lax.broadcasted_iota(jnp.int32, (tb, LANES), 1)
    lane_tf = lane_t.astype(F32)
    p1 = jnp.sum(jnp.where(lane_tf == route_ref[:, 0:1], off + r1_sc[...], 0.0), axis=-1, keepdims=True)
    p2 = jnp.sum(jnp.where(lane_tf == route_ref[:, 1:2], off + r2_sc[...], 0.0), axis=-1, keepdims=True)
    q1 = p1 + (SUBLANES * ROW_PIECES - SUBLANES) * jnp.floor(p1 * (1.0 / SUBLANES))
    q2 = p2 + (SUBLANES * ROW_PIECES - SUBLANES) * jnp.floor(p2 * (1.0 / SUBLANES))
    pos = jnp.where(lane_t == 0, q1, jnp.where(lane_t == 1, q2, 0.0))
    pos_out[...] = pos.astype(jnp.int32)
    row8 = lax.broadcasted_iota(jnp.int32, (SUBLANES, LANES), 0)
    meta = jnp.where(row8 == 0, jnp.broadcast_to(off, (SUBLANES, LANES)),
                     jnp.where(row8 == 1, jnp.broadcast_to(cnt, (SUBLANES, LANES)), 0.0))
    meta_out[...] = meta.astype(jnp.int32)


def _rank(route, b, t):
    return pl.pallas_call(
        _rank_kernel,
        grid=(b,),
        in_specs=[pl.BlockSpec((t, LANES), lambda i: (i, 0))],
        out_specs=[pl.BlockSpec((t, LANES), lambda i: (i, 0)), pl.BlockSpec((SUBLANES, LANES), lambda i: (i, 0))],
        out_shape=[jax.ShapeDtypeStruct((b * t, LANES), jnp.int32),
                   jax.ShapeDtypeStruct((b * SUBLANES, LANES), jnp.int32)],
        scratch_shapes=[pltpu.VMEM((t, LANES), F32), pltpu.VMEM((t, LANES), F32)],
        compiler_params=_params("parallel"),
        name="rank",
    )(route)


def _sorted_rows(t):
    return 2 * t + N_EXPERTS * SUBLANES + ROW_TILE


GROUP_ROWS = SUBLANES * ROW_PIECES


def _load_flat_tile(ref, base, rows):
    cols = []
    for c in range(ROW_PIECES):
        cols.append(jnp.concatenate(
            [ref[pl.ds(base + g * GROUP_ROWS + c * SUBLANES, SUBLANES), :] for g in range(rows // SUBLANES)], axis=0))
    return jnp.concatenate(cols, axis=1)


def _store_flat_tile(ref, base, words):
    for c in range(ROW_PIECES):
        for g in range(words.shape[0] // SUBLANES):
            ref[pl.ds(base + g * GROUP_ROWS + c * SUBLANES, SUBLANES), :] = (
                words[g * SUBLANES:(g + 1) * SUBLANES, c * LANES:(c + 1) * LANES])


def _moe_kernel(n_seq, off_ref, cnt_ref, q_disp_ref, q_comb_ref, h2p_ref, wgu_ref, wd_ref, x1_ref, route_ref,
                out_ref, xs0, xs1, ys0, ys1, g1, g2):
    o = pl.program_id(0)
    s = pl.program_id(1)
    ts = x1_ref.shape[0]
    half = D_MODEL // 2
    seq_e = jnp.clip(o - 1, 0, n_seq - 1)

    @pl.when((o == 0) & (s == 0))
    def _init():
        for ref in (xs0, xs1, ys0, ys1):
            ref[...] = jnp.zeros(ref.shape, ref.dtype)

    def expert_tile(xs_e, ys_e, k, base, i):
        start = pl.multiple_of(jnp.where(i >= 0, base + i * (ROW_TILE * ROW_PIECES),
                                         xs_e.shape[0] - ROW_TILE * ROW_PIECES), GROUP_ROWS)
        lo, hi = _unpack_words(_load_flat_tile(xs_e, start, ROW_TILE))
        xb = jnp.concatenate([lo.astype(BF16), hi.astype(BF16)], axis=1)
        au = _dot(xb, wgu_ref[k])
        a = au[:, :EXPERT_FF]
        mid = (a * _sigmoid(a) * au[:, EXPERT_FF:]).astype(BF16)
        y = _dot(mid, wd_ref[k])
        _store_flat_tile(ys_e, start, _pack_words(y[:, :half], y[:, half:]))

    def body(xs_d, xs_e, ys_e, ys_c):
        @pl.when(s == 0)
        def _clear():
            xs_d[...] = jnp.zeros(xs_d.shape, xs_d.dtype)

        bases, tiles = [], []
        for k in range(EXPERTS_PER_STEP):
            idx = seq_e * N_EXPERTS + s * EXPERTS_PER_STEP + k
            bases.append(off_ref[idx] * ROW_PIECES)
            tiles.append(lax.shift_right_logical(cnt_ref[idx] + (ROW_TILE - 1), ROW_TILE.bit_length() - 1))
        for k in range(EXPERTS_PER_STEP):
            expert_tile(xs_e, ys_e, k, bases[k], tiles[k] - 1)

        qd = s * (2 * ts)
        for g in range(ts // SUBLANES):
            for r in range(SUBLANES):
                row = h2p_ref[_row_ds(g * GROUP_ROWS + r), :]
                tok = g * SUBLANES + r
                xs_d[_row_ds(q_disp_ref[qd + 2 * tok]), :] = row
                xs_d[_row_ds(q_disp_ref[qd + 2 * tok + 1]), :] = row

        for g in range(ts // SUBLANES):
            for r in range(SUBLANES):
                tok = g * SUBLANES + r
                g1[_row_ds(g * GROUP_ROWS + r), :] = ys_c[_row_ds(q_comb_ref[qd + 2 * tok]), :]
                g2[_row_ds(g * GROUP_ROWS + r), :] = ys_c[_row_ds(q_comb_ref[qd + 2 * tok + 1]), :]
        w1 = route_ref[:, 2:3]
        w2 = route_ref[:, 3:4]
        for c in range(ROW_PIECES):
            tile_of = lambda ref: jnp.concatenate(
                [ref[g * GROUP_ROWS + c * SUBLANES:g * GROUP_ROWS + (c + 1) * SUBLANES, :]
                 for g in range(ts // SUBLANES)], axis=0)
            lo1, hi1 = _unpack_words(tile_of(g1))
            lo2, hi2 = _unpack_words(tile_of(g2))
            cols = slice(c * LANES, (c + 1) * LANES)
            cols_hi = slice(half + c * LANES, half + (c + 1) * LANES)
            out_ref[:, cols] = x1_ref[:, cols] + (w1 * lo1 + w2 * lo2)
            out_ref[:, cols_hi] = x1_ref[:, cols_hi] + (w1 * hi1 + w2 * hi2)

        def more(j, carry):
            for k in range(EXPERTS_PER_STEP):
                expert_tile(xs_e, ys_e, k, bases[k], tiles[k] - 1 - j)
            return carry

        lax.fori_loop(1, functools.reduce(jnp.maximum, tiles), more, 0)

    @pl.when(o % 2 == 0)
    def _even():
        body(xs0, xs1, ys1, ys0)

    @pl.when(o % 2 == 1)
    def _odd():
        body(xs1, xs0, ys0, ys1)


def _moe(off, cnt, q_flat, h2p_flat, x1, route, w, b, t):
    flat_rows = _sorted_rows(t) * ROW_PIECES
    n_steps = N_EXPERTS // EXPERTS_PER_STEP
    assert t % (n_steps * SUBLANES) == 0
    ts = t // n_steps
    seq_d = lambda o: jnp.minimum(o, b - 1)
    seq_c = lambda o: jnp.clip(o - 2, 0, b - 1)
    chunk = lambda o, s, *_: (jnp.where(o < 2, 0, seq_c(o) * n_steps + s), 0)
    wmap = lambda o, s, *_: (s, 0, 0)
    smem = lambda f: pl.BlockSpec((2 * t,), lambda o, s, *_: (f(o),), memory_space=pltpu.SMEM)
    grid_spec = pltpu.PrefetchScalarGridSpec(
        num_scalar_prefetch=2,
        grid=(b + 2, n_steps),
        in_specs=[smem(seq_d), smem(seq_c),
                  pl.BlockSpec((ts * ROW_PIECES, LANES), lambda o, s, *_: (seq_d(o) * n_steps + s, 0)),
                  pl.BlockSpec((EXPERTS_PER_STEP, D_MODEL, 2 * EXPERT_FF), wmap),
                  pl.BlockSpec((EXPERTS_PER_STEP, EXPERT_FF, D_MODEL), wmap),
                  pl.BlockSpec((ts, D_MODEL), chunk),
                  pl.BlockSpec((ts, LANES), chunk)],
        out_specs=pl.BlockSpec((ts, D_MODEL), chunk),
        scratch_shapes=[pltpu.VMEM((flat_rows, LANES), jnp.uint32)] * 4
        + [pltpu.VMEM((ts * ROW_PIECES, LANES), jnp.uint32)] * 2,
    )
    return pl.pallas_call(
        functools.partial(_moe_kernel, b),
        grid_spec=grid_spec,
        out_shape=jax.ShapeDtypeStruct((b * t, D_MODEL), F32),
        compiler_params=_params("arbitrary", "arbitrary"),
        name="moe",
    )(off, cnt, q_flat, q_flat, h2p_flat, w["wgu"], w["wd"], x1, route)


def _head_slots(wmat, width):
    k = wmat.shape[0]
    w3 = wmat.reshape(k, N_HEADS, width)
    return jnp.pad(w3, ((0, 0), (0, 0), (0, HEAD_SLOT - width))).reshape(k, N_HEADS * HEAD_SLOT)


def _gain3(g):
    gp = jnp.pad(g.astype(F32), (0, HEAD_SLOT - QK_DIM))
    return jnp.stack([gp, jnp.roll(gp, ROPE_HALF), jnp.roll(gp, -ROPE_HALF)])


def _prepare_weights(g_mix, w_in, g_q_lora, w_uq, g_kv_lora, w_ukv, g_qk_q, g_qk_k, w_o_mla, w_dw, b_dw,
                     g_conv_ln, b_conv_ln, w_conv_out, b_gates, w_out, g_ffn, w_group_router, b_group_router,
                     w_expert_router, b_expert_router, w_e_gate, w_e_up, w_e_down):
    row = lambda a: a.astype(F32).reshape(1, -1)
    s0, s1, s2, s3 = Q_LORA, Q_LORA + KV_LORA, Q_LORA + KV_LORA + ROPE, Q_LORA + KV_LORA + ROPE + 2 * CONV_W
    w = {}
    w["g_mix"] = row(g_mix)
    w["wa"] = jnp.concatenate([w_in[:, :s1], jnp.pad(w_in[:, s1:s2], ((0, 0), (0, LANES - ROPE)))], axis=1).astype(BF16)
    w["wc"] = w_in[:, s2:s3].astype(BF16)
    w["wg"] = w_in[:, s3:].astype(BF16)
    w["b_gates"] = row(b_gates)
    w["g_q_lora"] = row(g_q_lora)
    w["wuq"] = _head_slots(w_uq, QK_DIM).astype(BF16)
    w["g_kv_lora"] = row(g_kv_lora)
    kv3 = w_ukv.reshape(KV_LORA, N_HEADS, NOPE + V_DIM)
    w["wk"] = _head_slots(kv3[:, :, :NOPE].reshape(KV_LORA, N_HEADS * NOPE), NOPE).astype(BF16)
    w["wv"] = kv3[:, :, NOPE:].reshape(KV_LORA, N_HEADS * V_DIM).astype(BF16)
    src = jnp.arange(LANES)[:, None]
    dst = jnp.arange(N_HEADS * HEAD_SLOT)[None, :]
    w["rrep"] = ((src < ROPE) & ((dst % HEAD_SLOT) == (src + NOPE))).astype(BF16)
    w["gq3"] = _gain3(g_qk_q)
    w["gk3"] = _gain3(g_qk_k)
    w["wo"] = w_o_mla.astype(BF16)
    w["w_dw"] = jnp.pad(w_dw.astype(F32), ((0, 32 - CONV_K), (0, 0)))
    w["b_dw"] = row(b_dw)
    w["g_conv_ln"] = row(g_conv_ln)
    w["b_conv_ln"] = row(b_conv_ln)
    w["wco"] = w_conv_out.astype(BF16)
    w["wout"] = w_out.astype(BF16)
    w["g_ffn"] = row(g_ffn)
    pad_r = LANES - N_EXPERTS - N_GROUPS
    wr = jnp.concatenate([w_expert_router, w_group_router, jnp.zeros((D_MODEL, pad_r), F32)], axis=1).astype(F32)
    w["wr_hi"] = wr.astype(BF16)
    w["wr_lo"] = (wr - w["wr_hi"].astype(F32)).astype(BF16)
    w["br"] = jnp.concatenate([b_expert_router, b_group_router, jnp.zeros((pad_r,), F32)]).astype(F32).reshape(1, LANES)
    w["wgu"] = jnp.concatenate([w_e_gate.astype(BF16), w_e_up.astype(BF16)], axis=-1)
    w["wd"] = w_e_down.astype(BF16)
    return w


def kernel(x, positions, g_mix, w_in, g_q_lora, w_uq, g_kv_lora, w_ukv, g_qk_q, g_qk_k, w_o_mla, w_dw, b_dw,
           g_conv_ln, b_conv_ln, w_conv_out, b_gates, w_out, g_ffn, w_group_router, b_group_router,
           w_expert_router, b_expert_router, w_e_gate, w_e_up, w_e_down):
    b, t, d = x.shape
    assert d == D_MODEL and g_mix.shape[0] == 1
    assert t % ATTN_TILE == 0 and t % RANK_CHUNK == 0 and t % COMBINE_TILE == 0
    tm = min(TOKEN_BLOCK, t)
    assert t % tm == 0 and tm % CONV_HALO == 0
    n = b * t
    w = _prepare_weights(g_mix[0], w_in[0], g_q_lora[0], w_uq[0], g_kv_lora[0], w_ukv[0], g_qk_q[0], g_qk_k[0],
                         w_o_mla[0], w_dw[0], b_dw[0], g_conv_ln[0], b_conv_ln[0], w_conv_out[0], b_gates[0],
                         w_out[0], g_ffn[0], w_group_router[0], b_group_router[0], w_expert_router[0],
                         b_expert_router[0], w_e_gate[0], w_e_up[0], w_e_down[0])
    x2 = x.reshape(n, d)
    cos, sin = _rope_tables(positions)
    q, k, v, u, gates = _pre_attention(x2, cos, sin, w, tm)
    attn = _attention(q, k, v, b, t)
    x1, h2p, route = _merge(attn, u, gates, x2, w, t, tm)
    pos, meta = _rank(route, b, t)
    q_flat = pos[:, :2].reshape(n * 2)
    meta3 = meta.reshape(b, SUBLANES, LANES)
    off = meta3[:, 0, :N_EXPERTS].reshape(b * N_EXPERTS)
    cnt = meta3[:, 1, :N_EXPERTS].reshape(b * N_EXPERTS)
    out = _moe(off, cnt, q_flat, h2p.reshape(n * ROW_PIECES, LANES), x1, route, w, b, t)
    return out.reshape(b, t, d)
```

```python
import functools

import jax
import jax.numpy as jnp
import numpy as np
from jax import lax
from jax.experimental import pallas as pl
from jax.experimental.pallas import tpu as pltpu

D_MODEL = 1024
N_HEADS = 8
NOPE = 64
ROPE = 32
ROPE_HALF = ROPE // 2
QK_DIM = NOPE + ROPE
V_DIM = 64
Q_LORA = 256
KV_LORA = 128
CONV_W = 512
CONV_K = 31
N_GROUPS = 4
EXPERTS_PER_GROUP = 8
N_EXPERTS = N_GROUPS * EXPERTS_PER_GROUP
EXPERT_FF = 256
EPS = 1e-6
ROPE_THETA = 10000.0
CHUNK = 64

LANES = 128
SUBLANES = 8
HEAD_SLOT = LANES
VMEM_LIMIT_BYTES = 56 * 1024 * 1024

TOKEN_BLOCK = 512
ATTN_TILE = 256
CONV_HALO = 32
ROW_TILE = 256
EXPERTS_PER_STEP = 4
COMBINE_TILE = 256
RANK_CHUNK = 256
SUB_ROWS = 256
CONV_CHUNK = 32

F32 = jnp.float32
BF16 = jnp.bfloat16
NEG_INF = float("-inf")
LOG2E = 1.4426950408889634


def _dot(a, b, **kw):
    return jnp.dot(a, b, preferred_element_type=F32, **kw)


def _rms(x, g):
    return x * lax.rsqrt(jnp.mean(x * x, axis=-1, keepdims=True) + EPS) * g


def _sigmoid(x):
    return 1.0 / (1.0 + jnp.exp(-x))


def _pack_words(lo, hi):
    return pltpu.pack_elementwise([lo, hi], packed_dtype=BF16)


def _unpack_words(wd):
    lo = pltpu.unpack_elementwise(wd, index=0, packed_dtype=BF16, unpacked_dtype=F32)
    hi = pltpu.unpack_elementwise(wd, index=1, packed_dtype=BF16, unpacked_dtype=F32)
    return lo, hi


PACKED_TILE = (D_MODEL // 2 // LANES, SUBLANES, LANES)
ROW_PIECES = PACKED_TILE[0]


def _store_tile_rows(ref, g0, words, lead=()):
    groups = words.shape[0] // SUBLANES
    for c in range(ROW_PIECES):
        piece = words[:, c * LANES:(c + 1) * LANES].reshape(groups, SUBLANES, LANES)
        ref[lead + (pl.ds(g0, groups), c)] = piece


def _load_tile_rows(ref, g0, rows, lead=()):
    groups = rows // SUBLANES
    pieces = [ref[lead + (pl.ds(g0, groups), c)].reshape(rows, LANES) for c in range(ROW_PIECES)]
    return jnp.concatenate(pieces, axis=1)


def _row_ds(q):
    return pl.ds(q, ROW_PIECES, stride=SUBLANES)


def _params(*sem):
    return pltpu.CompilerParams(dimension_semantics=sem, vmem_limit_bytes=VMEM_LIMIT_BYTES)


def _full(shape):
    nd = len(shape)
    return pl.BlockSpec(shape, lambda *_: (0,) * nd)


def _rope_table_kernel(pos_ref, freq_ref, cos_ref, sin_ref):
    ang = pos_ref[...].astype(F32) * freq_ref[...]
    cos_ref[...] = jnp.cos(ang)
    sin_ref[...] = jnp.sin(ang)


def _rope_tables(positions):
    n = positions.size
    per_row = LANES // ROPE_HALF
    rows = n // per_row
    pos_rep = jnp.repeat(positions.reshape(rows, per_row), ROPE_HALF, axis=1)
    inv_freq = ROPE_THETA ** (-jnp.arange(ROPE_HALF, dtype=F32) / ROPE_HALF)
    freq = jnp.tile(inv_freq, per_row).reshape(1, LANES)
    rb = min(rows, 512)
    cos, sin = pl.pallas_call(
        _rope_table_kernel,
        grid=(rows // rb,),
        in_specs=[pl.BlockSpec((rb, LANES), lambda i: (i, 0)), _full((1, LANES))],
        out_specs=[pl.BlockSpec((rb, LANES), lambda i: (i, 0))] * 2,
        out_shape=[jax.ShapeDtypeStruct((rows, LANES), F32)] * 2,
        compiler_params=_params("parallel"),
        name="rope_tables",
    )(pos_rep, freq)
    return cos.reshape(n, ROPE_HALF), sin.reshape(n, ROPE_HALF)


def _pre_attention_kernel(x_ref, cos_ref, sin_ref, gmix_ref, wa_ref, wc_ref, wg_ref, bg_ref,
                          gql_ref, wuq_ref, gkv_ref, wk_ref, wv_ref, rrep_ref, gq_ref, gk_ref,
                          q_out, k_out, v_out, u_out, gate_out):
    tm = x_ref.shape[0]
    hb = _rms(x_ref[...], gmix_ref[...]).astype(BF16)

    a = _dot(hb, wa_ref[...])
    qdn = _rms(a[:, :Q_LORA], gql_ref[...]).astype(BF16)
    kvn = _rms(a[:, Q_LORA:Q_LORA + KV_LORA], gkv_ref[...]).astype(BF16)
    kr = a[:, Q_LORA + KV_LORA:].astype(BF16)
    qf = _dot(qdn, wuq_ref[...])
    kf = _dot(kvn, wk_ref[...]) + _dot(kr, rrep_ref[...])
    v_out[...] = _dot(kvn, wv_ref[...]).astype(BF16)

    cos = cos_ref[...]
    sin = sin_ref[...]
    z = lambda w: jnp.zeros((tm, w), F32)
    c_tab = jnp.concatenate([jnp.ones((tm, NOPE), F32), cos, cos, z(HEAD_SLOT - QK_DIM)], axis=-1)
    s_up = jnp.concatenate([z(NOPE + ROPE_HALF), sin, z(HEAD_SLOT - QK_DIM)], axis=-1)
    s_dn = jnp.concatenate([z(NOPE), -sin, z(HEAD_SLOT - NOPE - ROPE_HALF)], axis=-1)

    def norm_rope(f, g_ref, out, scale):
        ta = c_tab * g_ref[0:1, :]
        tu = s_up * g_ref[1:2, :]
        td = s_dn * g_ref[2:3, :]
        for h in range(N_HEADS):
            xs = f[:, h * HEAD_SLOT:(h + 1) * HEAD_SLOT]
            ss = jnp.sum(xs * xs, axis=-1, keepdims=True)
            s = lax.rsqrt(ss * (1.0 / QK_DIM) + EPS) * scale
            y = xs * ta + pltpu.roll(xs, ROPE_HALF, 1) * tu + pltpu.roll(xs, HEAD_SLOT - ROPE_HALF, 1) * td
            out[:, h * HEAD_SLOT:(h + 1) * HEAD_SLOT] = (y * s).astype(BF16)

    norm_rope(qf, gq_ref, q_out, LOG2E * QK_DIM ** -0.5)
    norm_rope(kf, gk_ref, k_out, 1.0)

    c = _dot(hb, wc_ref[...])
    u_out[...] = c[:, :CONV_W] * _sigmoid(c[:, CONV_W:])
    gate_out[...] = _sigmoid(_dot(hb, wg_ref[...]) + bg_ref[...]).astype(BF16)


def _pre_attention(x2, cos, sin, w, tm):
    n = x2.shape[0]
    row = lambda width: pl.BlockSpec((tm, width), lambda i: (i, 0))
    hs = N_HEADS * HEAD_SLOT
    weights = [w["g_mix"], w["wa"], w["wc"], w["wg"], w["b_gates"], w["g_q_lora"], w["wuq"],
               w["g_kv_lora"], w["wk"], w["wv"], w["rrep"], w["gq3"], w["gk3"]]
    return pl.pallas_call(
        _pre_attention_kernel,
        grid=(n // tm,),
        in_specs=[row(D_MODEL), row(ROPE_HALF), row(ROPE_HALF)] + [_full(a.shape) for a in weights],
        out_specs=[row(hs), row(hs), row(N_HEADS * V_DIM), row(CONV_W), row(2 * D_MODEL)],
        out_shape=[jax.ShapeDtypeStruct((n, hs), BF16), jax.ShapeDtypeStruct((n, hs), BF16),
                   jax.ShapeDtypeStruct((n, N_HEADS * V_DIM), BF16), jax.ShapeDtypeStruct((n, CONV_W), F32),
                   jax.ShapeDtypeStruct((n, 2 * D_MODEL), BF16)],
        compiler_params=_params("parallel"),
        name="pre_attention",
    )(x2, cos, sin, *weights)


def _attention_kernel(q_ref, k_ref, v_ref, o_ref, kbd, vbd):
    t = ATTN_TILE
    hk = HEAD_SLOT
    seq = q_ref.shape[0]
    lane = lax.broadcasted_iota(jnp.int32, (hk, HEAD_SLOT), 1)
    zero = jnp.zeros((hk, HEAD_SLOT), BF16)
    for c in range(seq // hk):
        rows = slice(c * hk, (c + 1) * hk)
        top = slice(2 * c * hk, (2 * c + 1) * hk)
        bot = slice((2 * c + 1) * hk, (2 * c + 2) * hk)
        kbd[top, :HEAD_SLOT] = k_ref[rows, :HEAD_SLOT]
        kbd[top, HEAD_SLOT:] = zero
        kbd[bot, :HEAD_SLOT] = zero
        kbd[bot, HEAD_SLOT:] = k_ref[rows, HEAD_SLOT:]
        vp = v_ref[rows, :]
        vbd[top, :] = jnp.where(lane < V_DIM, vp, zero)
        vbd[bot, :] = jnp.where(lane >= V_DIM, vp, zero)

    qrow = lax.broadcasted_iota(jnp.int32, (t, hk), 0) // CHUNK
    kcol = lax.broadcasted_iota(jnp.int32, (t, hk), 1)
    diag_masks = [(kcol + h2 * hk) // CHUNK <= qrow for h2 in range(t // hk)]
    out_lane = lax.broadcasted_iota(jnp.int32, (t, HEAD_SLOT), 1)
    for i in range(seq // t):
        rows = slice(i * t, (i + 1) * t)
        n_tiles = 2 * (i + 1) * t // hk
        s = lax.dot_general(q_ref[rows, :], kbd[0:n_tiles * hk, :], (((1,), (1,)), ((), ())),
                            preferred_element_type=F32)
        tiles = [s[:, j * hk:(j + 1) * hk] for j in range(n_tiles)]
        first_diag = n_tiles - 2 * (t // hk)
        for j in range(first_diag, n_tiles):
            tiles[j] = jnp.where(diag_masks[(j - first_diag) // 2], tiles[j], NEG_INF)
        probs = [None] * n_tiles
        inv_l = []
        for h in range(2):
            mine = range(h, n_tiles, 2)
            m = jnp.max(functools.reduce(jnp.maximum, [tiles[j] for j in mine]), axis=-1, keepdims=True)
            for j in mine:
                probs[j] = jnp.exp2(tiles[j] - m)
            l = jnp.sum(functools.reduce(jnp.add, [probs[j] for j in mine]), axis=-1, keepdims=True)
            inv_l.append(1.0 / l)
        p = jnp.concatenate(probs, axis=1).astype(BF16)
        o = _dot(p, vbd[0:n_tiles * hk, :])
        o_ref[rows, :] = (o * jnp.where(out_lane < V_DIM, inv_l[0], inv_l[1])).astype(BF16)


def _attention(q, k, v, b, t):
    hs2 = 2 * HEAD_SLOT
    spec = pl.BlockSpec((t, hs2), lambda bi, p: (bi, p))
    vspec = pl.BlockSpec((t, 2 * V_DIM), lambda bi, p: (bi, p))
    return pl.pallas_call(
        _attention_kernel,
        grid=(b, N_HEADS // 2),
        in_specs=[spec, spec, vspec],
        out_specs=pl.BlockSpec((t, 2 * V_DIM), lambda bi, p: (bi, p)),
        out_shape=jax.ShapeDtypeStruct((b * t, N_HEADS * V_DIM), BF16),
        scratch_shapes=[pltpu.VMEM((2 * t, hs2), BF16), pltpu.VMEM((2 * t, 2 * V_DIM), BF16)],
        compiler_params=_params("parallel", "parallel"),
        name="attention",
    )(q, k, v)


def _merge_kernel(blocks_per_seq, attn_ref, u_ref, halo_ref, gate_ref, x_ref, wo_ref, wdw_ref, bdw_ref,
                  gln_ref, bln_ref, wco_ref, wout_ref, gffn_ref, wrh_ref, wrl_ref, br_ref,
                  x1_out, h2p_out, route_out, sh_ref):
    tm = x_ref.shape[0]
    first = (pl.program_id(0) % blocks_per_seq) == 0
    halo = halo_ref[...]
    sh_ref[0, 0:CONV_HALO, :] = jnp.where(first, jnp.zeros_like(halo), halo)
    sh_ref[0, CONV_HALO:, :] = u_ref[...]
    span = tm + CONV_HALO - SUBLANES
    for s in range(1, SUBLANES):
        sh_ref[s, 0:span, :] = sh_ref[0, s:s + span, :]

    base = CONV_HALO - (CONV_K - 1)
    half = D_MODEL // 2
    for sb in range(tm // SUB_ROWS):
        rows = slice(sb * SUB_ROWS, (sb + 1) * SUB_ROWS)
        parts = []
        for ck in range(SUB_ROWS // CONV_CHUNK):
            c0 = sb * SUB_ROWS + ck * CONV_CHUNK
            acc = jnp.broadcast_to(bdw_ref[...], (CONV_CHUNK, CONV_W))
            for j in range(CONV_K):
                s, a = (base + j) % SUBLANES, (base + j) // SUBLANES * SUBLANES
                acc = acc + sh_ref[s, c0 + a:c0 + a + CONV_CHUNK, :] * wdw_ref[j:j + 1, :]
            parts.append(acc)
        conv = jnp.concatenate(parts, axis=0)
        mu = jnp.mean(conv, axis=-1, keepdims=True)
        cen = conv - mu
        var = jnp.mean(cen * cen, axis=-1, keepdims=True)
        ln = cen * lax.rsqrt(var + EPS) * gln_ref[...] + bln_ref[...]
        ub = (ln * _sigmoid(ln)).astype(BF16)
        y_b = _dot(ub, wco_ref[...])
        y_a = _dot(attn_ref[rows, :], wo_ref[...])
        merged = gate_ref[rows, :D_MODEL].astype(F32) * y_a + gate_ref[rows, D_MODEL:].astype(F32) * y_b
        x1 = x_ref[rows, :] + _dot(merged.astype(BF16), wout_ref[...])
        x1_out[rows, :] = x1
        h2 = _rms(x1, gffn_ref[...])
        _store_tile_rows(h2p_out, sb * (SUB_ROWS // SUBLANES), _pack_words(h2[:, :half], h2[:, half:]))
        h2_hi = h2.astype(BF16)
        h2_lo = (h2 - h2_hi.astype(F32)).astype(BF16)
        logits = (_dot(h2_hi, wrh_ref[...]) + _dot(h2_lo, wrh_ref[...]) + _dot(h2_hi, wrl_ref[...])
                  + br_ref[...])
        route_out[rows, :] = _route(logits)


def _route(logits):
    lane_i = lax.broadcasted_iota(jnp.int32, logits.shape, 1)
    lane = lane_i.astype(F32)
    big = float(LANES)
    is_g = (lane_i >= N_EXPERTS) & (lane_i < N_EXPERTS + N_GROUPS)
    gl = jnp.where(is_g, logits, NEG_INF)
    gmax = jnp.max(gl, axis=-1, keepdims=True)
    gidx = jnp.min(jnp.where(gl == gmax, lane - N_EXPERTS, big), axis=-1, keepdims=True)
    g_p = 1.0 / jnp.sum(jnp.exp(gl - gmax), axis=-1, keepdims=True)
    lane_group = (lane_i // EXPERTS_PER_GROUP).astype(F32)
    valid = (lane_i < N_EXPERTS) & (lane_group == gidx)
    el = jnp.where(valid, logits, NEG_INF)
    ee = jnp.exp(el - jnp.max(el, axis=-1, keepdims=True))
    within = jnp.where(valid, ee / jnp.sum(ee, axis=-1, keepdims=True), -1.0)
    p1 = jnp.max(within, axis=-1, keepdims=True)
    i1 = jnp.min(jnp.where(within == p1, lane, big), axis=-1, keepdims=True)
    within2 = jnp.where(lane == i1, -1.0, within)
    p2 = jnp.max(within2, axis=-1, keepdims=True)
    i2 = jnp.min(jnp.where(within2 == p2, lane, big), axis=-1, keepdims=True)
    psum = p1 + p2
    w1 = g_p * (p1 / psum)
    w2 = g_p * (p2 / psum)
    route = jnp.where(lane_i == 0, i1, 0.0)
    route = jnp.where(lane_i == 1, i2, route)
    route = jnp.where(lane_i == 2, w1, route)
    return jnp.where(lane_i == 3, w2, route)


def _merge(attn, u, gates, x2, w, t, tm):
    n = x2.shape[0]
    bps = t // tm
    row = lambda width: pl.BlockSpec((tm, width), lambda i: (i, 0))
    hpb = tm // CONV_HALO
    halo = pl.BlockSpec((CONV_HALO, CONV_W), lambda i: (jnp.maximum(i * hpb - 1, 0), 0))
    weights = [w["wo"], w["w_dw"], w["b_dw"], w["g_conv_ln"], w["b_conv_ln"], w["wco"], w["wout"],
               w["g_ffn"], w["wr_hi"], w["wr_lo"], w["br"]]
    tile_rows = (tm // SUBLANES,) + PACKED_TILE
    return pl.pallas_call(
        functools.partial(_merge_kernel, bps),
        grid=(n // tm,),
        in_specs=[row(N_HEADS * V_DIM), row(CONV_W), halo, row(2 * D_MODEL), row(D_MODEL)]
        + [_full(a.shape) for a in weights],
        out_specs=[row(D_MODEL), pl.BlockSpec(tile_rows, lambda i: (i, 0, 0, 0)), row(LANES)],
        out_shape=[jax.ShapeDtypeStruct((n, D_MODEL), F32),
                   jax.ShapeDtypeStruct((n // SUBLANES,) + PACKED_TILE, jnp.uint32),
                   jax.ShapeDtypeStruct((n, LANES), F32)],
        scratch_shapes=[pltpu.VMEM((SUBLANES, tm + CONV_HALO, CONV_W), F32)],
        compiler_params=_params("parallel"),
        name="merge_router",
    )(attn, u, u, gates, x2, *weights)


def _rank_kernel(route_ref, pos_out, meta_out, r1_sc, r2_sc):
    tb = route_ref.shape[0]
    lane = lax.broadcasted_iota(jnp.int32, (RANK_CHUNK, LANES), 1).astype(F32)
    rr = lax.broadcasted_iota(jnp.int32, (RANK_CHUNK, RANK_CHUNK), 0)
    cc = lax.broadcasted_iota(jnp.int32, (RANK_CHUNK, RANK_CHUNK), 1)
    lower = jnp.where(rr > cc, 1.0, 0.0).astype(BF16)
    run = jnp.zeros((1, LANES), F32)
    for choice, sc in ((0, r1_sc), (1, r2_sc)):
        for ci in range(tb // RANK_CHUNK):
            rows = slice(ci * RANK_CHUNK, (ci + 1) * RANK_CHUNK)
            oh = jnp.where(lane == route_ref[rows, choice:choice + 1], 1.0, 0.0)
            sc[rows, :] = run + _dot(lower, oh.astype(BF16))
            run = run + jnp.sum(oh, axis=0, keepdims=True)
    cnt = run
    cnt8 = jnp.ceil(cnt * (1.0 / SUBLANES)) * SUBLANES
    ur = lax.broadcasted_iota(jnp.int32, (LANES, LANES), 0)
    uc = lax.broadcasted_iota(jnp.int32, (LANES, LANES), 1)
    upper = jnp.where(ur < uc, 1.0, 0.0)
    off = _dot(jnp.broadcast_to(cnt8, (SUBLANES, LANES)), upper, precision=lax.Precision.HIGHEST)[0:1, :]
    lane_t = lax.broadcasted_iota(jnp.int32, (tb, LANES), 1)
    lane_tf = lane_t.astype(F32)
    p1 = jnp.sum(jnp.where(lane_tf == route_ref[:, 0:1], off + r1_sc[...], 0.0), axis=-1, keepdims=True)
    p2 = jnp.sum(jnp.where(lane_tf == route_ref[:, 1:2], off + r2_sc[...], 0.0), axis=-1, keepdims=True)
    q1 = p1 + (SUBLANES * ROW_PIECES - SUBLANES) * jnp.floor(p1 * (1.0 / SUBLANES))
    q2 = p2 + (SUBLANES * ROW_PIECES - SUBLANES) * jnp.floor(p2 * (1.0 / SUBLANES))
    pos = jnp.where(lane_t == 0, q1, jnp.where(lane_t == 1, q2, 0.0))
    pos_out[...] = pos.astype(jnp.int32)
    row8 = lax.broadcasted_iota(jnp.int32, (SUBLANES, LANES), 0)
    meta = jnp.where(row8 == 0, jnp.broadcast_to(off, (SUBLANES, LANES)),
                     jnp.where(row8 == 1, jnp.broadcast_to(cnt, (SUBLANES, LANES)), 0.0))
    meta_out[...] = meta.astype(jnp.int32)


def _rank(route, b, t):
    return pl.pallas_call(
        _rank_kernel,
        grid=(b,),
        in_specs=[pl.BlockSpec((t, LANES), lambda i: (i, 0))],
        out_specs=[pl.BlockSpec((t, LANES), lambda i: (i, 0)), pl.BlockSpec((SUBLANES, LANES), lambda i: (i, 0))],
        out_shape=[jax.ShapeDtypeStruct((b * t, LANES), jnp.int32),
                   jax.ShapeDtypeStruct((b * SUBLANES, LANES), jnp.int32)],
        scratch_shapes=[pltpu.VMEM((t, LANES), F32), pltpu.VMEM((t, LANES), F32)],
        compiler_params=_params("parallel"),
        name="rank",
    )(route)


def _sorted_rows(t):
    return 2 * t + N_EXPERTS * SUBLANES + ROW_TILE


GROUP_ROWS = SUBLANES * ROW_PIECES


def _load_flat_tile(ref, base, rows):
    cols = []
    for c in range(ROW_PIECES):
        cols.append(jnp.concatenate(
            [ref[pl.ds(base + g * GROUP_ROWS + c * SUBLANES, SUBLANES), :] for g in range(rows // SUBLANES)], axis=0))
    return jnp.concatenate(cols, axis=1)


def _store_flat_tile(ref, base, words):
    for c in range(ROW_PIECES):
        for g in range(words.shape[0] // SUBLANES):
            ref[pl.ds(base + g * GROUP_ROWS + c * SUBLANES, SUBLANES), :] = (
                words[g * SUBLANES:(g + 1) * SUBLANES, c * LANES:(c + 1) * LANES])


def _moe_kernel(n_seq, off_ref, cnt_ref, q_disp_ref, q_comb_ref, h2p_ref, wgu_ref, wd_ref, x1_ref, route_ref,
                out_ref, xs0, xs1, ys0, ys1, g1, g2):
    o = pl.program_id(0)
    s = pl.program_id(1)
    ts = x1_ref.shape[0]
    half = D_MODEL // 2
    seq_e = jnp.clip(o - 1, 0, n_seq - 1)

    @pl.when((o == 0) & (s == 0))
    def _init():
        for ref in (xs0, xs1, ys0, ys1):
            ref[...] = jnp.zeros(ref.shape, ref.dtype)

    def expert_tile(xs_e, ys_e, k, base, i):
        start = pl.multiple_of(jnp.where(i >= 0, base + i * (ROW_TILE * ROW_PIECES),
                                         xs_e.shape[0] - ROW_TILE * ROW_PIECES), GROUP_ROWS)
        lo, hi = _unpack_words(_load_flat_tile(xs_e, start, ROW_TILE))
        xb = jnp.concatenate([lo.astype(BF16), hi.astype(BF16)], axis=1)
        au = _dot(xb, wgu_ref[k])
        a = au[:, :EXPERT_FF]
        mid = (a * _sigmoid(a) * au[:, EXPERT_FF:]).astype(BF16)
        y = _dot(mid, wd_ref[k])
        _store_flat_tile(ys_e, start, _pack_words(y[:, :half], y[:, half:]))

    def body(xs_d, xs_e, ys_e, ys_c):
        @pl.when(s == 0)
        def _clear():
            xs_d[...] = jnp.zeros(xs_d.shape, xs_d.dtype)

        bases, tiles = [], []
        for k in range(EXPERTS_PER_STEP):
            idx = seq_e * N_EXPERTS + s * EXPERTS_PER_STEP + k
            bases.append(off_ref[idx] * ROW_PIECES)
            tiles.append(lax.shift_right_logical(cnt_ref[idx] + (ROW_TILE - 1), ROW_TILE.bit_length() - 1))
        for k in range(EXPERTS_PER_STEP):
            expert_tile(xs_e, ys_e, k, bases[k], tiles[k] - 1)

        qd = s * (2 * ts)
        for g in range(ts // SUBLANES):
            for r in range(SUBLANES):
                row = h2p_ref[_row_ds(g * GROUP_ROWS + r), :]
                tok = g * SUBLANES + r
                xs_d[_row_ds(q_disp_ref[qd + 2 * tok]), :] = row
                xs_d[_row_ds(q_disp_ref[qd + 2 * tok + 1]), :] = row

        for g in range(ts // SUBLANES):
            for r in range(SUBLANES):
                tok = g * SUBLANES + r
                g1[_row_ds(g * GROUP_ROWS + r), :] = ys_c[_row_ds(q_comb_ref[qd + 2 * tok]), :]
                g2[_row_ds(g * GROUP_ROWS + r), :] = ys_c[_row_ds(q_comb_ref[qd + 2 * tok + 1]), :]
        w1 = route_ref[:, 2:3]
        w2 = route_ref[:, 3:4]
        for c in range(ROW_PIECES):
            tile_of = lambda ref: jnp.concatenate(
                [ref[g * GROUP_ROWS + c * SUBLANES:g * GROUP_ROWS + (c + 1) * SUBLANES, :]
                 for g in range(ts // SUBLANES)], axis=0)
            lo1, hi1 = _unpack_words(tile_of(g1))
            lo2, hi2 = _unpack_words(tile_of(g2))
            cols = slice(c * LANES, (c + 1) * LANES)
            cols_hi = slice(half + c * LANES, half + (c + 1) * LANES)
            out_ref[:, cols] = x1_ref[:, cols] + (w1 * lo1 + w2 * lo2)
            out_ref[:, cols_hi] = x1_ref[:, cols_hi] + (w1 * hi1 + w2 * hi2)

        def more(j, carry):
            for k in range(EXPERTS_PER_STEP):
                expert_tile(xs_e, ys_e, k, bases[k], tiles[k] - 1 - j)
            return carry

        lax.fori_loop(1, functools.reduce(jnp.maximum, tiles), more, 0)

    @pl.when(o % 2 == 0)
    def _even():
        body(xs0, xs1, ys1, ys0)

    @pl.when(o % 2 == 1)
    def _odd():
        body(xs1, xs0, ys0, ys1)


def _moe(off, cnt, q_flat, h2p_flat, x1, route, w, b, t):
    flat_rows = _sorted_rows(t) * ROW_PIECES
    n_steps = N_EXPERTS // EXPERTS_PER_STEP
    assert t % (n_steps * SUBLANES) == 0
    ts = t // n_steps
    seq_d = lambda o: jnp.minimum(o, b - 1)
    seq_c = lambda o: jnp.clip(o - 2, 0, b - 1)
    chunk = lambda o, s, *_: (jnp.where(o < 2, 0, seq_c(o) * n_steps + s), 0)
    wmap = lambda o, s, *_: (s, 0, 0)
    smem = lambda f: pl.BlockSpec((2 * t,), lambda o, s, *_: (f(o),), memory_space=pltpu.SMEM)
    grid_spec = pltpu.PrefetchScalarGridSpec(
        num_scalar_prefetch=2,
        grid=(b + 2, n_steps),
        in_specs=[smem(seq_d), smem(seq_c),
                  pl.BlockSpec((ts * ROW_PIECES, LANES), lambda o, s, *_: (seq_d(o) * n_steps + s, 0)),
                  pl.BlockSpec((EXPERTS_PER_STEP, D_MODEL, 2 * EXPERT_FF), wmap),
                  pl.BlockSpec((EXPERTS_PER_STEP, EXPERT_FF, D_MODEL), wmap),
                  pl.BlockSpec((ts, D_MODEL), chunk),
                  pl.BlockSpec((ts, LANES), chunk)],
        out_specs=pl.BlockSpec((ts, D_MODEL), chunk),
        scratch_shapes=[pltpu.VMEM((flat_rows, LANES), jnp.uint32)] * 4
        + [pltpu.VMEM((ts * ROW_PIECES, LANES), jnp.uint32)] * 2,
    )
    return pl.pallas_call(
        functools.partial(_moe_kernel, b),
        grid_spec=grid_spec,
        out_shape=jax.ShapeDtypeStruct((b * t, D_MODEL), F32),
        compiler_params=_params("arbitrary", "arbitrary"),
        name="moe",
    )(off, cnt, q_flat, q_flat, h2p_flat, w["wgu"], w["wd"], x1, route)


def _head_slots(wmat, width):
    k = wmat.shape[0]
    w3 = wmat.reshape(k, N_HEADS, width)
    return jnp.pad(w3, ((0, 0), (0, 0), (0, HEAD_SLOT - width))).reshape(k, N_HEADS * HEAD_SLOT)


def _gain3(g):
    gp = jnp.pad(g.astype(F32), (0, HEAD_SLOT - QK_DIM))
    return jnp.stack([gp, jnp.roll(gp, ROPE_HALF), jnp.roll(gp, -ROPE_HALF)])


def _prepare_weights(g_mix, w_in, g_q_lora, w_uq, g_kv_lora, w_ukv, g_qk_q, g_qk_k, w_o_mla, w_dw, b_dw,
                     g_conv_ln, b_conv_ln, w_conv_out, b_gates, w_out, g_ffn, w_group_router, b_group_router,
                     w_expert_router, b_expert_router, w_e_gate, w_e_up, w_e_down):
    row = lambda a: a.astype(F32).reshape(1, -1)
    s0, s1, s2, s3 = Q_LORA, Q_LORA + KV_LORA, Q_LORA + KV_LORA + ROPE, Q_LORA + KV_LORA + ROPE + 2 * CONV_W
    w = {}
    w["g_mix"] = row(g_mix)
    w["wa"] = jnp.concatenate([w_in[:, :s1], jnp.pad(w_in[:, s1:s2], ((0, 0), (0, LANES - ROPE)))], axis=1).astype(BF16)
    w["wc"] = w_in[:, s2:s3].astype(BF16)
    w["wg"] = w_in[:, s3:].astype(BF16)
    w["b_gates"] = row(b_gates)
    w["g_q_lora"] = row(g_q_lora)
    w["wuq"] = _head_slots(w_uq, QK_DIM).astype(BF16)
    w["g_kv_lora"] = row(g_kv_lora)
    kv3 = w_ukv.reshape(KV_LORA, N_HEADS, NOPE + V_DIM)
    w["wk"] = _head_slots(kv3[:, :, :NOPE].reshape(KV_LORA, N_HEADS * NOPE), NOPE).astype(BF16)
    w["wv"] = kv3[:, :, NOPE:].reshape(KV_LORA, N_HEADS * V_DIM).astype(BF16)
    src = jnp.arange(LANES)[:, None]
    dst = jnp.arange(N_HEADS * HEAD_SLOT)[None, :]
    w["rrep"] = ((src < ROPE) & ((dst % HEAD_SLOT) == (src + NOPE))).astype(BF16)
    w["gq3"] = _gain3(g_qk_q)
    w["gk3"] = _gain3(g_qk_k)
    w["wo"] = w_o_mla.astype(BF16)
    w["w_dw"] = jnp.pad(w_dw.astype(F32), ((0, 32 - CONV_K), (0, 0)))
    w["b_dw"] = row(b_dw)
    w["g_conv_ln"] = row(g_conv_ln)
    w["b_conv_ln"] = row(b_conv_ln)
    w["wco"] = w_conv_out.astype(BF16)
    w["wout"] = w_out.astype(BF16)
    w["g_ffn"] = row(g_ffn)
    pad_r = LANES - N_EXPERTS - N_GROUPS
    wr = jnp.concatenate([w_expert_router, w_group_router, jnp.zeros((D_MODEL, pad_r), F32)], axis=1).astype(F32)
    w["wr_hi"] = wr.astype(BF16)
    w["wr_lo"] = (wr - w["wr_hi"].astype(F32)).astype(BF16)
    w["br"] = jnp.concatenate([b_expert_router, b_group_router, jnp.zeros((pad_r,), F32)]).astype(F32).reshape(1, LANES)
    w["wgu"] = jnp.concatenate([w_e_gate.astype(BF16), w_e_up.astype(BF16)], axis=-1)
    w["wd"] = w_e_down.astype(BF16)
    return w


def kernel(x, positions, g_mix, w_in, g_q_lora, w_uq, g_kv_lora, w_ukv, g_qk_q, g_qk_k, w_o_mla, w_dw, b_dw,
           g_conv_ln, b_conv_ln, w_conv_out, b_gates, w_out, g_ffn, w_group_router, b_group_router,
           w_expert_router, b_expert_router, w_e_gate, w_e_up, w_e_down):
    b, t, d = x.shape
    assert d == D_MODEL and g_mix.shape[0] == 1
    assert t % ATTN_TILE == 0 and t % RANK_CHUNK == 0 and t % COMBINE_TILE == 0
    tm = min(TOKEN_BLOCK, t)
    assert t % tm == 0 and tm % CONV_HALO == 0
    n = b * t
    w = _prepare_weights(g_mix[0], w_in[0], g_q_lora[0], w_uq[0], g_kv_lora[0], w_ukv[0], g_qk_q[0], g_qk_k[0],
                         w_o_mla[0], w_dw[0], b_dw[0], g_conv_ln[0], b_conv_ln[0], w_conv_out[0], b_gates[0],
                         w_out[0], g_ffn[0], w_group_router[0], b_group_router[0], w_expert_router[0],
                         b_expert_router[0], w_e_gate[0], w_e_up[0], w_e_down[0])
    x2 = x.reshape(n, d)
    cos, sin = _rope_tables(positions)
    q, k, v, u, gates = _pre_attention(x2, cos, sin, w, tm)
    attn = _attention(q, k, v, b, t)
    x1, h2p, route = _merge(attn, u, gates, x2, w, t, tm)
    pos, meta = _rank(route, b, t)
    q_flat = pos[:, :2].reshape(n * 2)
    meta3 = meta.reshape(b, SUBLANES, LANES)
    off = meta3[:, 0, :N_EXPERTS].reshape(b * N_EXPERTS)
    cnt = meta3[:, 1, :N_EXPERTS].reshape(b * N_EXPERTS)
    out = _moe(off, cnt, q_flat, h2p.reshape(n * ROW_PIECES, LANES), x1, route, w, b, t)
    return out.reshape(b, t, d)
```

```python
import functools

import jax
import jax.numpy as jnp
import numpy as np
from jax import lax
from jax.experimental import pallas as pl
from jax.experimental.pallas import tpu as pltpu

D_MODEL = 1024
N_HEADS = 8
NOPE = 64
ROPE = 32
ROPE_HALF = ROPE // 2
QK_DIM = NOPE + ROPE
V_DIM = 64
Q_LORA = 256
KV_LORA = 128
CONV_W = 512
CONV_K = 31
N_GROUPS = 4
EXPERTS_PER_GROUP = 8
N_EXPERTS = N_GROUPS * EXPERTS_PER_GROUP
EXPERT_FF = 256
EPS = 1e-6
ROPE_THETA = 10000.0
CHUNK = 64

LANES = 128
SUBLANES = 8
HEAD_SLOT = LANES
VMEM_LIMIT_BYTES = 56 * 1024 * 1024

TOKEN_BLOCK = 512
ATTN_TILE = 256
CONV_HALO = 32
ROW_TILE = 256
EXPERTS_PER_STEP = 4
COMBINE_TILE = 256
RANK_CHUNK = 256
SUB_ROWS = 256
CONV_CHUNK = 32

F32 = jnp.float32
BF16 = jnp.bfloat16
NEG_INF = float("-inf")
LOG2E = 1.4426950408889634


def _dot(a, b, **kw):
    return jnp.dot(a, b, preferred_element_type=F32, **kw)


def _rms(x, g):
    return x * lax.rsqrt(jnp.mean(x * x, axis=-1, keepdims=True) + EPS) * g


def _sigmoid(x):
    return 1.0 / (1.0 + jnp.exp(-x))


def _pack_words(lo, hi):
    return pltpu.pack_elementwise([lo, hi], packed_dtype=BF16)


def _unpack_words(wd):
    lo = pltpu.unpack_elementwise(wd, index=0, packed_dtype=BF16, unpacked_dtype=F32)
    hi = pltpu.unpack_elementwise(wd, index=1, packed_dtype=BF16, unpacked_dtype=F32)
    return lo, hi


PACKED_TILE = (D_MODEL // 2 // LANES, SUBLANES, LANES)
ROW_PIECES = PACKED_TILE[0]


def _store_tile_rows(ref, g0, words, lead=()):
    groups = words.shape[0] // SUBLANES
    for c in range(ROW_PIECES):
        piece = words[:, c * LANES:(c + 1) * LANES].reshape(groups, SUBLANES, LANES)
        ref[lead + (pl.ds(g0, groups), c)] = piece


def _load_tile_rows(ref, g0, rows, lead=()):
    groups = rows // SUBLANES
    pieces = [ref[lead + (pl.ds(g0, groups), c)].reshape(rows, LANES) for c in range(ROW_PIECES)]
    return jnp.concatenate(pieces, axis=1)


def _row_ds(q):
    return pl.ds(q, ROW_PIECES, stride=SUBLANES)


def _params(*sem):
    return pltpu.CompilerParams(dimension_semantics=sem, vmem_limit_bytes=VMEM_LIMIT_BYTES)


def _full(shape):
    nd = len(shape)
    return pl.BlockSpec(shape, lambda *_: (0,) * nd)


def _rope_table_kernel(pos_ref, freq_ref, cos_ref, sin_ref):
    ang = pos_ref[...].astype(F32) * freq_ref[...]
    cos_ref[...] = jnp.cos(ang)
    sin_ref[...] = jnp.sin(ang)


def _rope_tables(positions):
    n = positions.size
    per_row = LANES // ROPE_HALF
    rows = n // per_row
    pos_rep = jnp.repeat(positions.reshape(rows, per_row), ROPE_HALF, axis=1)
    inv_freq = ROPE_THETA ** (-jnp.arange(ROPE_HALF, dtype=F32) / ROPE_HALF)
    freq = jnp.tile(inv_freq, per_row).reshape(1, LANES)
    rb = min(rows, 512)
    cos, sin = pl.pallas_call(
        _rope_table_kernel,
        grid=(rows // rb,),
        in_specs=[pl.BlockSpec((rb, LANES), lambda i: (i, 0)), _full((1, LANES))],
        out_specs=[pl.BlockSpec((rb, LANES), lambda i: (i, 0))] * 2,
        out_shape=[jax.ShapeDtypeStruct((rows, LANES), F32)] * 2,
        compiler_params=_params("parallel"),
        name="rope_tables",
    )(pos_rep, freq)
    return cos.reshape(n, ROPE_HALF), sin.reshape(n, ROPE_HALF)


def _pre_attention_kernel(x_ref, cos_ref, sin_ref, gmix_ref, wa_ref, wc_ref, wg_ref, bg_ref,
                          gql_ref, wuq_ref, gkv_ref, wk_ref, wv_ref, rrep_ref, gq_ref, gk_ref,
                          q_out, k_out, v_out, u_out, gate_out):
    tm = x_ref.shape[0]
    hb = _rms(x_ref[...], gmix_ref[...]).astype(BF16)

    a = _dot(hb, wa_ref[...])
    qdn = _rms(a[:, :Q_LORA], gql_ref[...]).astype(BF16)
    kvn = _rms(a[:, Q_LORA:Q_LORA + KV_LORA], gkv_ref[...]).astype(BF16)
    kr = a[:, Q_LORA + KV_LORA:].astype(BF16)
    qf = _dot(qdn, wuq_ref[...])
    kf = _dot(kvn, wk_ref[...]) + _dot(kr, rrep_ref[...])
    v_out[...] = _dot(kvn, wv_ref[...]).astype(BF16)

    cos = cos_ref[...]
    sin = sin_ref[...]
    z = lambda w: jnp.zeros((tm, w), F32)
    c_tab = jnp.concatenate([jnp.ones((tm, NOPE), F32), cos, cos, z(HEAD_SLOT - QK_DIM)], axis=-1)
    s_up = jnp.concatenate([z(NOPE + ROPE_HALF), sin, z(HEAD_SLOT - QK_DIM)], axis=-1)
    s_dn = jnp.concatenate([z(NOPE), -sin, z(HEAD_SLOT - NOPE - ROPE_HALF)], axis=-1)

    def norm_rope(f, g_ref, out, scale):
        ta = c_tab * g_ref[0:1, :]
        tu = s_up * g_ref[1:2, :]
        td = s_dn * g_ref[2:3, :]
        for h in range(N_HEADS):
            xs = f[:, h * HEAD_SLOT:(h + 1) * HEAD_SLOT]
            ss = jnp.sum(xs * xs, axis=-1, keepdims=True)
            s = lax.rsqrt(ss * (1.0 / QK_DIM) + EPS) * scale
            y = xs * ta + pltpu.roll(xs, ROPE_HALF, 1) * tu + pltpu.roll(xs, HEAD_SLOT - ROPE_HALF, 1) * td
            out[:, h * HEAD_SLOT:(h + 1) * HEAD_SLOT] = (y * s).astype(BF16)

    norm_rope(qf, gq_ref, q_out, LOG2E * QK_DIM ** -0.5)
    norm_rope(kf, gk_ref, k_out, 1.0)

    c = _dot(hb, wc_ref[...])
    u_out[...] = c[:, :CONV_W] * _sigmoid(c[:, CONV_W:])
    gate_out[...] = _sigmoid(_dot(hb, wg_ref[...]) + bg_ref[...]).astype(BF16)


def _pre_attention(x2, cos, sin, w, tm):
    n = x2.shape[0]
    row = lambda width: pl.BlockSpec((tm, width), lambda i: (i, 0))
    hs = N_HEADS * HEAD_SLOT
    weights = [w["g_mix"], w["wa"], w["wc"], w["wg"], w["b_gates"], w["g_q_lora"], w["wuq"],
               w["g_kv_lora"], w["wk"], w["wv"], w["rrep"], w["gq3"], w["gk3"]]
    return pl.pallas_call(
        _pre_attention_kernel,
        grid=(n // tm,),
        in_specs=[row(D_MODEL), row(ROPE_HALF), row(ROPE_HALF)] + [_full(a.shape) for a in weights],
        out_specs=[row(hs), row(hs), row(N_HEADS * V_DIM), row(CONV_W), row(2 * D_MODEL)],
        out_shape=[jax.ShapeDtypeStruct((n, hs), BF16), jax.ShapeDtypeStruct((n, hs), BF16),
                   jax.ShapeDtypeStruct((n, N_HEADS * V_DIM), BF16), jax.ShapeDtypeStruct((n, CONV_W), F32),
                   jax.ShapeDtypeStruct((n, 2 * D_MODEL), BF16)],
        compiler_params=_params("parallel"),
        name="pre_attention",
    )(x2, cos, sin, *weights)


def _causal_conv_slice(u_ref, wdw_ref, bdw_ref, conv_ref, sh_ref):
    seq = u_ref.shape[0]
    sh_ref[0, 0:CONV_HALO, :] = jnp.zeros((CONV_HALO, LANES), F32)
    sh_ref[0, CONV_HALO:, :] = u_ref[...]
    span = seq + CONV_HALO - SUBLANES
    for s in range(1, SUBLANES):
        sh_ref[s, 0:span, :] = sh_ref[0, s:s + span, :]
    base = CONV_HALO - (CONV_K - 1)
    done = []
    for ck in range(seq // CONV_CHUNK):
        c0 = ck * CONV_CHUNK
        acc = jnp.broadcast_to(bdw_ref[...], (CONV_CHUNK, LANES))
        for j in range(CONV_K):
            s, a = (base + j) % SUBLANES, (base + j) // SUBLANES * SUBLANES
            acc = acc + sh_ref[s, c0 + a:c0 + a + CONV_CHUNK, :] * wdw_ref[j:j + 1, :]
        conv_ref[c0:c0 + CONV_CHUNK, :] = acc
        done.append(acc[0:SUBLANES, :])
    return done


def _ordering_zero(values):
    bits = functools.reduce(jnp.bitwise_or, [lax.bitcast_convert_type(v, jnp.uint32) for v in values])
    half_width = jnp.uint32(16)
    return lax.bitcast_convert_type(
        lax.shift_right_logical(lax.shift_right_logical(bits, half_width), half_width), F32)


def _attention_kernel(q_ref, k_ref, v_ref, u_ref, wdw_ref, bdw_ref, o_ref, conv_ref, kbd, vbd, sh_ref):
    conv_done = _causal_conv_slice(u_ref, wdw_ref, bdw_ref, conv_ref, sh_ref)
    n_q = q_ref.shape[0] // ATTN_TILE
    work = [i + 1 for i in range(n_q)]
    share_end = [len(conv_done) * sum(work[:i + 1]) // sum(work) for i in range(n_q)]

    t = ATTN_TILE
    hk = HEAD_SLOT
    seq = q_ref.shape[0]
    lane = lax.broadcasted_iota(jnp.int32, (hk, HEAD_SLOT), 1)
    zero = jnp.zeros((hk, HEAD_SLOT), BF16)
    for c in range(seq // hk):
        rows = slice(c * hk, (c + 1) * hk)
        top = slice(2 * c * hk, (2 * c + 1) * hk)
        bot = slice((2 * c + 1) * hk, (2 * c + 2) * hk)
        kbd[top, :HEAD_SLOT] = k_ref[rows, :HEAD_SLOT]
        kbd[top, HEAD_SLOT:] = zero
        kbd[bot, :HEAD_SLOT] = zero
        kbd[bot, HEAD_SLOT:] = k_ref[rows, HEAD_SLOT:]
        vp = v_ref[rows, :]
        vbd[top, :] = jnp.where(lane < V_DIM, vp, zero)
        vbd[bot, :] = jnp.where(lane >= V_DIM, vp, zero)

    qrow = lax.broadcasted_iota(jnp.int32, (t, hk), 0) // CHUNK
    kcol = lax.broadcasted_iota(jnp.int32, (t, hk), 1)
    diag_masks = [(kcol + h2 * hk) // CHUNK <= qrow for h2 in range(t // hk)]
    out_lane = lax.broadcasted_iota(jnp.int32, (t, HEAD_SLOT), 1)
    for i in range(seq // t):
        rows = slice(i * t, (i + 1) * t)
        n_tiles = 2 * (i + 1) * t // hk
        s = lax.dot_general(q_ref[rows, :], kbd[0:n_tiles * hk, :], (((1,), (1,)), ((), ())),
                            preferred_element_type=F32)
        tiles = [s[:, j * hk:(j + 1) * hk] for j in range(n_tiles)]
        conv_deps = conv_done[(share_end[i - 1] if i else 0):share_end[i]]
        if conv_deps:
            tiles[0] = tiles[0] + jnp.tile(_ordering_zero(conv_deps), (t // SUBLANES, 1))
        first_diag = n_tiles - 2 * (t // hk)
        for j in range(first_diag, n_tiles):
            tiles[j] = jnp.where(diag_masks[(j - first_diag) // 2], tiles[j], NEG_INF)
        probs = [None] * n_tiles
        inv_l = []
        for h in range(2):
            mine = range(h, n_tiles, 2)
            m = jnp.max(functools.reduce(jnp.maximum, [tiles[j] for j in mine]), axis=-1, keepdims=True)
            for j in mine:
                probs[j] = jnp.exp2(tiles[j] - m)
            l = jnp.sum(functools.reduce(jnp.add, [probs[j] for j in mine]), axis=-1, keepdims=True)
            inv_l.append(1.0 / l)
        p = jnp.concatenate(probs, axis=1).astype(BF16)
        o = _dot(p, vbd[0:n_tiles * hk, :])
        o_ref[rows, :] = (o * jnp.where(out_lane < V_DIM, inv_l[0], inv_l[1])).astype(BF16)


def _attention(q, k, v, u, w, b, t):
    hs2 = 2 * HEAD_SLOT
    assert CONV_W == (N_HEADS // 2) * LANES
    spec = pl.BlockSpec((t, hs2), lambda bi, p: (bi, p))
    lane_tile = pl.BlockSpec((t, LANES), lambda bi, p: (bi, p))
    wslice = lambda rows: pl.BlockSpec((rows, LANES), lambda bi, p: (0, p))
    return pl.pallas_call(
        _attention_kernel,
        grid=(b, N_HEADS // 2),
        in_specs=[spec, spec, lane_tile, lane_tile, wslice(w["w_dw"].shape[0]), wslice(1)],
        out_specs=[lane_tile, lane_tile],
        out_shape=[jax.ShapeDtypeStruct((b * t, N_HEADS * V_DIM), BF16),
                   jax.ShapeDtypeStruct((b * t, CONV_W), F32)],
        scratch_shapes=[pltpu.VMEM((2 * t, hs2), BF16), pltpu.VMEM((2 * t, 2 * V_DIM), BF16),
                        pltpu.VMEM((SUBLANES, t + CONV_HALO, LANES), F32)],
        compiler_params=_params("parallel", "parallel"),
        name="attention",
    )(q, k, v, u, w["w_dw"], w["b_dw"])


def _merge_kernel(attn_ref, conv_ref, gate_ref, x_ref, wo_ref, gln_ref, bln_ref, wco_ref, wout_ref, gffn_ref,
                  wrh_ref, wrl_ref, br_ref, x1_out, h2p_out, route_out):
    tm = x_ref.shape[0]
    half = D_MODEL // 2
    for sb in range(tm // SUB_ROWS):
        rows = slice(sb * SUB_ROWS, (sb + 1) * SUB_ROWS)
        conv = conv_ref[rows, :]
        mu = jnp.mean(conv, axis=-1, keepdims=True)
        cen = conv - mu
        var = jnp.mean(cen * cen, axis=-1, keepdims=True)
        ln = cen * lax.rsqrt(var + EPS) * gln_ref[...] + bln_ref[...]
        ub = (ln * _sigmoid(ln)).astype(BF16)
        y_b = _dot(ub, wco_ref[...])
        y_a = _dot(attn_ref[rows, :], wo_ref[...])
        merged = gate_ref[rows, :D_MODEL].astype(F32) * y_a + gate_ref[rows, D_MODEL:].astype(F32) * y_b
        x1 = x_ref[rows, :] + _dot(merged.astype(BF16), wout_ref[...])
        x1_out[rows, :] = x1
        h2 = _rms(x1, gffn_ref[...])
        _store_tile_rows(h2p_out, sb * (SUB_ROWS // SUBLANES), _pack_words(h2[:, :half], h2[:, half:]))
        h2_hi = h2.astype(BF16)
        h2_lo = (h2 - h2_hi.astype(F32)).astype(BF16)
        logits = (_dot(h2_hi, wrh_ref[...]) + _dot(h2_lo, wrh_ref[...]) + _dot(h2_hi, wrl_ref[...])
                  + br_ref[...])
        route_out[rows, :] = _route(logits)


def _route(logits):
    lane_i = lax.broadcasted_iota(jnp.int32, logits.shape, 1)
    lane = lane_i.astype(F32)
    big = float(LANES)
    is_g = (lane_i >= N_EXPERTS) & (lane_i < N_EXPERTS + N_GROUPS)
    gl = jnp.where(is_g, logits, NEG_INF)
    gmax = jnp.max(gl, axis=-1, keepdims=True)
    gidx = jnp.min(jnp.where(gl == gmax, lane - N_EXPERTS, big), axis=-1, keepdims=True)
    g_p = 1.0 / jnp.sum(jnp.exp(gl - gmax), axis=-1, keepdims=True)
    lane_group = (lane_i // EXPERTS_PER_GROUP).astype(F32)
    valid = (lane_i < N_EXPERTS) & (lane_group == gidx)
    el = jnp.where(valid, logits, NEG_INF)
    ee = jnp.exp(el - jnp.max(el, axis=-1, keepdims=True))
    within = jnp.where(valid, ee / jnp.sum(ee, axis=-1, keepdims=True), -1.0)
    p1 = jnp.max(within, axis=-1, keepdims=True)
    i1 = jnp.min(jnp.where(within == p1, lane, big), axis=-1, keepdims=True)
    within2 = jnp.where(lane == i1, -1.0, within)
    p2 = jnp.max(within2, axis=-1, keepdims=True)
    i2 = jnp.min(jnp.where(within2 == p2, lane, big), axis=-1, keepdims=True)
    psum = p1 + p2
    w1 = g_p * (p1 / psum)
    w2 = g_p * (p2 / psum)
    route = jnp.where(lane_i == 0, i1, 0.0)
    route = jnp.where(lane_i == 1, i2, route)
    route = jnp.where(lane_i == 2, w1, route)
    return jnp.where(lane_i == 3, w2, route)


def _merge(attn, conv, gates, x2, w, tm):
    n = x2.shape[0]
    row = lambda width: pl.BlockSpec((tm, width), lambda i: (i, 0))
    weights = [w["wo"], w["g_conv_ln"], w["b_conv_ln"], w["wco"], w["wout"],
               w["g_ffn"], w["wr_hi"], w["wr_lo"], w["br"]]
    tile_rows = (tm // SUBLANES,) + PACKED_TILE
    return pl.pallas_call(
        _merge_kernel,
        grid=(n // tm,),
        in_specs=[row(N_HEADS * V_DIM), row(CONV_W), row(2 * D_MODEL), row(D_MODEL)]
        + [_full(a.shape) for a in weights],
        out_specs=[row(D_MODEL), pl.BlockSpec(tile_rows, lambda i: (i, 0, 0, 0)), row(LANES)],
        out_shape=[jax.ShapeDtypeStruct((n, D_MODEL), F32),
                   jax.ShapeDtypeStruct((n // SUBLANES,) + PACKED_TILE, jnp.uint32),
                   jax.ShapeDtypeStruct((n, LANES), F32)],
        compiler_params=_params("parallel"),
        name="merge_router",
    )(attn, conv, gates, x2, *weights)


def _rank_kernel(route_ref, pos_out, meta_out, r1_sc, r2_sc):
    tb = route_ref.shape[0]
    lane = lax.broadcasted_iota(jnp.int32, (RANK_CHUNK, LANES), 1).astype(F32)
    rr = lax.broadcasted_iota(jnp.int32, (RANK_CHUNK, RANK_CHUNK), 0)
    cc = lax.broadcasted_iota(jnp.int32, (RANK_CHUNK, RANK_CHUNK), 1)
    lower = jnp.where(rr > cc, 1.0, 0.0).astype(BF16)
    run = jnp.zeros((1, LANES), F32)
    for choice, sc in ((0, r1_sc), (1, r2_sc)):
        for ci in range(tb // RANK_CHUNK):
            rows = slice(ci * RANK_CHUNK, (ci + 1) * RANK_CHUNK)
            oh = jnp.where(lane == route_ref[rows, choice:choice + 1], 1.0, 0.0)
            sc[rows, :] = run + _dot(lower, oh.astype(BF16))
            run = run + jnp.sum(oh, axis=0, keepdims=True)
    cnt = run
    cnt8 = jnp.ceil(cnt * (1.0 / SUBLANES)) * SUBLANES
    ur = lax.broadcasted_iota(jnp.int32, (LANES, LANES), 0)
    uc = lax.broadcasted_iota(jnp.int32, (LANES, LANES), 1)
    upper = jnp.where(ur < uc, 1.0, 0.0)
    off = _dot(jnp.broadcast_to(cnt8, (SUBLANES, LANES)), upper, precision=lax.Precision.HIGHEST)[0:1, :]
    lane_t = lax.broadcasted_iota(jnp.int32, (tb, LANES), 1)
    lane_tf = lane_t.astype(F32)
    p1 = jnp.sum(jnp.where(lane_tf == route_ref[:, 0:1], off + r1_sc[...], 0.0), axis=-1, keepdims=True)
    p2 = jnp.sum(jnp.where(lane_tf == route_ref[:, 1:2], off + r2_sc[...], 0.0), axis=-1, keepdims=True)
    q1 = p1 + (SUBLANES * ROW_PIECES - SUBLANES) * jnp.floor(p1 * (1.0 / SUBLANES))
    q2 = p2 + (SUBLANES * ROW_PIECES - SUBLANES) * jnp.floor(p2 * (1.0 / SUBLANES))
    pos = jnp.where(lane_t == 0, q1, jnp.where(lane_t == 1, q2, 0.0))
    pos_out[...] = pos.astype(jnp.int32)
    row8 = lax.broadcasted_iota(jnp.int32, (SUBLANES, LANES), 0)
    meta = jnp.where(row8 == 0, jnp.broadcast_to(off, (SUBLANES, LANES)),
                     jnp.where(row8 == 1, jnp.broadcast_to(cnt, (SUBLANES, LANES)), 0.0))
    meta_out[...] = meta.astype(jnp.int32)


def _rank(route, b, t):
    return pl.pallas_call(
        _rank_kernel,
        grid=(b,),
        in_specs=[pl.BlockSpec((t, LANES), lambda i: (i, 0))],
        out_specs=[pl.BlockSpec((t, LANES), lambda i: (i, 0)), pl.BlockSpec((SUBLANES, LANES), lambda i: (i, 0))],
        out_shape=[jax.ShapeDtypeStruct((b * t, LANES), jnp.int32),
                   jax.ShapeDtypeStruct((b * SUBLANES, LANES), jnp.int32)],
        scratch_shapes=[pltpu.VMEM((t, LANES), F32), pltpu.VMEM((t, LANES), F32)],
        compiler_params=_params("parallel"),
        name="rank",
    )(route)


def _sorted_rows(t):
    return 2 * t + N_EXPERTS * SUBLANES + ROW_TILE


GROUP_ROWS = SUBLANES * ROW_PIECES


def _load_flat_tile(ref, base, rows):
    cols = []
    for c in range(ROW_PIECES):
        cols.append(jnp.concatenate(
            [ref[pl.ds(base + g * GROUP_ROWS + c * SUBLANES, SUBLANES), :] for g in range(rows // SUBLANES)], axis=0))
    return jnp.concatenate(cols, axis=1)


def _store_flat_tile(ref, base, words):
    for c in range(ROW_PIECES):
        for g in range(words.shape[0] // SUBLANES):
            ref[pl.ds(base + g * GROUP_ROWS + c * SUBLANES, SUBLANES), :] = (
                words[g * SUBLANES:(g + 1) * SUBLANES, c * LANES:(c + 1) * LANES])


def _moe_kernel(n_seq, off_ref, cnt_ref, q_disp_ref, q_comb_ref, h2p_ref, wgu_ref, wd_ref, x1_ref, route_ref,
                out_ref, xs0, xs1, ys0, ys1, g1, g2):
    o = pl.program_id(0)
    s = pl.program_id(1)
    ts = x1_ref.shape[0]
    half = D_MODEL // 2
    seq_e = jnp.clip(o - 1, 0, n_seq - 1)

    @pl.when((o == 0) & (s == 0))
    def _init():
        for ref in (xs0, xs1, ys0, ys1):
            ref[...] = jnp.zeros(ref.shape, ref.dtype)

    def expert_tile(xs_e, ys_e, k, base, i):
        start = pl.multiple_of(jnp.where(i >= 0, base + i * (ROW_TILE * ROW_PIECES),
                                         xs_e.shape[0] - ROW_TILE * ROW_PIECES), GROUP_ROWS)
        lo, hi = _unpack_words(_load_flat_tile(xs_e, start, ROW_TILE))
        xb = jnp.concatenate([lo.astype(BF16), hi.astype(BF16)], axis=1)
        au = _dot(xb, wgu_ref[k])
        a = au[:, :EXPERT_FF]
        mid = (a * _sigmoid(a) * au[:, EXPERT_FF:]).astype(BF16)
        y = _dot(mid, wd_ref[k])
        _store_flat_tile(ys_e, start, _pack_words(y[:, :half], y[:, half:]))

    def body(xs_d, xs_e, ys_e, ys_c):
        @pl.when(s == 0)
        def _clear():
            xs_d[...] = jnp.zeros(xs_d.shape, xs_d.dtype)

        bases, tiles = [], []
        for k in range(EXPERTS_PER_STEP):
            idx = seq_e * N_EXPERTS + s * EXPERTS_PER_STEP + k
            bases.append(off_ref[idx] * ROW_PIECES)
            tiles.append(lax.shift_right_logical(cnt_ref[idx] + (ROW_TILE - 1), ROW_TILE.bit_length() - 1))
        for k in range(EXPERTS_PER_STEP):
            expert_tile(xs_e, ys_e, k, bases[k], tiles[k] - 1)

        qd = s * (2 * ts)
        for g in range(ts // SUBLANES):
            for r in range(SUBLANES):
                row = h2p_ref[_row_ds(g * GROUP_ROWS + r), :]
                tok = g * SUBLANES + r
                xs_d[_row_ds(q_disp_ref[qd + 2 * tok]), :] = row
                xs_d[_row_ds(q_disp_ref[qd + 2 * tok + 1]), :] = row

        for g in range(ts // SUBLANES):
            for r in range(SUBLANES):
                tok = g * SUBLANES + r
                g1[_row_ds(g * GROUP_ROWS + r), :] = ys_c[_row_ds(q_comb_ref[qd + 2 * tok]), :]
                g2[_row_ds(g * GROUP_ROWS + r), :] = ys_c[_row_ds(q_comb_ref[qd + 2 * tok + 1]), :]
        w1 = route_ref[:, 2:3]
        w2 = route_ref[:, 3:4]
        for c in range(ROW_PIECES):
            tile_of = lambda ref: jnp.concatenate(
                [ref[g * GROUP_ROWS + c * SUBLANES:g * GROUP_ROWS + (c + 1) * SUBLANES, :]
                 for g in range(ts // SUBLANES)], axis=0)
            lo1, hi1 = _unpack_words(tile_of(g1))
            lo2, hi2 = _unpack_words(tile_of(g2))
            cols = slice(c * LANES, (c + 1) * LANES)
            cols_hi = slice(half + c * LANES, half + (c + 1) * LANES)
            out_ref[:, cols] = x1_ref[:, cols] + (w1 * lo1 + w2 * lo2)
            out_ref[:, cols_hi] = x1_ref[:, cols_hi] + (w1 * hi1 + w2 * hi2)

        def more(j, carry):
            for k in range(EXPERTS_PER_STEP):
                expert_tile(xs_e, ys_e, k, bases[k], tiles[k] - 1 - j)
            return carry

        lax.fori_loop(1, functools.reduce(jnp.maximum, tiles), more, 0)

    @pl.when(o % 2 == 0)
    def _even():
        body(xs0, xs1, ys1, ys0)

    @pl.when(o % 2 == 1)
    def _odd():
        body(xs1, xs0, ys0, ys1)


def _moe(off, cnt, q_flat, h2p_flat, x1, route, w, b, t):
    flat_rows = _sorted_rows(t) * ROW_PIECES
    n_steps = N_EXPERTS // EXPERTS_PER_STEP
    assert t % (n_steps * SUBLANES) == 0
    ts = t // n_steps
    seq_d = lambda o: jnp.minimum(o, b - 1)
    seq_c = lambda o: jnp.clip(o - 2, 0, b - 1)
    chunk = lambda o, s, *_: (jnp.where(o < 2, 0, seq_c(o) * n_steps + s), 0)
    wmap = lambda o, s, *_: (s, 0, 0)
    smem = lambda f: pl.BlockSpec((2 * t,), lambda o, s, *_: (f(o),), memory_space=pltpu.SMEM)
    grid_spec = pltpu.PrefetchScalarGridSpec(
        num_scalar_prefetch=2,
        grid=(b + 2, n_steps),
        in_specs=[smem(seq_d), smem(seq_c),
                  pl.BlockSpec((ts * ROW_PIECES, LANES), lambda o, s, *_: (seq_d(o) * n_steps + s, 0)),
                  pl.BlockSpec((EXPERTS_PER_STEP, D_MODEL, 2 * EXPERT_FF), wmap),
                  pl.BlockSpec((EXPERTS_PER_STEP, EXPERT_FF, D_MODEL), wmap),
                  pl.BlockSpec((ts, D_MODEL), chunk),
                  pl.BlockSpec((ts, LANES), chunk)],
        out_specs=pl.BlockSpec((ts, D_MODEL), chunk),
        scratch_shapes=[pltpu.VMEM((flat_rows, LANES), jnp.uint32)] * 4
        + [pltpu.VMEM((ts * ROW_PIECES, LANES), jnp.uint32)] * 2,
    )
    return pl.pallas_call(
        functools.partial(_moe_kernel, b),
        grid_spec=grid_spec,
        out_shape=jax.ShapeDtypeStruct((b * t, D_MODEL), F32),
        compiler_params=_params("arbitrary", "arbitrary"),
        name="moe",
    )(off, cnt, q_flat, q_flat, h2p_flat, w["wgu"], w["wd"], x1, route)


def _head_slots(wmat, width):
    k = wmat.shape[0]
    w3 = wmat.reshape(k, N_HEADS, width)
    return jnp.pad(w3, ((0, 0), (0, 0), (0, HEAD_SLOT - width))).reshape(k, N_HEADS * HEAD_SLOT)


def _gain3(g):
    gp = jnp.pad(g.astype(F32), (0, HEAD_SLOT - QK_DIM))
    return jnp.stack([gp, jnp.roll(gp, ROPE_HALF), jnp.roll(gp, -ROPE_HALF)])


def _prepare_weights(g_mix, w_in, g_q_lora, w_uq, g_kv_lora, w_ukv, g_qk_q, g_qk_k, w_o_mla, w_dw, b_dw,
                     g_conv_ln, b_conv_ln, w_conv_out, b_gates, w_out, g_ffn, w_group_router, b_group_router,
                     w_expert_router, b_expert_router, w_e_gate, w_e_up, w_e_down):
    row = lambda a: a.astype(F32).reshape(1, -1)
    s0, s1, s2, s3 = Q_LORA, Q_LORA + KV_LORA, Q_LORA + KV_LORA + ROPE, Q_LORA + KV_LORA + ROPE + 2 * CONV_W
    w = {}
    w["g_mix"] = row(g_mix)
    w["wa"] = jnp.concatenate([w_in[:, :s1], jnp.pad(w_in[:, s1:s2], ((0, 0), (0, LANES - ROPE)))], axis=1).astype(BF16)
    w["wc"] = w_in[:, s2:s3].astype(BF16)
    w["wg"] = w_in[:, s3:].astype(BF16)
    w["b_gates"] = row(b_gates)
    w["g_q_lora"] = row(g_q_lora)
    w["wuq"] = _head_slots(w_uq, QK_DIM).astype(BF16)
    w["g_kv_lora"] = row(g_kv_lora)
    kv3 = w_ukv.reshape(KV_LORA, N_HEADS, NOPE + V_DIM)
    w["wk"] = _head_slots(kv3[:, :, :NOPE].reshape(KV_LORA, N_HEADS * NOPE), NOPE).astype(BF16)
    w["wv"] = kv3[:, :, NOPE:].reshape(KV_LORA, N_HEADS * V_DIM).astype(BF16)
    src = jnp.arange(LANES)[:, None]
    dst = jnp.arange(N_HEADS * HEAD_SLOT)[None, :]
    w["rrep"] = ((src < ROPE) & ((dst % HEAD_SLOT) == (src + NOPE))).astype(BF16)
    w["gq3"] = _gain3(g_qk_q)
    w["gk3"] = _gain3(g_qk_k)
    w["wo"] = w_o_mla.astype(BF16)
    w["w_dw"] = jnp.pad(w_dw.astype(F32), ((0, 32 - CONV_K), (0, 0)))
    w["b_dw"] = row(b_dw)
    w["g_conv_ln"] = row(g_conv_ln)
    w["b_conv_ln"] = row(b_conv_ln)
    w["wco"] = w_conv_out.astype(BF16)
    w["wout"] = w_out.astype(BF16)
    w["g_ffn"] = row(g_ffn)
    pad_r = LANES - N_EXPERTS - N_GROUPS
    wr = jnp.concatenate([w_expert_router, w_group_router, jnp.zeros((D_MODEL, pad_r), F32)], axis=1).astype(F32)
    w["wr_hi"] = wr.astype(BF16)
    w["wr_lo"] = (wr - w["wr_hi"].astype(F32)).astype(BF16)
    w["br"] = jnp.concatenate([b_expert_router, b_group_router, jnp.zeros((pad_r,), F32)]).astype(F32).reshape(1, LANES)
    w["wgu"] = jnp.concatenate([w_e_gate.astype(BF16), w_e_up.astype(BF16)], axis=-1)
    w["wd"] = w_e_down.astype(BF16)
    return w


def kernel(x, positions, g_mix, w_in, g_q_lora, w_uq, g_kv_lora, w_ukv, g_qk_q, g_qk_k, w_o_mla, w_dw, b_dw,
           g_conv_ln, b_conv_ln, w_conv_out, b_gates, w_out, g_ffn, w_group_router, b_group_router,
           w_expert_router, b_expert_router, w_e_gate, w_e_up, w_e_down):
    b, t, d = x.shape
    assert d == D_MODEL and g_mix.shape[0] == 1
    assert t % ATTN_TILE == 0 and t % RANK_CHUNK == 0 and t % COMBINE_TILE == 0
    tm = min(TOKEN_BLOCK, t)
    assert t % tm == 0 and t % CONV_CHUNK == 0
    n = b * t
    w = _prepare_weights(g_mix[0], w_in[0], g_q_lora[0], w_uq[0], g_kv_lora[0], w_ukv[0], g_qk_q[0], g_qk_k[0],
                         w_o_mla[0], w_dw[0], b_dw[0], g_conv_ln[0], b_conv_ln[0], w_conv_out[0], b_gates[0],
                         w_out[0], g_ffn[0], w_group_router[0], b_group_router[0], w_expert_router[0],
                         b_expert_router[0], w_e_gate[0], w_e_up[0], w_e_down[0])
    x2 = x.reshape(n, d)
    cos, sin = _rope_tables(positions)
    q, k, v, u, gates = _pre_attention(x2, cos, sin, w, tm)
    attn, conv = _attention(q, k, v, u, w, b, t)
    x1, h2p, route = _merge(attn, conv, gates, x2, w, tm)
    pos, meta = _rank(route, b, t)
    q_flat = pos[:, :2].reshape(n * 2)
    meta3 = meta.reshape(b, SUBLANES, LANES)
    off = meta3[:, 0, :N_EXPERTS].reshape(b * N_EXPERTS)
    cnt = meta3[:, 1, :N_EXPERTS].reshape(b * N_EXPERTS)
    out = _moe(off, cnt, q_flat, h2p.reshape(n * ROW_PIECES, LANES), x1, route, w, b, t)
    return out.reshape(b, t, d)
```

```python
import functools

import jax
import jax.numpy as jnp
import numpy as np
from jax import lax
from jax.experimental import pallas as pl
from jax.experimental.pallas import tpu as pltpu

D_MODEL = 1024
N_HEADS = 8
NOPE = 64
ROPE = 32
ROPE_HALF = ROPE // 2
QK_DIM = NOPE + ROPE
V_DIM = 64
Q_LORA = 256
KV_LORA = 128
CONV_W = 512
CONV_K = 31
N_GROUPS = 4
EXPERTS_PER_GROUP = 8
N_EXPERTS = N_GROUPS * EXPERTS_PER_GROUP
EXPERT_FF = 256
EPS = 1e-6
ROPE_THETA = 10000.0
CHUNK = 64

LANES = 128
SUBLANES = 8
HEAD_SLOT = LANES
VMEM_LIMIT_BYTES = 56 * 1024 * 1024

TOKEN_BLOCK = 512
ATTN_TILE = 256
CONV_HALO = 32
ROW_TILE = 256
EXPERTS_PER_STEP = 4
COMBINE_TILE = 256
RANK_CHUNK = 256
SUB_ROWS = 256
CONV_CHUNK = 32

F32 = jnp.float32
BF16 = jnp.bfloat16
NEG_INF = float("-inf")
LOG2E = 1.4426950408889634


def _dot(a, b, **kw):
    return jnp.dot(a, b, preferred_element_type=F32, **kw)


def _rms(x, g):
    return x * lax.rsqrt(jnp.mean(x * x, axis=-1, keepdims=True) + EPS) * g


def _sigmoid(x):
    return 1.0 / (1.0 + jnp.exp(-x))


def _pack_words(lo, hi):
    return pltpu.pack_elementwise([lo, hi], packed_dtype=BF16)


def _unpack_words(wd):
    lo = pltpu.unpack_elementwise(wd, index=0, packed_dtype=BF16, unpacked_dtype=F32)
    hi = pltpu.unpack_elementwise(wd, index=1, packed_dtype=BF16, unpacked_dtype=F32)
    return lo, hi


PACKED_TILE = (D_MODEL // 2 // LANES, SUBLANES, LANES)
ROW_PIECES = PACKED_TILE[0]


def _store_tile_rows(ref, g0, words, lead=()):
    groups = words.shape[0] // SUBLANES
    for c in range(ROW_PIECES):
        piece = words[:, c * LANES:(c + 1) * LANES].reshape(groups, SUBLANES, LANES)
        ref[lead + (pl.ds(g0, groups), c)] = piece


def _load_tile_rows(ref, g0, rows, lead=()):
    groups = rows // SUBLANES
    pieces = [ref[lead + (pl.ds(g0, groups), c)].reshape(rows, LANES) for c in range(ROW_PIECES)]
    return jnp.concatenate(pieces, axis=1)


def _row_ds(q):
    return pl.ds(q, ROW_PIECES, stride=SUBLANES)


def _params(*sem):
    return pltpu.CompilerParams(dimension_semantics=sem, vmem_limit_bytes=VMEM_LIMIT_BYTES)


def _full(shape):
    nd = len(shape)
    return pl.BlockSpec(shape, lambda *_: (0,) * nd)


def _rope_table_kernel(pos_ref, freq_ref, cos_ref, sin_ref):
    ang = pos_ref[...].astype(F32) * freq_ref[...]
    cos_ref[...] = jnp.cos(ang)
    sin_ref[...] = jnp.sin(ang)


def _rope_tables(positions):
    n = positions.size
    per_row = LANES // ROPE_HALF
    rows = n // per_row
    pos_rep = jnp.repeat(positions.reshape(rows, per_row), ROPE_HALF, axis=1)
    inv_freq = ROPE_THETA ** (-jnp.arange(ROPE_HALF, dtype=F32) / ROPE_HALF)
    freq = jnp.tile(inv_freq, per_row).reshape(1, LANES)
    rb = min(rows, 512)
    cos, sin = pl.pallas_call(
        _rope_table_kernel,
        grid=(rows // rb,),
        in_specs=[pl.BlockSpec((rb, LANES), lambda i: (i, 0)), _full((1, LANES))],
        out_specs=[pl.BlockSpec((rb, LANES), lambda i: (i, 0))] * 2,
        out_shape=[jax.ShapeDtypeStruct((rows, LANES), F32)] * 2,
        compiler_params=_params("parallel"),
        name="rope_tables",
    )(pos_rep, freq)
    return cos.reshape(n, ROPE_HALF), sin.reshape(n, ROPE_HALF)


def _pre_attention_kernel(x_ref, cos_ref, sin_ref, gmix_ref, wa_ref, wc_ref, wg_ref, bg_ref,
                          gql_ref, wuq_ref, gkv_ref, wk_ref, wv_ref, rrep_ref, gq_ref, gk_ref,
                          q_out, k_out, v_out, u_out, gate_out):
    tm = x_ref.shape[0]
    hb = _rms(x_ref[...], gmix_ref[...]).astype(BF16)

    a = _dot(hb, wa_ref[...])
    qdn = _rms(a[:, :Q_LORA], gql_ref[...]).astype(BF16)
    kvn = _rms(a[:, Q_LORA:Q_LORA + KV_LORA], gkv_ref[...]).astype(BF16)
    kr = a[:, Q_LORA + KV_LORA:].astype(BF16)
    qf = _dot(qdn, wuq_ref[...])
    kf = _dot(kvn, wk_ref[...]) + _dot(kr, rrep_ref[...])
    v_out[...] = _dot(kvn, wv_ref[...]).astype(BF16)

    cos = cos_ref[...]
    sin = sin_ref[...]
    z = lambda w: jnp.zeros((tm, w), F32)
    c_tab = jnp.concatenate([jnp.ones((tm, NOPE), F32), cos, cos, z(HEAD_SLOT - QK_DIM)], axis=-1)
    s_up = jnp.concatenate([z(NOPE + ROPE_HALF), sin, z(HEAD_SLOT - QK_DIM)], axis=-1)
    s_dn = jnp.concatenate([z(NOPE), -sin, z(HEAD_SLOT - NOPE - ROPE_HALF)], axis=-1)

    def norm_rope(f, g_ref, out, scale):
        ta = c_tab * g_ref[0:1, :]
        tu = s_up * g_ref[1:2, :]
        td = s_dn * g_ref[2:3, :]
        for h in range(N_HEADS):
            xs = f[:, h * HEAD_SLOT:(h + 1) * HEAD_SLOT]
            ss = jnp.sum(xs * xs, axis=-1, keepdims=True)
            s = lax.rsqrt(ss * (1.0 / QK_DIM) + EPS) * scale
            y = xs * ta + pltpu.roll(xs, ROPE_HALF, 1) * tu + pltpu.roll(xs, HEAD_SLOT - ROPE_HALF, 1) * td
            out[:, h * HEAD_SLOT:(h + 1) * HEAD_SLOT] = (y * s).astype(BF16)

    norm_rope(qf, gq_ref, q_out, LOG2E * QK_DIM ** -0.5)
    norm_rope(kf, gk_ref, k_out, 1.0)

    c = _dot(hb, wc_ref[...])
    u_out[...] = c[:, :CONV_W] * _sigmoid(c[:, CONV_W:])
    gate_out[...] = _sigmoid(_dot(hb, wg_ref[...]) + bg_ref[...]).astype(BF16)


def _pre_attention(x2, cos, sin, w, tm):
    n = x2.shape[0]
    row = lambda width: pl.BlockSpec((tm, width), lambda i: (i, 0))
    hs = N_HEADS * HEAD_SLOT
    weights = [w["g_mix"], w["wa"], w["wc"], w["wg"], w["b_gates"], w["g_q_lora"], w["wuq"],
               w["g_kv_lora"], w["wk"], w["wv"], w["rrep"], w["gq3"], w["gk3"]]
    return pl.pallas_call(
        _pre_attention_kernel,
        grid=(n // tm,),
        in_specs=[row(D_MODEL), row(ROPE_HALF), row(ROPE_HALF)] + [_full(a.shape) for a in weights],
        out_specs=[row(hs), row(hs), row(N_HEADS * V_DIM), row(CONV_W), row(2 * D_MODEL)],
        out_shape=[jax.ShapeDtypeStruct((n, hs), BF16), jax.ShapeDtypeStruct((n, hs), BF16),
                   jax.ShapeDtypeStruct((n, N_HEADS * V_DIM), BF16), jax.ShapeDtypeStruct((n, CONV_W), F32),
                   jax.ShapeDtypeStruct((n, 2 * D_MODEL), BF16)],
        compiler_params=_params("parallel"),
        name="pre_attention",
    )(x2, cos, sin, *weights)


def _causal_conv_slice(u_ref, wdw_ref, bdw_ref, conv_ref, sh_ref):
    seq = u_ref.shape[0]
    sh_ref[0, 0:CONV_HALO, :] = jnp.zeros((CONV_HALO, LANES), F32)
    sh_ref[0, CONV_HALO:, :] = u_ref[...]
    span = seq + CONV_HALO - SUBLANES
    for s in range(1, SUBLANES):
        sh_ref[s, 0:span, :] = sh_ref[0, s:s + span, :]
    base = CONV_HALO - (CONV_K - 1)
    done = []
    for ck in range(seq // CONV_CHUNK):
        c0 = ck * CONV_CHUNK
        acc = jnp.broadcast_to(bdw_ref[...], (CONV_CHUNK, LANES))
        for j in range(CONV_K):
            s, a = (base + j) % SUBLANES, (base + j) // SUBLANES * SUBLANES
            acc = acc + sh_ref[s, c0 + a:c0 + a + CONV_CHUNK, :] * wdw_ref[j:j + 1, :]
        conv_ref[c0:c0 + CONV_CHUNK, :] = acc
        done.append(acc[0:SUBLANES, :])
    return done


def _ordering_zero(values):
    bits = functools.reduce(jnp.bitwise_or, [lax.bitcast_convert_type(v, jnp.uint32) for v in values])
    half_width = jnp.uint32(16)
    return lax.bitcast_convert_type(
        lax.shift_right_logical(lax.shift_right_logical(bits, half_width), half_width), F32)


def _attention_kernel(q_ref, k_ref, v_ref, u_ref, wdw_ref, bdw_ref, o_ref, conv_ref, kbd, vbd, sh_ref):
    conv_done = _causal_conv_slice(u_ref, wdw_ref, bdw_ref, conv_ref, sh_ref)
    n_q = q_ref.shape[0] // ATTN_TILE
    work = [i + 1 for i in range(n_q)]
    share_end = [len(conv_done) * sum(work[:i + 1]) // sum(work) for i in range(n_q)]

    t = ATTN_TILE
    hk = HEAD_SLOT
    seq = q_ref.shape[0]
    lane = lax.broadcasted_iota(jnp.int32, (hk, HEAD_SLOT), 1)
    zero = jnp.zeros((hk, HEAD_SLOT), BF16)
    for c in range(seq // hk):
        rows = slice(c * hk, (c + 1) * hk)
        top = slice(2 * c * hk, (2 * c + 1) * hk)
        bot = slice((2 * c + 1) * hk, (2 * c + 2) * hk)
        kbd[top, :HEAD_SLOT] = k_ref[rows, :HEAD_SLOT]
        kbd[top, HEAD_SLOT:] = zero
        kbd[bot, :HEAD_SLOT] = zero
        kbd[bot, HEAD_SLOT:] = k_ref[rows, HEAD_SLOT:]
        vp = v_ref[rows, :]
        vbd[top, :] = jnp.where(lane < V_DIM, vp, zero)
        vbd[bot, :] = jnp.where(lane >= V_DIM, vp, zero)

    qrow = lax.broadcasted_iota(jnp.int32, (t, hk), 0) // CHUNK
    kcol = lax.broadcasted_iota(jnp.int32, (t, hk), 1)
    diag_masks = [(kcol + h2 * hk) // CHUNK <= qrow for h2 in range(t // hk)]
    out_lane = lax.broadcasted_iota(jnp.int32, (t, HEAD_SLOT), 1)
    for i in range(seq // t):
        rows = slice(i * t, (i + 1) * t)
        n_tiles = 2 * (i + 1) * t // hk
        s = lax.dot_general(q_ref[rows, :], kbd[0:n_tiles * hk, :], (((1,), (1,)), ((), ())),
                            preferred_element_type=F32)
        tiles = [s[:, j * hk:(j + 1) * hk] for j in range(n_tiles)]
        conv_deps = conv_done[(share_end[i - 1] if i else 0):share_end[i]]
        if conv_deps:
            tiles[0] = tiles[0] + jnp.tile(_ordering_zero(conv_deps), (t // SUBLANES, 1))
        first_diag = n_tiles - 2 * (t // hk)
        for j in range(first_diag, n_tiles):
            tiles[j] = jnp.where(diag_masks[(j - first_diag) // 2], tiles[j], NEG_INF)
        probs = [None] * n_tiles
        inv_l = []
        for h in range(2):
            mine = range(h, n_tiles, 2)
            m = jnp.max(functools.reduce(jnp.maximum, [tiles[j] for j in mine]), axis=-1, keepdims=True)
            for j in mine:
                probs[j] = jnp.exp2(tiles[j] - m)
            l = jnp.sum(functools.reduce(jnp.add, [probs[j] for j in mine]), axis=-1, keepdims=True)
            inv_l.append(1.0 / l)
        p = jnp.concatenate(probs, axis=1).astype(BF16)
        o = _dot(p, vbd[0:n_tiles * hk, :])
        o_ref[rows, :] = (o * jnp.where(out_lane < V_DIM, inv_l[0], inv_l[1])).astype(BF16)


def _attention(q, k, v, u, w, b, t):
    hs2 = 2 * HEAD_SLOT
    assert CONV_W == (N_HEADS // 2) * LANES
    spec = pl.BlockSpec((t, hs2), lambda bi, p: (bi, p))
    lane_tile = pl.BlockSpec((t, LANES), lambda bi, p: (bi, p))
    wslice = lambda rows: pl.BlockSpec((rows, LANES), lambda bi, p: (0, p))
    return pl.pallas_call(
        _attention_kernel,
        grid=(b, N_HEADS // 2),
        in_specs=[spec, spec, lane_tile, lane_tile, wslice(w["w_dw"].shape[0]), wslice(1)],
        out_specs=[lane_tile, lane_tile],
        out_shape=[jax.ShapeDtypeStruct((b * t, N_HEADS * V_DIM), BF16),
                   jax.ShapeDtypeStruct((b * t, CONV_W), F32)],
        scratch_shapes=[pltpu.VMEM((2 * t, hs2), BF16), pltpu.VMEM((2 * t, 2 * V_DIM), BF16),
                        pltpu.VMEM((SUBLANES, t + CONV_HALO, LANES), F32)],
        compiler_params=_params("parallel", "parallel"),
        name="attention",
    )(q, k, v, u, w["w_dw"], w["b_dw"])


def _merge_kernel(attn_ref, conv_ref, gate_ref, x_ref, wo_ref, gln_ref, bln_ref, wco_ref, wout_ref, gffn_ref,
                  wrh_ref, wrl_ref, br_ref, x1_out, h2p_out, route_out):
    tm = x_ref.shape[0]
    half = D_MODEL // 2
    for sb in range(tm // SUB_ROWS):
        rows = slice(sb * SUB_ROWS, (sb + 1) * SUB_ROWS)
        conv = conv_ref[rows, :]
        mu = jnp.mean(conv, axis=-1, keepdims=True)
        cen = conv - mu
        var = jnp.mean(cen * cen, axis=-1, keepdims=True)
        ln = cen * lax.rsqrt(var + EPS) * gln_ref[...] + bln_ref[...]
        ub = (ln * _sigmoid(ln)).astype(BF16)
        y_b = _dot(ub, wco_ref[...])
        y_a = _dot(attn_ref[rows, :], wo_ref[...])
        merged = gate_ref[rows, :D_MODEL].astype(F32) * y_a + gate_ref[rows, D_MODEL:].astype(F32) * y_b
        x1 = x_ref[rows, :] + _dot(merged.astype(BF16), wout_ref[...])
        x1_out[rows, :] = x1
        h2 = _rms(x1, gffn_ref[...])
        _store_tile_rows(h2p_out, sb * (SUB_ROWS // SUBLANES), _pack_words(h2[:, :half], h2[:, half:]))
        h2_hi = h2.astype(BF16)
        h2_lo = (h2 - h2_hi.astype(F32)).astype(BF16)
        logits = (_dot(h2_hi, wrh_ref[...]) + _dot(h2_lo, wrh_ref[...]) + _dot(h2_hi, wrl_ref[...])
                  + br_ref[...])
        route_out[rows, :] = _route(logits)


def _route(logits):
    lane_i = lax.broadcasted_iota(jnp.int32, logits.shape, 1)
    lane = lane_i.astype(F32)
    big = float(LANES)
    is_g = (lane_i >= N_EXPERTS) & (lane_i < N_EXPERTS + N_GROUPS)
    gl = jnp.where(is_g, logits, NEG_INF)
    gmax = jnp.max(gl, axis=-1, keepdims=True)
    gidx = jnp.min(jnp.where(gl == gmax, lane - N_EXPERTS, big), axis=-1, keepdims=True)
    g_p = 1.0 / jnp.sum(jnp.exp(gl - gmax), axis=-1, keepdims=True)
    lane_group = (lane_i // EXPERTS_PER_GROUP).astype(F32)
    valid = (lane_i < N_EXPERTS) & (lane_group == gidx)
    el = jnp.where(valid, logits, NEG_INF)
    ee = jnp.exp(el - jnp.max(el, axis=-1, keepdims=True))
    within = jnp.where(valid, ee / jnp.sum(ee, axis=-1, keepdims=True), -1.0)
    p1 = jnp.max(within, axis=-1, keepdims=True)
    i1 = jnp.min(jnp.where(within == p1, lane, big), axis=-1, keepdims=True)
    within2 = jnp.where(lane == i1, -1.0, within)
    p2 = jnp.max(within2, axis=-1, keepdims=True)
    i2 = jnp.min(jnp.where(within2 == p2, lane, big), axis=-1, keepdims=True)
    psum = p1 + p2
    w1 = g_p * (p1 / psum)
    w2 = g_p * (p2 / psum)
    route = jnp.where(lane_i == 0, i1, 0.0)
    route = jnp.where(lane_i == 1, i2, route)
    route = jnp.where(lane_i == 2, w1, route)
    return jnp.where(lane_i == 3, w2, route)


def _merge(attn, conv, gates, x2, w, tm):
    n = x2.shape[0]
    row = lambda width: pl.BlockSpec((tm, width), lambda i: (i, 0))
    weights = [w["wo"], w["g_conv_ln"], w["b_conv_ln"], w["wco"], w["wout"],
               w["g_ffn"], w["wr_hi"], w["wr_lo"], w["br"]]
    tile_rows = (tm // SUBLANES,) + PACKED_TILE
    return pl.pallas_call(
        _merge_kernel,
        grid=(n // tm,),
        in_specs=[row(N_HEADS * V_DIM), row(CONV_W), row(2 * D_MODEL), row(D_MODEL)]
        + [_full(a.shape) for a in weights],
        out_specs=[row(D_MODEL), pl.BlockSpec(tile_rows, lambda i: (i, 0, 0, 0)), row(LANES)],
        out_shape=[jax.ShapeDtypeStruct((n, D_MODEL), F32),
                   jax.ShapeDtypeStruct((n // SUBLANES,) + PACKED_TILE, jnp.uint32),
                   jax.ShapeDtypeStruct((n, LANES), F32)],
        compiler_params=_params("parallel"),
        name="merge_router",
    )(attn, conv, gates, x2, *weights)


def _rank_kernel(route_ref, pos_out, meta_out, r1_sc, r2_sc):
    tb = route_ref.shape[0]
    lane = lax.broadcasted_iota(jnp.int32, (RANK_CHUNK, LANES), 1).astype(F32)
    rr = lax.broadcasted_iota(jnp.int32, (RANK_CHUNK, RANK_CHUNK), 0)
    cc = lax.broadcasted_iota(jnp.int32, (RANK_CHUNK, RANK_CHUNK), 1)
    lower = jnp.where(rr > cc, 1.0, 0.0).astype(BF16)
    run = jnp.zeros((1, LANES), F32)
    for choice, sc in ((0, r1_sc), (1, r2_sc)):
        for ci in range(tb // RANK_CHUNK):
            rows = slice(ci * RANK_CHUNK, (ci + 1) * RANK_CHUNK)
            oh = jnp.where(lane == route_ref[rows, choice:choice + 1], 1.0, 0.0)
            sc[rows, :] = run + _dot(lower, oh.astype(BF16))
            run = run + jnp.sum(oh, axis=0, keepdims=True)
    cnt = run
    cnt8 = jnp.ceil(cnt * (1.0 / SUBLANES)) * SUBLANES
    ur = lax.broadcasted_iota(jnp.int32, (LANES, LANES), 0)
    uc = lax.broadcasted_iota(jnp.int32, (LANES, LANES), 1)
    upper = jnp.where(ur < uc, 1.0, 0.0)
    off = _dot(jnp.broadcast_to(cnt8, (SUBLANES, LANES)), upper, precision=lax.Precision.HIGHEST)[0:1, :]
    lane_t = lax.broadcasted_iota(jnp.int32, (tb, LANES), 1)
    lane_tf = lane_t.astype(F32)
    p1 = jnp.sum(jnp.where(lane_tf == route_ref[:, 0:1], off + r1_sc[...], 0.0), axis=-1, keepdims=True)
    p2 = jnp.sum(jnp.where(lane_tf == route_ref[:, 1:2], off + r2_sc[...], 0.0), axis=-1, keepdims=True)
    q1 = p1 + (SUBLANES * ROW_PIECES - SUBLANES) * jnp.floor(p1 * (1.0 / SUBLANES))
    q2 = p2 + (SUBLANES * ROW_PIECES - SUBLANES) * jnp.floor(p2 * (1.0 / SUBLANES))
    pos = jnp.where(lane_t == 0, q1, jnp.where(lane_t == 1, q2, 0.0))
    pos_out[...] = jnp.transpose(pos)[0:SUBLANES, :].astype(jnp.int32)
    row8 = lax.broadcasted_iota(jnp.int32, (SUBLANES, LANES), 0)
    meta = jnp.where(row8 == 0, jnp.broadcast_to(off, (SUBLANES, LANES)),
                     jnp.where(row8 == 1, jnp.broadcast_to(cnt, (SUBLANES, LANES)), 0.0))
    meta_out[...] = meta.astype(jnp.int32)


def _rank(route, b, t):
    return pl.pallas_call(
        _rank_kernel,
        grid=(b,),
        in_specs=[pl.BlockSpec((t, LANES), lambda i: (i, 0))],
        out_specs=[pl.BlockSpec((SUBLANES, t), lambda i: (i, 0)), pl.BlockSpec((SUBLANES, LANES), lambda i: (i, 0))],
        out_shape=[jax.ShapeDtypeStruct((b * SUBLANES, t), jnp.int32),
                   jax.ShapeDtypeStruct((b * SUBLANES, LANES), jnp.int32)],
        scratch_shapes=[pltpu.VMEM((t, LANES), F32), pltpu.VMEM((t, LANES), F32)],
        compiler_params=_params("parallel"),
        name="rank",
    )(route)


def _sorted_rows(t):
    return 2 * t + N_EXPERTS * SUBLANES + ROW_TILE


GROUP_ROWS = SUBLANES * ROW_PIECES


def _load_flat_tile(ref, base, rows):
    cols = []
    for c in range(ROW_PIECES):
        cols.append(jnp.concatenate(
            [ref[pl.ds(base + g * GROUP_ROWS + c * SUBLANES, SUBLANES), :] for g in range(rows // SUBLANES)], axis=0))
    return jnp.concatenate(cols, axis=1)


def _store_flat_tile(ref, base, words):
    for c in range(ROW_PIECES):
        for g in range(words.shape[0] // SUBLANES):
            ref[pl.ds(base + g * GROUP_ROWS + c * SUBLANES, SUBLANES), :] = (
                words[g * SUBLANES:(g + 1) * SUBLANES, c * LANES:(c + 1) * LANES])


def _moe_kernel(n_seq, off_ref, cnt_ref, qd1_ref, qd2_ref, qc1_ref, qc2_ref, h2p_ref, wgu_ref, wd_ref, x1_ref,
                route_ref, out_ref, xs0, xs1, ys0, ys1, g1, g2):
    o = pl.program_id(0)
    s = pl.program_id(1)
    ts = x1_ref.shape[0]
    half = D_MODEL // 2
    seq_e = jnp.clip(o - 1, 0, n_seq - 1)

    @pl.when((o == 0) & (s == 0))
    def _init():
        for ref in (xs0, xs1, ys0, ys1):
            ref[...] = jnp.zeros(ref.shape, ref.dtype)

    def expert_tile(xs_e, ys_e, k, base, i):
        start = pl.multiple_of(jnp.where(i >= 0, base + i * (ROW_TILE * ROW_PIECES),
                                         xs_e.shape[0] - ROW_TILE * ROW_PIECES), GROUP_ROWS)
        lo, hi = _unpack_words(_load_flat_tile(xs_e, start, ROW_TILE))
        xb = jnp.concatenate([lo.astype(BF16), hi.astype(BF16)], axis=1)
        au = _dot(xb, wgu_ref[k])
        a = au[:, :EXPERT_FF]
        mid = (a * _sigmoid(a) * au[:, EXPERT_FF:]).astype(BF16)
        y = _dot(mid, wd_ref[k])
        _store_flat_tile(ys_e, start, _pack_words(y[:, :half], y[:, half:]))

    def body(xs_d, xs_e, ys_e, ys_c, with_experts):
        @pl.when(s == 0)
        def _clear():
            xs_d[...] = jnp.zeros(xs_d.shape, xs_d.dtype)

        bases, tiles = [], []
        if with_experts:
            for k in range(EXPERTS_PER_STEP):
                idx = seq_e * N_EXPERTS + s * EXPERTS_PER_STEP + k
                bases.append(off_ref[idx] * ROW_PIECES)
                tiles.append(lax.shift_right_logical(cnt_ref[idx] + (ROW_TILE - 1), ROW_TILE.bit_length() - 1))
            for k in range(EXPERTS_PER_STEP):
                expert_tile(xs_e, ys_e, k, bases[k], tiles[k] - 1)

        t0 = s * ts
        for g in range(ts // SUBLANES):
            for r in range(SUBLANES):
                row = h2p_ref[_row_ds(g * GROUP_ROWS + r), :]
                tok = t0 + g * SUBLANES + r
                xs_d[_row_ds(qd1_ref[tok]), :] = row
                xs_d[_row_ds(qd2_ref[tok]), :] = row

        for g in range(ts // SUBLANES):
            for r in range(SUBLANES):
                tok = t0 + g * SUBLANES + r
                g1[_row_ds(g * GROUP_ROWS + r), :] = ys_c[_row_ds(qc1_ref[tok]), :]
                g2[_row_ds(g * GROUP_ROWS + r), :] = ys_c[_row_ds(qc2_ref[tok]), :]
        w1 = route_ref[:, 2:3]
        w2 = route_ref[:, 3:4]
        for c in range(ROW_PIECES):
            tile_of = lambda ref: jnp.concatenate(
                [ref[g * GROUP_ROWS + c * SUBLANES:g * GROUP_ROWS + (c + 1) * SUBLANES, :]
                 for g in range(ts // SUBLANES)], axis=0)
            lo1, hi1 = _unpack_words(tile_of(g1))
            lo2, hi2 = _unpack_words(tile_of(g2))
            cols = slice(c * LANES, (c + 1) * LANES)
            cols_hi = slice(half + c * LANES, half + (c + 1) * LANES)
            out_ref[:, cols] = x1_ref[:, cols] + (w1 * lo1 + w2 * lo2)
            out_ref[:, cols_hi] = x1_ref[:, cols_hi] + (w1 * hi1 + w2 * hi2)

        def more(j, carry):
            for k in range(EXPERTS_PER_STEP):
                expert_tile(xs_e, ys_e, k, bases[k], tiles[k] - 1 - j)
            return carry

        if with_experts:
            lax.fori_loop(1, functools.reduce(jnp.maximum, tiles), more, 0)

    has_experts = (o >= 1) & (o <= n_seq)
    for parity, bufs in ((0, (xs0, xs1, ys1, ys0)), (1, (xs1, xs0, ys0, ys1))):
        for flag in (True, False):
            pl.when((o % 2 == parity) & (has_experts == flag))(functools.partial(body, *bufs, flag))


def _moe(off, cnt, q1, q2, h2p_flat, x1, route, w, b, t):
    flat_rows = _sorted_rows(t) * ROW_PIECES
    n_steps = N_EXPERTS // EXPERTS_PER_STEP
    assert t % (n_steps * SUBLANES) == 0
    ts = t // n_steps
    seq_d = lambda o: jnp.minimum(o, b - 1)
    seq_c = lambda o: jnp.clip(o - 2, 0, b - 1)
    chunk = lambda o, s, *_: (jnp.where(o < 2, 0, seq_c(o) * n_steps + s), 0)
    wmap = lambda o, s, *_: (s, 0, 0)
    smem = lambda f: pl.BlockSpec((t,), lambda o, s, *_: (f(o),), memory_space=pltpu.SMEM)
    grid_spec = pltpu.PrefetchScalarGridSpec(
        num_scalar_prefetch=2,
        grid=(b + 2, n_steps),
        in_specs=[smem(seq_d), smem(seq_d), smem(seq_c), smem(seq_c),
                  pl.BlockSpec((ts * ROW_PIECES, LANES), lambda o, s, *_: (seq_d(o) * n_steps + s, 0)),
                  pl.BlockSpec((EXPERTS_PER_STEP, D_MODEL, 2 * EXPERT_FF), wmap),
                  pl.BlockSpec((EXPERTS_PER_STEP, EXPERT_FF, D_MODEL), wmap),
                  pl.BlockSpec((ts, D_MODEL), chunk),
                  pl.BlockSpec((ts, LANES), chunk)],
        out_specs=pl.BlockSpec((ts, D_MODEL), chunk),
        scratch_shapes=[pltpu.VMEM((flat_rows, LANES), jnp.uint32)] * 4
        + [pltpu.VMEM((ts * ROW_PIECES, LANES), jnp.uint32)] * 2,
    )
    return pl.pallas_call(
        functools.partial(_moe_kernel, b),
        grid_spec=grid_spec,
        out_shape=jax.ShapeDtypeStruct((b * t, D_MODEL), F32),
        compiler_params=_params("arbitrary", "arbitrary"),
        name="moe",
    )(off, cnt, q1, q2, q1, q2, h2p_flat, w["wgu"], w["wd"], x1, route)


def _head_slots(wmat, width):
    k = wmat.shape[0]
    w3 = wmat.reshape(k, N_HEADS, width)
    return jnp.pad(w3, ((0, 0), (0, 0), (0, HEAD_SLOT - width))).reshape(k, N_HEADS * HEAD_SLOT)


def _gain3(g):
    gp = jnp.pad(g.astype(F32), (0, HEAD_SLOT - QK_DIM))
    return jnp.stack([gp, jnp.roll(gp, ROPE_HALF), jnp.roll(gp, -ROPE_HALF)])


def _prepare_weights(g_mix, w_in, g_q_lora, w_uq, g_kv_lora, w_ukv, g_qk_q, g_qk_k, w_o_mla, w_dw, b_dw,
                     g_conv_ln, b_conv_ln, w_conv_out, b_gates, w_out, g_ffn, w_group_router, b_group_router,
                     w_expert_router, b_expert_router, w_e_gate, w_e_up, w_e_down):
    row = lambda a: a.astype(F32).reshape(1, -1)
    s0, s1, s2, s3 = Q_LORA, Q_LORA + KV_LORA, Q_LORA + KV_LORA + ROPE, Q_LORA + KV_LORA + ROPE + 2 * CONV_W
    w = {}
    w["g_mix"] = row(g_mix)
    w["wa"] = jnp.concatenate([w_in[:, :s1], jnp.pad(w_in[:, s1:s2], ((0, 0), (0, LANES - ROPE)))], axis=1).astype(BF16)
    w["wc"] = w_in[:, s2:s3].astype(BF16)
    w["wg"] = w_in[:, s3:].astype(BF16)
    w["b_gates"] = row(b_gates)
    w["g_q_lora"] = row(g_q_lora)
    w["wuq"] = _head_slots(w_uq, QK_DIM).astype(BF16)
    w["g_kv_lora"] = row(g_kv_lora)
    kv3 = w_ukv.reshape(KV_LORA, N_HEADS, NOPE + V_DIM)
    w["wk"] = _head_slots(kv3[:, :, :NOPE].reshape(KV_LORA, N_HEADS * NOPE), NOPE).astype(BF16)
    w["wv"] = kv3[:, :, NOPE:].reshape(KV_LORA, N_HEADS * V_DIM).astype(BF16)
    src = jnp.arange(LANES)[:, None]
    dst = jnp.arange(N_HEADS * HEAD_SLOT)[None, :]
    w["rrep"] = ((src < ROPE) & ((dst % HEAD_SLOT) == (src + NOPE))).astype(BF16)
    w["gq3"] = _gain3(g_qk_q)
    w["gk3"] = _gain3(g_qk_k)
    w["wo"] = w_o_mla.astype(BF16)
    w["w_dw"] = jnp.pad(w_dw.astype(F32), ((0, 32 - CONV_K), (0, 0)))
    w["b_dw"] = row(b_dw)
    w["g_conv_ln"] = row(g_conv_ln)
    w["b_conv_ln"] = row(b_conv_ln)
    w["wco"] = w_conv_out.astype(BF16)
    w["wout"] = w_out.astype(BF16)
    w["g_ffn"] = row(g_ffn)
    pad_r = LANES - N_EXPERTS - N_GROUPS
    wr = jnp.concatenate([w_expert_router, w_group_router, jnp.zeros((D_MODEL, pad_r), F32)], axis=1).astype(F32)
    w["wr_hi"] = wr.astype(BF16)
    w["wr_lo"] = (wr - w["wr_hi"].astype(F32)).astype(BF16)
    w["br"] = jnp.concatenate([b_expert_router, b_group_router, jnp.zeros((pad_r,), F32)]).astype(F32).reshape(1, LANES)
    w["wgu"] = jnp.concatenate([w_e_gate.astype(BF16), w_e_up.astype(BF16)], axis=-1)
    w["wd"] = w_e_down.astype(BF16)
    return w


def kernel(x, positions, g_mix, w_in, g_q_lora, w_uq, g_kv_lora, w_ukv, g_qk_q, g_qk_k, w_o_mla, w_dw, b_dw,
           g_conv_ln, b_conv_ln, w_conv_out, b_gates, w_out, g_ffn, w_group_router, b_group_router,
           w_expert_router, b_expert_router, w_e_gate, w_e_up, w_e_down):
    b, t, d = x.shape
    assert d == D_MODEL and g_mix.shape[0] == 1
    assert t % ATTN_TILE == 0 and t % RANK_CHUNK == 0 and t % COMBINE_TILE == 0
    tm = min(TOKEN_BLOCK, t)
    assert t % tm == 0 and t % CONV_CHUNK == 0
    n = b * t
    w = _prepare_weights(g_mix[0], w_in[0], g_q_lora[0], w_uq[0], g_kv_lora[0], w_ukv[0], g_qk_q[0], g_qk_k[0],
                         w_o_mla[0], w_dw[0], b_dw[0], g_conv_ln[0], b_conv_ln[0], w_conv_out[0], b_gates[0],
                         w_out[0], g_ffn[0], w_group_router[0], b_group_router[0], w_expert_router[0],
                         b_expert_router[0], w_e_gate[0], w_e_up[0], w_e_down[0])
    x2 = x.reshape(n, d)
    cos, sin = _rope_tables(positions)
    q, k, v, u, gates = _pre_attention(x2, cos, sin, w, tm)
    attn, conv = _attention(q, k, v, u, w, b, t)
    x1, h2p, route = _merge(attn, conv, gates, x2, w, tm)
    pos, meta = _rank(route, b, t)
    pos3 = pos.reshape(b, SUBLANES, t)
    q1 = pos3[:, 0, :].reshape(n)
    q2 = pos3[:, 1, :].reshape(n)
    meta3 = meta.reshape(b, SUBLANES, LANES)
    off = meta3[:, 0, :N_EXPERTS].reshape(b * N_EXPERTS)
    cnt = meta3[:, 1, :N_EXPERTS].reshape(b * N_EXPERTS)
    out = _moe(off, cnt, q1, q2, h2p.reshape(n * ROW_PIECES, LANES), x1, route, w, b, t)
    return out.reshape(b, t, d)
```

```python
import functools

import jax
import jax.numpy as jnp
import numpy as np
from jax import lax
from jax.experimental import pallas as pl
from jax.experimental.pallas import tpu as pltpu

D_MODEL = 1024
N_HEADS = 8
NOPE = 64
ROPE = 32
ROPE_HALF = ROPE // 2
QK_DIM = NOPE + ROPE
V_DIM = 64
Q_LORA = 256
KV_LORA = 128
CONV_W = 512
CONV_K = 31
N_GROUPS = 4
EXPERTS_PER_GROUP = 8
N_EXPERTS = N_GROUPS * EXPERTS_PER_GROUP
EXPERT_FF = 256
EPS = 1e-6
ROPE_THETA = 10000.0
CHUNK = 64

LANES = 128
SUBLANES = 8
HEAD_SLOT = LANES
VMEM_LIMIT_BYTES = 56 * 1024 * 1024

TOKEN_BLOCK = 512
ATTN_TILE = 256
CONV_HALO = 32
ROW_TILE = 256
EXPERTS_PER_STEP = 4
COMBINE_TILE = 256
RANK_CHUNK = 256
SUB_ROWS = 256
CONV_CHUNK = 32

F32 = jnp.float32
BF16 = jnp.bfloat16
NEG_INF = float("-inf")
LOG2E = 1.4426950408889634


def _dot(a, b, **kw):
    return jnp.dot(a, b, preferred_element_type=F32, **kw)


def _rms(x, g):
    return x * lax.rsqrt(jnp.mean(x * x, axis=-1, keepdims=True) + EPS) * g


def _sigmoid(x):
    return 1.0 / (1.0 + jnp.exp(-x))


def _pack_words(lo, hi):
    return pltpu.pack_elementwise([lo, hi], packed_dtype=BF16)


def _unpack_words(wd):
    lo = pltpu.unpack_elementwise(wd, index=0, packed_dtype=BF16, unpacked_dtype=F32)
    hi = pltpu.unpack_elementwise(wd, index=1, packed_dtype=BF16, unpacked_dtype=F32)
    return lo, hi


PACKED_TILE = (D_MODEL // 2 // LANES, SUBLANES, LANES)
ROW_PIECES = PACKED_TILE[0]


def _store_tile_rows(ref, g0, words, lead=()):
    groups = words.shape[0] // SUBLANES
    for c in range(ROW_PIECES):
        piece = words[:, c * LANES:(c + 1) * LANES].reshape(groups, SUBLANES, LANES)
        ref[lead + (pl.ds(g0, groups), c)] = piece


def _load_tile_rows(ref, g0, rows, lead=()):
    groups = rows // SUBLANES
    pieces = [ref[lead + (pl.ds(g0, groups), c)].reshape(rows, LANES) for c in range(ROW_PIECES)]
    return jnp.concatenate(pieces, axis=1)


def _row_ds(q):
    return pl.ds(q, ROW_PIECES, stride=SUBLANES)


def _params(*sem):
    return pltpu.CompilerParams(dimension_semantics=sem, vmem_limit_bytes=VMEM_LIMIT_BYTES)


def _full(shape):
    nd = len(shape)
    return pl.BlockSpec(shape, lambda *_: (0,) * nd)


def _rope_table_kernel(pos_ref, freq_ref, cos_ref, sin_ref):
    ang = pos_ref[...].astype(F32) * freq_ref[...]
    cos_ref[...] = jnp.cos(ang)
    sin_ref[...] = jnp.sin(ang)


def _rope_tables(positions):
    n = positions.size
    per_row = LANES // ROPE_HALF
    rows = n // per_row
    pos_rep = jnp.repeat(positions.reshape(rows, per_row), ROPE_HALF, axis=1)
    inv_freq = ROPE_THETA ** (-jnp.arange(ROPE_HALF, dtype=F32) / ROPE_HALF)
    freq = jnp.tile(inv_freq, per_row).reshape(1, LANES)
    rb = min(rows, 512)
    cos, sin = pl.pallas_call(
        _rope_table_kernel,
        grid=(rows // rb,),
        in_specs=[pl.BlockSpec((rb, LANES), lambda i: (i, 0)), _full((1, LANES))],
        out_specs=[pl.BlockSpec((rb, LANES), lambda i: (i, 0))] * 2,
        out_shape=[jax.ShapeDtypeStruct((rows, LANES), F32)] * 2,
        compiler_params=_params("parallel"),
        name="rope_tables",
    )(pos_rep, freq)
    return cos.reshape(n, ROPE_HALF), sin.reshape(n, ROPE_HALF)


def _pre_attention_kernel(x_ref, cos_ref, sin_ref, gmix_ref, wa_ref, wc_ref, wg_ref, bg_ref,
                          gql_ref, wuq_ref, gkv_ref, wk_ref, wv_ref, gq_ref, gk_ref,
                          q_out, k_out, v_out, u_out, gate_out):
    tm = x_ref.shape[0]
    hb = _rms(x_ref[...], gmix_ref[...]).astype(BF16)

    a = _dot(hb, wa_ref[...])
    qdn = _rms(a[:, :Q_LORA], gql_ref[...]).astype(BF16)
    kvn = _rms(a[:, Q_LORA:Q_LORA + KV_LORA], gkv_ref[...]).astype(BF16)
    kr = a[:, Q_LORA + KV_LORA:].astype(BF16)
    qf = _dot(qdn, wuq_ref[...])
    kf = _dot(jnp.concatenate([kvn, kr], axis=1), wk_ref[...])
    v_out[...] = _dot(kvn, wv_ref[...]).astype(BF16)

    cos = cos_ref[...]
    sin = sin_ref[...]
    z = lambda w: jnp.zeros((tm, w), F32)
    c_tab = jnp.concatenate([jnp.ones((tm, NOPE), F32), cos, cos, z(HEAD_SLOT - QK_DIM)], axis=-1)
    s_up = jnp.concatenate([z(NOPE + ROPE_HALF), sin, z(HEAD_SLOT - QK_DIM)], axis=-1)
    s_dn = jnp.concatenate([z(NOPE), -sin, z(HEAD_SLOT - NOPE - ROPE_HALF)], axis=-1)

    def norm_rope(f, g_ref, out, scale):
        ta = c_tab * g_ref[0:1, :]
        tu = s_up * g_ref[1:2, :]
        td = s_dn * g_ref[2:3, :]
        for h in range(N_HEADS):
            xs = f[:, h * HEAD_SLOT:(h + 1) * HEAD_SLOT]
            ss = jnp.sum(xs * xs, axis=-1, keepdims=True)
            s = lax.rsqrt(ss * (1.0 / QK_DIM) + EPS) * scale
            y = xs * ta + pltpu.roll(xs, ROPE_HALF, 1) * tu + pltpu.roll(xs, HEAD_SLOT - ROPE_HALF, 1) * td
            out[:, h * HEAD_SLOT:(h + 1) * HEAD_SLOT] = (y * s).astype(BF16)

    norm_rope(qf, gq_ref, q_out, LOG2E * QK_DIM ** -0.5)
    norm_rope(kf, gk_ref, k_out, 1.0)

    c = _dot(hb, wc_ref[...])
    u_out[...] = c[:, :CONV_W] * _sigmoid(c[:, CONV_W:])
    gate_out[...] = _sigmoid(_dot(hb, wg_ref[...]) + bg_ref[...]).astype(BF16)


def _pre_attention(x2, cos, sin, w, tm):
    n = x2.shape[0]
    row = lambda width: pl.BlockSpec((tm, width), lambda i: (i, 0))
    hs = N_HEADS * HEAD_SLOT
    weights = [w["g_mix"], w["wa"], w["wc"], w["wg"], w["b_gates"], w["g_q_lora"], w["wuq"],
               w["g_kv_lora"], w["wk"], w["wv"], w["gq3"], w["gk3"]]
    return pl.pallas_call(
        _pre_attention_kernel,
        grid=(n // tm,),
        in_specs=[row(D_MODEL), row(ROPE_HALF), row(ROPE_HALF)] + [_full(a.shape) for a in weights],
        out_specs=[row(hs), row(hs), row(N_HEADS * V_DIM), row(CONV_W), row(2 * D_MODEL)],
        out_shape=[jax.ShapeDtypeStruct((n, hs), BF16), jax.ShapeDtypeStruct((n, hs), BF16),
                   jax.ShapeDtypeStruct((n, N_HEADS * V_DIM), BF16), jax.ShapeDtypeStruct((n, CONV_W), F32),
                   jax.ShapeDtypeStruct((n, 2 * D_MODEL), BF16)],
        compiler_params=_params("parallel"),
        name="pre_attention",
    )(x2, cos, sin, *weights)


def _causal_conv_slice(u_ref, wdw_ref, bdw_ref, conv_ref, sh_ref):
    seq = u_ref.shape[0]
    sh_ref[0, 0:CONV_HALO, :] = jnp.zeros((CONV_HALO, LANES), F32)
    sh_ref[0, CONV_HALO:, :] = u_ref[...]
    span = seq + CONV_HALO - SUBLANES
    for s in range(1, SUBLANES):
        sh_ref[s, 0:span, :] = sh_ref[0, s:s + span, :]
    base = CONV_HALO - (CONV_K - 1)
    done = []
    for ck in range(seq // CONV_CHUNK):
        c0 = ck * CONV_CHUNK
        acc = jnp.broadcast_to(bdw_ref[...], (CONV_CHUNK, LANES))
        for j in range(CONV_K):
            s, a = (base + j) % SUBLANES, (base + j) // SUBLANES * SUBLANES
            acc = acc + sh_ref[s, c0 + a:c0 + a + CONV_CHUNK, :] * wdw_ref[j:j + 1, :]
        conv_ref[c0:c0 + CONV_CHUNK, :] = acc
        done.append(acc[0:SUBLANES, :])
    return done


def _ordering_zero(values):
    bits = functools.reduce(jnp.bitwise_or, [lax.bitcast_convert_type(v, jnp.uint32) for v in values])
    half_width = jnp.uint32(16)
    return lax.bitcast_convert_type(
        lax.shift_right_logical(lax.shift_right_logical(bits, half_width), half_width), F32)


def _attention_kernel(q_ref, k_ref, v_ref, u_ref, wdw_ref, bdw_ref, o_ref, conv_ref, kbd, vbd, sh_ref):
    conv_done = _causal_conv_slice(u_ref, wdw_ref, bdw_ref, conv_ref, sh_ref)
    n_q = q_ref.shape[0] // ATTN_TILE
    work = [i + 1 for i in range(n_q)]
    share_end = [len(conv_done) * sum(work[:i + 1]) // sum(work) for i in range(n_q)]

    t = ATTN_TILE
    hk = HEAD_SLOT
    seq = q_ref.shape[0]
    lane = lax.broadcasted_iota(jnp.int32, (hk, HEAD_SLOT), 1)
    zero = jnp.zeros((hk, HEAD_SLOT), BF16)
    for c in range(seq // hk):
        rows = slice(c * hk, (c + 1) * hk)
        top = slice(2 * c * hk, (2 * c + 1) * hk)
        bot = slice((2 * c + 1) * hk, (2 * c + 2) * hk)
        kbd[top, :HEAD_SLOT] = k_ref[rows, :HEAD_SLOT]
        kbd[top, HEAD_SLOT:] = zero
        kbd[bot, :HEAD_SLOT] = zero
        kbd[bot, HEAD_SLOT:] = k_ref[rows, HEAD_SLOT:]
        vp = v_ref[rows, :]
        vbd[top, :] = jnp.where(lane < V_DIM, vp, zero)
        vbd[bot, :] = jnp.where(lane >= V_DIM, vp, zero)

    qrow = lax.broadcasted_iota(jnp.int32, (t, hk), 0) // CHUNK
    kcol = lax.broadcasted_iota(jnp.int32, (t, hk), 1)
    diag_masks = [(kcol + h2 * hk) // CHUNK <= qrow for h2 in range(t // hk)]
    out_lane = lax.broadcasted_iota(jnp.int32, (t, HEAD_SLOT), 1)
    for i in range(seq // t):
        rows = slice(i * t, (i + 1) * t)
        n_tiles = 2 * (i + 1) * t // hk
        s = lax.dot_general(q_ref[rows, :], kbd[0:n_tiles * hk, :], (((1,), (1,)), ((), ())),
                            preferred_element_type=F32)
        tiles = [s[:, j * hk:(j + 1) * hk] for j in range(n_tiles)]
        conv_deps = conv_done[(share_end[i - 1] if i else 0):share_end[i]]
        if conv_deps:
            tiles[0] = tiles[0] + jnp.tile(_ordering_zero(conv_deps), (t // SUBLANES, 1))
        first_diag = n_tiles - 2 * (t // hk)
        for j in range(first_diag, n_tiles):
            tiles[j] = jnp.where(diag_masks[(j - first_diag) // 2], tiles[j], NEG_INF)
        probs = [None] * n_tiles
        inv_l = []
        for h in range(2):
            mine = range(h, n_tiles, 2)
            m = jnp.max(functools.reduce(jnp.maximum, [tiles[j] for j in mine]), axis=-1, keepdims=True)
            for j in mine:
                probs[j] = jnp.exp2(tiles[j] - m)
            l = jnp.sum(functools.reduce(jnp.add, [probs[j] for j in mine]), axis=-1, keepdims=True)
            inv_l.append(1.0 / l)
        p = jnp.concatenate(probs, axis=1).astype(BF16)
        o = _dot(p, vbd[0:n_tiles * hk, :])
        o_ref[rows, :] = (o * jnp.where(out_lane < V_DIM, inv_l[0], inv_l[1])).astype(BF16)


def _attention(q, k, v, u, w, b, t):
    hs2 = 2 * HEAD_SLOT
    assert CONV_W == (N_HEADS // 2) * LANES
    spec = pl.BlockSpec((t, hs2), lambda bi, p: (bi, p))
    lane_tile = pl.BlockSpec((t, LANES), lambda bi, p: (bi, p))
    wslice = lambda rows: pl.BlockSpec((rows, LANES), lambda bi, p: (0, p))
    return pl.pallas_call(
        _attention_kernel,
        grid=(b, N_HEADS // 2),
        in_specs=[spec, spec, lane_tile, lane_tile, wslice(w["w_dw"].shape[0]), wslice(1)],
        out_specs=[lane_tile, lane_tile],
        out_shape=[jax.ShapeDtypeStruct((b * t, N_HEADS * V_DIM), BF16),
                   jax.ShapeDtypeStruct((b * t, CONV_W), F32)],
        scratch_shapes=[pltpu.VMEM((2 * t, hs2), BF16), pltpu.VMEM((2 * t, 2 * V_DIM), BF16),
                        pltpu.VMEM((SUBLANES, t + CONV_HALO, LANES), F32)],
        compiler_params=_params("parallel", "parallel"),
        name="attention",
    )(q, k, v, u, w["w_dw"], w["b_dw"])


def _merge_kernel(attn_ref, conv_ref, gate_ref, x_ref, wo_ref, gln_ref, bln_ref, wco_ref, wout_ref, gffn_ref,
                  wrh_ref, wrl_ref, br_ref, x1_out, h2p_out, route_out):
    tm = x_ref.shape[0]
    half = D_MODEL // 2
    for sb in range(tm // SUB_ROWS):
        rows = slice(sb * SUB_ROWS, (sb + 1) * SUB_ROWS)
        conv = conv_ref[rows, :]
        mu = jnp.mean(conv, axis=-1, keepdims=True)
        cen = conv - mu
        var = jnp.mean(cen * cen, axis=-1, keepdims=True)
        ln = cen * lax.rsqrt(var + EPS) * gln_ref[...] + bln_ref[...]
        ub = (ln * _sigmoid(ln)).astype(BF16)
        y_b = _dot(ub, wco_ref[...])
        y_a = _dot(attn_ref[rows, :], wo_ref[...])
        merged = gate_ref[rows, :D_MODEL].astype(F32) * y_a + gate_ref[rows, D_MODEL:].astype(F32) * y_b
        x1 = x_ref[rows, :] + _dot(merged.astype(BF16), wout_ref[...])
        x1_out[rows, :] = x1
        h2 = _rms(x1, gffn_ref[...])
        _store_tile_rows(h2p_out, sb * (SUB_ROWS // SUBLANES), _pack_words(h2[:, :half], h2[:, half:]))
        h2_hi = h2.astype(BF16)
        h2_lo = (h2 - h2_hi.astype(F32)).astype(BF16)
        logits = (_dot(h2_hi, wrh_ref[...]) + _dot(h2_lo, wrh_ref[...]) + _dot(h2_hi, wrl_ref[...])
                  + br_ref[...])
        route_out[rows, :] = _route(logits)


def _route(logits):
    lane_i = lax.broadcasted_iota(jnp.int32, logits.shape, 1)
    lane = lane_i.astype(F32)
    big = float(LANES)
    is_g = (lane_i >= N_EXPERTS) & (lane_i < N_EXPERTS + N_GROUPS)
    gl = jnp.where(is_g, logits, NEG_INF)
    gmax = jnp.max(gl, axis=-1, keepdims=True)
    gidx = jnp.min(jnp.where(gl == gmax, lane - N_EXPERTS, big), axis=-1, keepdims=True)
    g_p = 1.0 / jnp.sum(jnp.exp(gl - gmax), axis=-1, keepdims=True)
    lane_group = (lane_i // EXPERTS_PER_GROUP).astype(F32)
    valid = (lane_i < N_EXPERTS) & (lane_group == gidx)
    el = jnp.where(valid, logits, NEG_INF)
    ee = jnp.exp(el - jnp.max(el, axis=-1, keepdims=True))
    within = jnp.where(valid, ee / jnp.sum(ee, axis=-1, keepdims=True), -1.0)
    p1 = jnp.max(within, axis=-1, keepdims=True)
    i1 = jnp.min(jnp.where(within == p1, lane, big), axis=-1, keepdims=True)
    within2 = jnp.where(lane == i1, -1.0, within)
    p2 = jnp.max(within2, axis=-1, keepdims=True)
    i2 = jnp.min(jnp.where(within2 == p2, lane, big), axis=-1, keepdims=True)
    psum = p1 + p2
    w1 = g_p * (p1 / psum)
    w2 = g_p * (p2 / psum)
    route = jnp.where(lane_i == 0, i1, 0.0)
    route = jnp.where(lane_i == 1, i2, route)
    route = jnp.where(lane_i == 2, w1, route)
    return jnp.where(lane_i == 3, w2, route)


def _merge(attn, conv, gates, x2, w, tm):
    n = x2.shape[0]
    row = lambda width: pl.BlockSpec((tm, width), lambda i: (i, 0))
    weights = [w["wo"], w["g_conv_ln"], w["b_conv_ln"], w["wco"], w["wout"],
               w["g_ffn"], w["wr_hi"], w["wr_lo"], w["br"]]
    tile_rows = (tm // SUBLANES,) + PACKED_TILE
    return pl.pallas_call(
        _merge_kernel,
        grid=(n // tm,),
        in_specs=[row(N_HEADS * V_DIM), row(CONV_W), row(2 * D_MODEL), row(D_MODEL)]
        + [_full(a.shape) for a in weights],
        out_specs=[row(D_MODEL), pl.BlockSpec(tile_rows, lambda i: (i, 0, 0, 0)), row(LANES)],
        out_shape=[jax.ShapeDtypeStruct((n, D_MODEL), F32),
                   jax.ShapeDtypeStruct((n // SUBLANES,) + PACKED_TILE, jnp.uint32),
                   jax.ShapeDtypeStruct((n, LANES), F32)],
        compiler_params=_params("parallel"),
        name="merge_router",
    )(attn, conv, gates, x2, *weights)


def _rank_kernel(route_ref, pos_out, meta_out, r1_sc, r2_sc):
    tb = route_ref.shape[0]
    lane = lax.broadcasted_iota(jnp.int32, (RANK_CHUNK, LANES), 1).astype(F32)
    rr = lax.broadcasted_iota(jnp.int32, (RANK_CHUNK, RANK_CHUNK), 0)
    cc = lax.broadcasted_iota(jnp.int32, (RANK_CHUNK, RANK_CHUNK), 1)
    lower = jnp.where(rr > cc, 1.0, 0.0).astype(BF16)
    run = jnp.zeros((1, LANES), F32)
    for choice, sc in ((0, r1_sc), (1, r2_sc)):
        for ci in range(tb // RANK_CHUNK):
            rows = slice(ci * RANK_CHUNK, (ci + 1) * RANK_CHUNK)
            oh = jnp.where(lane == route_ref[rows, choice:choice + 1], 1.0, 0.0)
            sc[rows, :] = run + _dot(lower, oh.astype(BF16))
            run = run + jnp.sum(oh, axis=0, keepdims=True)
    cnt = run
    cnt8 = jnp.ceil(cnt * (1.0 / SUBLANES)) * SUBLANES
    ur = lax.broadcasted_iota(jnp.int32, (LANES, LANES), 0)
    uc = lax.broadcasted_iota(jnp.int32, (LANES, LANES), 1)
    upper = jnp.where(ur < uc, 1.0, 0.0)
    off = _dot(jnp.broadcast_to(cnt8, (SUBLANES, LANES)), upper, precision=lax.Precision.HIGHEST)[0:1, :]
    lane_t = lax.broadcasted_iota(jnp.int32, (tb, LANES), 1)
    lane_tf = lane_t.astype(F32)
    p1 = jnp.sum(jnp.where(lane_tf == route_ref[:, 0:1], off + r1_sc[...], 0.0), axis=-1, keepdims=True)
    p2 = jnp.sum(jnp.where(lane_tf == route_ref[:, 1:2], off + r2_sc[...], 0.0), axis=-1, keepdims=True)
    q1 = p1 + (SUBLANES * ROW_PIECES - SUBLANES) * jnp.floor(p1 * (1.0 / SUBLANES))
    q2 = p2 + (SUBLANES * ROW_PIECES - SUBLANES) * jnp.floor(p2 * (1.0 / SUBLANES))
    pos = jnp.where(lane_t == 0, q1, jnp.where(lane_t == 1, q2, 0.0))
    pos_out[...] = jnp.transpose(pos)[0:SUBLANES, :].astype(jnp.int32)
    row8 = lax.broadcasted_iota(jnp.int32, (SUBLANES, LANES), 0)
    meta = jnp.where(row8 == 0, jnp.broadcast_to(off, (SUBLANES, LANES)),
                     jnp.where(row8 == 1, jnp.broadcast_to(cnt, (SUBLANES, LANES)), 0.0))
    meta_out[...] = meta.astype(jnp.int32)


def _rank(route, b, t):
    return pl.pallas_call(
        _rank_kernel,
        grid=(b,),
        in_specs=[pl.BlockSpec((t, LANES), lambda i: (i, 0))],
        out_specs=[pl.BlockSpec((SUBLANES, t), lambda i: (i, 0)), pl.BlockSpec((SUBLANES, LANES), lambda i: (i, 0))],
        out_shape=[jax.ShapeDtypeStruct((b * SUBLANES, t), jnp.int32),
                   jax.ShapeDtypeStruct((b * SUBLANES, LANES), jnp.int32)],
        scratch_shapes=[pltpu.VMEM((t, LANES), F32), pltpu.VMEM((t, LANES), F32)],
        compiler_params=_params("parallel"),
        name="rank",
    )(route)


def _sorted_rows(t):
    return 2 * t + N_EXPERTS * SUBLANES + ROW_TILE


GROUP_ROWS = SUBLANES * ROW_PIECES


def _load_flat_tile(ref, base, rows):
    cols = []
    for c in range(ROW_PIECES):
        cols.append(jnp.concatenate(
            [ref[pl.ds(base + g * GROUP_ROWS + c * SUBLANES, SUBLANES), :] for g in range(rows // SUBLANES)], axis=0))
    return jnp.concatenate(cols, axis=1)


def _store_flat_tile(ref, base, words):
    for c in range(ROW_PIECES):
        for g in range(words.shape[0] // SUBLANES):
            ref[pl.ds(base + g * GROUP_ROWS + c * SUBLANES, SUBLANES), :] = (
                words[g * SUBLANES:(g + 1) * SUBLANES, c * LANES:(c + 1) * LANES])


def _moe_kernel(n_seq, off_ref, cnt_ref, qd1_ref, qd2_ref, qc1_ref, qc2_ref, h2p_ref, wg_ref, wu_ref, wd_ref,
                x1_ref, route_ref, out_ref, xs0, xs1, ys0, ys1, g1, g2):
    o = pl.program_id(0)
    s = pl.program_id(1)
    ts = x1_ref.shape[0]
    half = D_MODEL // 2
    seq_e = jnp.clip(o - 1, 0, n_seq - 1)

    @pl.when((o == 0) & (s == 0))
    def _init():
        for ref in (xs0, xs1, ys0, ys1):
            ref[...] = jnp.zeros(ref.shape, ref.dtype)

    def expert_tile(xs_e, ys_e, k, base, i):
        start = pl.multiple_of(jnp.where(i >= 0, base + i * (ROW_TILE * ROW_PIECES),
                                         xs_e.shape[0] - ROW_TILE * ROW_PIECES), GROUP_ROWS)
        lo, hi = _unpack_words(_load_flat_tile(xs_e, start, ROW_TILE))
        xb = jnp.concatenate([lo.astype(BF16), hi.astype(BF16)], axis=1)
        a = _dot(xb, wg_ref[k])
        mid = (a * _sigmoid(a) * _dot(xb, wu_ref[k])).astype(BF16)
        y = _dot(mid, wd_ref[k])
        _store_flat_tile(ys_e, start, _pack_words(y[:, :half], y[:, half:]))

    def body(xs_d, xs_e, ys_e, ys_c, with_experts):
        @pl.when(s == 0)
        def _clear():
            xs_d[...] = jnp.zeros(xs_d.shape, xs_d.dtype)

        bases, tiles = [], []
        if with_experts:
            for k in range(EXPERTS_PER_STEP):
                idx = seq_e * N_EXPERTS + s * EXPERTS_PER_STEP + k
                bases.append(off_ref[idx] * ROW_PIECES)
                tiles.append(lax.shift_right_logical(cnt_ref[idx] + (ROW_TILE - 1), ROW_TILE.bit_length() - 1))
            for k in range(EXPERTS_PER_STEP):
                expert_tile(xs_e, ys_e, k, bases[k], tiles[k] - 1)

        t0 = s * ts
        for g in range(ts // SUBLANES):
            for r in range(SUBLANES):
                row = h2p_ref[_row_ds(g * GROUP_ROWS + r), :]
                tok = t0 + g * SUBLANES + r
                xs_d[_row_ds(qd1_ref[tok]), :] = row
                xs_d[_row_ds(qd2_ref[tok]), :] = row

        for g in range(ts // SUBLANES):
            for r in range(SUBLANES):
                tok = t0 + g * SUBLANES + r
                g1[_row_ds(g * GROUP_ROWS + r), :] = ys_c[_row_ds(qc1_ref[tok]), :]
                g2[_row_ds(g * GROUP_ROWS + r), :] = ys_c[_row_ds(qc2_ref[tok]), :]
        w1 = route_ref[:, 2:3]
        w2 = route_ref[:, 3:4]
        for c in range(ROW_PIECES):
            tile_of = lambda ref: jnp.concatenate(
                [ref[g * GROUP_ROWS + c * SUBLANES:g * GROUP_ROWS + (c + 1) * SUBLANES, :]
                 for g in range(ts // SUBLANES)], axis=0)
            lo1, hi1 = _unpack_words(tile_of(g1))
            lo2, hi2 = _unpack_words(tile_of(g2))
            cols = slice(c * LANES, (c + 1) * LANES)
            cols_hi = slice(half + c * LANES, half + (c + 1) * LANES)
            out_ref[:, cols] = x1_ref[:, cols] + (w1 * lo1 + w2 * lo2)
            out_ref[:, cols_hi] = x1_ref[:, cols_hi] + (w1 * hi1 + w2 * hi2)

        def more(j, carry):
            for k in range(EXPERTS_PER_STEP):
                expert_tile(xs_e, ys_e, k, bases[k], tiles[k] - 1 - j)
            return carry

        if with_experts:
            lax.fori_loop(1, functools.reduce(jnp.maximum, tiles), more, 0)

    has_experts = (o >= 1) & (o <= n_seq)
    for parity, bufs in ((0, (xs0, xs1, ys1, ys0)), (1, (xs1, xs0, ys0, ys1))):
        for flag in (True, False):
            pl.when((o % 2 == parity) & (has_experts == flag))(functools.partial(body, *bufs, flag))


def _moe(off, cnt, q1, q2, h2p_flat, x1, route, w, b, t):
    flat_rows = _sorted_rows(t) * ROW_PIECES
    n_steps = N_EXPERTS // EXPERTS_PER_STEP
    assert t % (n_steps * SUBLANES) == 0
    ts = t // n_steps
    seq_d = lambda o: jnp.minimum(o, b - 1)
    seq_c = lambda o: jnp.clip(o - 2, 0, b - 1)
    chunk = lambda o, s, *_: (jnp.where(o < 2, 0, seq_c(o) * n_steps + s), 0)
    wmap = lambda o, s, *_: (s, 0, 0)
    smem = lambda f: pl.BlockSpec((t,), lambda o, s, *_: (f(o),), memory_space=pltpu.SMEM)
    grid_spec = pltpu.PrefetchScalarGridSpec(
        num_scalar_prefetch=2,
        grid=(b + 2, n_steps),
        in_specs=[smem(seq_d), smem(seq_d), smem(seq_c), smem(seq_c),
                  pl.BlockSpec((ts * ROW_PIECES, LANES), lambda o, s, *_: (seq_d(o) * n_steps + s, 0)),
                  pl.BlockSpec((EXPERTS_PER_STEP, D_MODEL, EXPERT_FF), wmap),
                  pl.BlockSpec((EXPERTS_PER_STEP, D_MODEL, EXPERT_FF), wmap),
                  pl.BlockSpec((EXPERTS_PER_STEP, EXPERT_FF, D_MODEL), wmap),
                  pl.BlockSpec((ts, D_MODEL), chunk),
                  pl.BlockSpec((ts, LANES), chunk)],
        out_specs=pl.BlockSpec((ts, D_MODEL), chunk),
        scratch_shapes=[pltpu.VMEM((flat_rows, LANES), jnp.uint32)] * 4
        + [pltpu.VMEM((ts * ROW_PIECES, LANES), jnp.uint32)] * 2,
    )
    return pl.pallas_call(
        functools.partial(_moe_kernel, b),
        grid_spec=grid_spec,
        out_shape=jax.ShapeDtypeStruct((b * t, D_MODEL), F32),
        compiler_params=_params("arbitrary", "arbitrary"),
        name="moe",
    )(off, cnt, q1, q2, q1, q2, h2p_flat, w["wg_e"], w["wu_e"], w["wd"], x1, route)


def _head_slots(wmat, width):
    k = wmat.shape[0]
    w3 = wmat.reshape(k, N_HEADS, width)
    return jnp.pad(w3, ((0, 0), (0, 0), (0, HEAD_SLOT - width))).reshape(k, N_HEADS * HEAD_SLOT)


def _gain3(g):
    gp = jnp.pad(g.astype(F32), (0, HEAD_SLOT - QK_DIM))
    return jnp.stack([gp, jnp.roll(gp, ROPE_HALF), jnp.roll(gp, -ROPE_HALF)])


def _prepare_weights(g_mix, w_in, g_q_lora, w_uq, g_kv_lora, w_ukv, g_qk_q, g_qk_k, w_o_mla, w_dw, b_dw,
                     g_conv_ln, b_conv_ln, w_conv_out, b_gates, w_out, g_ffn, w_group_router, b_group_router,
                     w_expert_router, b_expert_router, w_e_gate, w_e_up, w_e_down):
    row = lambda a: a.astype(F32).reshape(1, -1)
    s0, s1, s2, s3 = Q_LORA, Q_LORA + KV_LORA, Q_LORA + KV_LORA + ROPE, Q_LORA + KV_LORA + ROPE + 2 * CONV_W
    w = {}
    w["g_mix"] = row(g_mix)
    w["wa"] = jnp.concatenate([w_in[:, :s1], jnp.pad(w_in[:, s1:s2], ((0, 0), (0, LANES - ROPE)))], axis=1).astype(BF16)
    w["wc"] = w_in[:, s2:s3].astype(BF16)
    w["wg"] = w_in[:, s3:].astype(BF16)
    w["b_gates"] = row(b_gates)
    w["g_q_lora"] = row(g_q_lora)
    w["wuq"] = _head_slots(w_uq, QK_DIM).astype(BF16)
    w["g_kv_lora"] = row(g_kv_lora)
    kv3 = w_ukv.reshape(KV_LORA, N_HEADS, NOPE + V_DIM)
    wk_nope = _head_slots(kv3[:, :, :NOPE].reshape(KV_LORA, N_HEADS * NOPE), NOPE)
    w["wv"] = kv3[:, :, NOPE:].reshape(KV_LORA, N_HEADS * V_DIM).astype(BF16)
    src = jnp.arange(LANES)[:, None]
    dst = jnp.arange(N_HEADS * HEAD_SLOT)[None, :]
    rope_copy = ((src < ROPE) & ((dst % HEAD_SLOT) == (src + NOPE))).astype(F32)
    w["wk"] = jnp.concatenate([wk_nope, rope_copy], axis=0).astype(BF16)
    w["gq3"] = _gain3(g_qk_q)
    w["gk3"] = _gain3(g_qk_k)
    w["wo"] = w_o_mla.astype(BF16)
    w["w_dw"] = jnp.pad(w_dw.astype(F32), ((0, 32 - CONV_K), (0, 0)))
    w["b_dw"] = row(b_dw)
    w["g_conv_ln"] = row(g_conv_ln)
    w["b_conv_ln"] = row(b_conv_ln)
    w["wco"] = w_conv_out.astype(BF16)
    w["wout"] = w_out.astype(BF16)
    w["g_ffn"] = row(g_ffn)
    pad_r = LANES - N_EXPERTS - N_GROUPS
    wr = jnp.concatenate([w_expert_router, w_group_router, jnp.zeros((D_MODEL, pad_r), F32)], axis=1).astype(F32)
    w["wr_hi"] = wr.astype(BF16)
    w["wr_lo"] = (wr - w["wr_hi"].astype(F32)).astype(BF16)
    w["br"] = jnp.concatenate([b_expert_router, b_group_router, jnp.zeros((pad_r,), F32)]).astype(F32).reshape(1, LANES)
    w["wg_e"] = w_e_gate.astype(BF16)
    w["wu_e"] = w_e_up.astype(BF16)
    w["wd"] = w_e_down.astype(BF16)
    return w


def kernel(x, positions, g_mix, w_in, g_q_lora, w_uq, g_kv_lora, w_ukv, g_qk_q, g_qk_k, w_o_mla, w_dw, b_dw,
           g_conv_ln, b_conv_ln, w_conv_out, b_gates, w_out, g_ffn, w_group_router, b_group_router,
           w_expert_router, b_expert_router, w_e_gate, w_e_up, w_e_down):
    b, t, d = x.shape
    assert d == D_MODEL and g_mix.shape[0] == 1
    assert t % ATTN_TILE == 0 and t % RANK_CHUNK == 0 and t % COMBINE_TILE == 0
    tm = min(TOKEN_BLOCK, t)
    assert t % tm == 0 and t % CONV_CHUNK == 0
    n = b * t
    w = _prepare_weights(g_mix[0], w_in[0], g_q_lora[0], w_uq[0], g_kv_lora[0], w_ukv[0], g_qk_q[0], g_qk_k[0],
                         w_o_mla[0], w_dw[0], b_dw[0], g_conv_ln[0], b_conv_ln[0], w_conv_out[0], b_gates[0],
                         w_out[0], g_ffn[0], w_group_router[0], b_group_router[0], w_expert_router[0],
                         b_expert_router[0], w_e_gate[0], w_e_up[0], w_e_down[0])
    x2 = x.reshape(n, d)
    cos, sin = _rope_tables(positions)
    q, k, v, u, gates = _pre_attention(x2, cos, sin, w, tm)
    attn, conv = _attention(q, k, v, u, w, b, t)
    x1, h2p, route = _merge(attn, conv, gates, x2, w, tm)
    pos, meta = _rank(route, b, t)
    pos3 = pos.reshape(b, SUBLANES, t)
    q1 = pos3[:, 0, :].reshape(n)
    q2 = pos3[:, 1, :].reshape(n)
    meta3 = meta.reshape(b, SUBLANES, LANES)
    off = meta3[:, 0, :N_EXPERTS].reshape(b * N_EXPERTS)
    cnt = meta3[:, 1, :N_EXPERTS].reshape(b * N_EXPERTS)
    out = _moe(off, cnt, q1, q2, h2p.reshape(n * ROW_PIECES, LANES), x1, route, w, b, t)
    return out.reshape(b, t, d)
```

```python
import functools

import jax
import jax.numpy as jnp
import numpy as np
from jax import lax
from jax.experimental import pallas as pl
from jax.experimental.pallas import tpu as pltpu

D_MODEL = 1024
N_HEADS = 8
NOPE = 64
ROPE = 32
ROPE_HALF = ROPE // 2
QK_DIM = NOPE + ROPE
V_DIM = 64
Q_LORA = 256
KV_LORA = 128
CONV_W = 512
CONV_K = 31
N_GROUPS = 4
EXPERTS_PER_GROUP = 8
N_EXPERTS = N_GROUPS * EXPERTS_PER_GROUP
EXPERT_FF = 256
EPS = 1e-6
ROPE_THETA = 10000.0
CHUNK = 64

LANES = 128
SUBLANES = 8
HEAD_SLOT = LANES
VMEM_LIMIT_BYTES = 56 * 1024 * 1024

TOKEN_BLOCK = 512
ATTN_TILE = 256
CONV_HALO = 32
ROW_TILE = 256
EXPERTS_PER_STEP = 4
COMBINE_TILE = 256
RANK_CHUNK = 256
SUB_ROWS = 512
CONV_CHUNK = 16

F32 = jnp.float32
BF16 = jnp.bfloat16
NEG_INF = float("-inf")
LOG2E = 1.4426950408889634


def _dot(a, b, **kw):
    return jnp.dot(a, b, preferred_element_type=F32, **kw)


def _rms(x, g):
    return x * lax.rsqrt(jnp.mean(x * x, axis=-1, keepdims=True) + EPS) * g


def _sigmoid(x):
    return 1.0 / (1.0 + jnp.exp(-x))


def _pack_words(lo, hi):
    return pltpu.pack_elementwise([lo, hi], packed_dtype=BF16)


def _unpack_words(wd):
    lo = pltpu.unpack_elementwise(wd, index=0, packed_dtype=BF16, unpacked_dtype=F32)
    hi = pltpu.unpack_elementwise(wd, index=1, packed_dtype=BF16, unpacked_dtype=F32)
    return lo, hi


PACKED_TILE = (D_MODEL // 2 // LANES, SUBLANES, LANES)
ROW_PIECES = PACKED_TILE[0]


def _store_tile_rows(ref, g0, words, lead=()):
    groups = words.shape[0] // SUBLANES
    for c in range(ROW_PIECES):
        piece = words[:, c * LANES:(c + 1) * LANES].reshape(groups, SUBLANES, LANES)
        ref[lead + (pl.ds(g0, groups), c)] = piece


def _load_tile_rows(ref, g0, rows, lead=()):
    groups = rows // SUBLANES
    pieces = [ref[lead + (pl.ds(g0, groups), c)].reshape(rows, LANES) for c in range(ROW_PIECES)]
    return jnp.concatenate(pieces, axis=1)


def _row_ds(q):
    return pl.ds(q, ROW_PIECES, stride=SUBLANES)


def _params(*sem):
    return pltpu.CompilerParams(dimension_semantics=sem, vmem_limit_bytes=VMEM_LIMIT_BYTES)


def _full(shape):
    nd = len(shape)
    return pl.BlockSpec(shape, lambda *_: (0,) * nd)


def _rope_table_kernel(pos_ref, freq_ref, cos_ref, sin_ref):
    ang = pos_ref[...].astype(F32) * freq_ref[...]
    cos_ref[...] = jnp.cos(ang)
    sin_ref[...] = jnp.sin(ang)


def _rope_tables(positions):
    n = positions.size
    per_row = LANES // ROPE_HALF
    rows = n // per_row
    pos_rep = jnp.repeat(positions.reshape(rows, per_row), ROPE_HALF, axis=1)
    inv_freq = ROPE_THETA ** (-jnp.arange(ROPE_HALF, dtype=F32) / ROPE_HALF)
    freq = jnp.tile(inv_freq, per_row).reshape(1, LANES)
    rb = min(rows, 512)
    cos, sin = pl.pallas_call(
        _rope_table_kernel,
        grid=(rows // rb,),
        in_specs=[pl.BlockSpec((rb, LANES), lambda i: (i, 0)), _full((1, LANES))],
        out_specs=[pl.BlockSpec((rb, LANES), lambda i: (i, 0))] * 2,
        out_shape=[jax.ShapeDtypeStruct((rows, LANES), F32)] * 2,
        compiler_params=_params("parallel"),
        name="rope_tables",
    )(pos_rep, freq)
    return cos.reshape(n, ROPE_HALF), sin.reshape(n, ROPE_HALF)


def _pre_attention_kernel(x_ref, cos_ref, sin_ref, gmix_ref, wa_ref, wc_ref, wg_ref, bg_ref,
                          gql_ref, wuq_ref, gkv_ref, wk_ref, wv_ref, gq_ref, gk_ref,
                          q_out, k_out, v_out, u_out, gate_out):
    tm = x_ref.shape[0]
    hb = _rms(x_ref[...], gmix_ref[...]).astype(BF16)

    a = _dot(hb, wa_ref[...])
    qdn = _rms(a[:, :Q_LORA], gql_ref[...]).astype(BF16)
    kvn = _rms(a[:, Q_LORA:Q_LORA + KV_LORA], gkv_ref[...]).astype(BF16)
    kr = a[:, Q_LORA + KV_LORA:].astype(BF16)
    qf = _dot(qdn, wuq_ref[...])
    kf = _dot(jnp.concatenate([kvn, kr], axis=1), wk_ref[...])
    v_out[...] = _dot(kvn, wv_ref[...]).astype(BF16)

    cos = cos_ref[...]
    sin = sin_ref[...]
    z = lambda w: jnp.zeros((tm, w), F32)
    c_tab = jnp.concatenate([jnp.ones((tm, NOPE), F32), cos, cos, z(HEAD_SLOT - QK_DIM)], axis=-1)
    s_up = jnp.concatenate([z(NOPE + ROPE_HALF), sin, z(HEAD_SLOT - QK_DIM)], axis=-1)
    s_dn = jnp.concatenate([z(NOPE), -sin, z(HEAD_SLOT - NOPE - ROPE_HALF)], axis=-1)

    def norm_rope(f, g_ref, out, scale):
        ta = c_tab * g_ref[0:1, :]
        tu = s_up * g_ref[1:2, :]
        td = s_dn * g_ref[2:3, :]
        for h in range(N_HEADS):
            xs = f[:, h * HEAD_SLOT:(h + 1) * HEAD_SLOT]
            ss = jnp.sum(xs * xs, axis=-1, keepdims=True)
            s = lax.rsqrt(ss * (1.0 / QK_DIM) + EPS) * scale
            y = xs * ta + pltpu.roll(xs, ROPE_HALF, 1) * tu + pltpu.roll(xs, HEAD_SLOT - ROPE_HALF, 1) * td
            out[:, h * HEAD_SLOT:(h + 1) * HEAD_SLOT] = (y * s).astype(BF16)

    norm_rope(qf, gq_ref, q_out, LOG2E * QK_DIM ** -0.5)
    norm_rope(kf, gk_ref, k_out, 1.0)

    c = _dot(hb, wc_ref[...])
    u_out[...] = c[:, :CONV_W] * _sigmoid(c[:, CONV_W:])
    gate_out[...] = _sigmoid(_dot(hb, wg_ref[...]) + bg_ref[...]).astype(BF16)


def _pre_attention(x2, cos, sin, w, tm):
    n = x2.shape[0]
    row = lambda width: pl.BlockSpec((tm, width), lambda i: (i, 0))
    hs = N_HEADS * HEAD_SLOT
    weights = [w["g_mix"], w["wa"], w["wc"], w["wg"], w["b_gates"], w["g_q_lora"], w["wuq"],
               w["g_kv_lora"], w["wk"], w["wv"], w["gq3"], w["gk3"]]
    return pl.pallas_call(
        _pre_attention_kernel,
        grid=(n // tm,),
        in_specs=[row(D_MODEL), row(ROPE_HALF), row(ROPE_HALF)] + [_full(a.shape) for a in weights],
        out_specs=[row(hs), row(hs), row(N_HEADS * V_DIM), row(CONV_W), row(2 * D_MODEL)],
        out_shape=[jax.ShapeDtypeStruct((n, hs), BF16), jax.ShapeDtypeStruct((n, hs), BF16),
                   jax.ShapeDtypeStruct((n, N_HEADS * V_DIM), BF16), jax.ShapeDtypeStruct((n, CONV_W), F32),
                   jax.ShapeDtypeStruct((n, 2 * D_MODEL), BF16)],
        compiler_params=_params("parallel"),
        name="pre_attention",
    )(x2, cos, sin, *weights)


def _causal_conv_slice(u_ref, wdw_ref, bdw_ref, conv_ref, sh_ref):
    seq = u_ref.shape[0]
    sh_ref[0, 0:CONV_HALO, :] = jnp.zeros((CONV_HALO, LANES), F32)
    sh_ref[0, CONV_HALO:, :] = u_ref[...]
    span = seq + CONV_HALO - SUBLANES
    for s in range(1, SUBLANES):
        sh_ref[s, 0:span, :] = sh_ref[0, s:s + span, :]
    base = CONV_HALO - (CONV_K - 1)
    done = []
    for ck in range(seq // CONV_CHUNK):
        c0 = ck * CONV_CHUNK
        acc = jnp.broadcast_to(bdw_ref[...], (CONV_CHUNK, LANES))
        for j in range(CONV_K):
            s, a = (base + j) % SUBLANES, (base + j) // SUBLANES * SUBLANES
            acc = acc + sh_ref[s, c0 + a:c0 + a + CONV_CHUNK, :] * wdw_ref[j:j + 1, :]
        conv_ref[c0:c0 + CONV_CHUNK, :] = acc
        done.append(acc[0:SUBLANES, :])
    return done


def _ordering_zero(values):
    bits = functools.reduce(jnp.bitwise_or, [lax.bitcast_convert_type(v, jnp.uint32) for v in values])
    half_width = jnp.uint32(16)
    return lax.bitcast_convert_type(
        lax.shift_right_logical(lax.shift_right_logical(bits, half_width), half_width), F32)


def _attention_kernel(q_ref, k_ref, v_ref, u_ref, wdw_ref, bdw_ref, o_ref, conv_ref, kbd, vbd, sh_ref):
    conv_done = _causal_conv_slice(u_ref, wdw_ref, bdw_ref, conv_ref, sh_ref)
    n_q = q_ref.shape[0] // ATTN_TILE
    work = [i + 1 for i in range(n_q)]
    share_end = [len(conv_done) * sum(work[:i + 1]) // sum(work) for i in range(n_q)]

    t = ATTN_TILE
    hk = HEAD_SLOT
    seq = q_ref.shape[0]
    lane = lax.broadcasted_iota(jnp.int32, (hk, HEAD_SLOT), 1)
    zero = jnp.zeros((hk, HEAD_SLOT), BF16)
    for c in range(seq // hk):
        rows = slice(c * hk, (c + 1) * hk)
        top = slice(2 * c * hk, (2 * c + 1) * hk)
        bot = slice((2 * c + 1) * hk, (2 * c + 2) * hk)
        kbd[top, :HEAD_SLOT] = k_ref[rows, :HEAD_SLOT]
        kbd[top, HEAD_SLOT:] = zero
        kbd[bot, :HEAD_SLOT] = zero
        kbd[bot, HEAD_SLOT:] = k_ref[rows, HEAD_SLOT:]
        vp = v_ref[rows, :]
        vbd[top, :] = jnp.where(lane < V_DIM, vp, zero)
        vbd[bot, :] = jnp.where(lane >= V_DIM, vp, zero)

    qrow = lax.broadcasted_iota(jnp.int32, (t, hk), 0) // CHUNK
    kcol = lax.broadcasted_iota(jnp.int32, (t, hk), 1)
    diag_masks = [(kcol + h2 * hk) // CHUNK <= qrow for h2 in range(t // hk)]
    out_lane = lax.broadcasted_iota(jnp.int32, (t, HEAD_SLOT), 1)
    for i in range(seq // t):
        rows = slice(i * t, (i + 1) * t)
        n_tiles = 2 * (i + 1) * t // hk
        s = lax.dot_general(q_ref[rows, :], kbd[0:n_tiles * hk, :], (((1,), (1,)), ((), ())),
                            preferred_element_type=F32)
        tiles = [s[:, j * hk:(j + 1) * hk] for j in range(n_tiles)]
        conv_deps = conv_done[(share_end[i - 1] if i else 0):share_end[i]]
        if conv_deps:
            tiles[0] = tiles[0] + jnp.tile(_ordering_zero(conv_deps), (t // SUBLANES, 1))
        first_diag = n_tiles - 2 * (t // hk)
        for j in range(first_diag, n_tiles):
            tiles[j] = jnp.where(diag_masks[(j - first_diag) // 2], tiles[j], NEG_INF)
        probs = [None] * n_tiles
        inv_l = []
        for h in range(2):
            mine = range(h, n_tiles, 2)
            m = jnp.max(functools.reduce(jnp.maximum, [tiles[j] for j in mine]), axis=-1, keepdims=True)
            for j in mine:
                probs[j] = jnp.exp2(tiles[j] - m)
            l = jnp.sum(functools.reduce(jnp.add, [probs[j] for j in mine]), axis=-1, keepdims=True)
            inv_l.append(1.0 / l)
        p = jnp.concatenate(probs, axis=1).astype(BF16)
        o = _dot(p, vbd[0:n_tiles * hk, :])
        o_ref[rows, :] = (o * jnp.where(out_lane < V_DIM, inv_l[0], inv_l[1])).astype(BF16)


def _attention(q, k, v, u, w, b, t):
    hs2 = 2 * HEAD_SLOT
    assert CONV_W == (N_HEADS // 2) * LANES
    spec = pl.BlockSpec((t, hs2), lambda bi, p: (bi, p))
    lane_tile = pl.BlockSpec((t, LANES), lambda bi, p: (bi, p))
    wslice = lambda rows: pl.BlockSpec((rows, LANES), lambda bi, p: (0, p))
    return pl.pallas_call(
        _attention_kernel,
        grid=(b, N_HEADS // 2),
        in_specs=[spec, spec, lane_tile, lane_tile, wslice(w["w_dw"].shape[0]), wslice(1)],
        out_specs=[lane_tile, lane_tile],
        out_shape=[jax.ShapeDtypeStruct((b * t, N_HEADS * V_DIM), BF16),
                   jax.ShapeDtypeStruct((b * t, CONV_W), F32)],
        scratch_shapes=[pltpu.VMEM((2 * t, hs2), BF16), pltpu.VMEM((2 * t, 2 * V_DIM), BF16),
                        pltpu.VMEM((SUBLANES, t + CONV_HALO, LANES), F32)],
        compiler_params=_params("parallel", "parallel"),
        name="attention",
    )(q, k, v, u, w["w_dw"], w["b_dw"])


def _merge_kernel(attn_ref, conv_ref, gate_ref, x_ref, wo_ref, gln_ref, bln_ref, wco_ref, wout_ref, gffn_ref,
                  wrh_ref, wrl_ref, br_ref, x1_out, h2p_out, route_out):
    tm = x_ref.shape[0]
    half = D_MODEL // 2
    for sb in range(tm // SUB_ROWS):
        rows = slice(sb * SUB_ROWS, (sb + 1) * SUB_ROWS)
        conv = conv_ref[rows, :]
        mu = jnp.mean(conv, axis=-1, keepdims=True)
        cen = conv - mu
        var = jnp.mean(cen * cen, axis=-1, keepdims=True)
        ln = cen * lax.rsqrt(var + EPS) * gln_ref[...] + bln_ref[...]
        ub = (ln * _sigmoid(ln)).astype(BF16)
        y_b = _dot(ub, wco_ref[...])
        y_a = _dot(attn_ref[rows, :], wo_ref[...])
        merged = gate_ref[rows, :D_MODEL].astype(F32) * y_a + gate_ref[rows, D_MODEL:].astype(F32) * y_b
        x1 = x_ref[rows, :] + _dot(merged.astype(BF16), wout_ref[...])
        x1_out[rows, :] = x1
        h2 = _rms(x1, gffn_ref[...])
        _store_tile_rows(h2p_out, sb * (SUB_ROWS // SUBLANES), _pack_words(h2[:, :half], h2[:, half:]))
        h2_hi = h2.astype(BF16)
        h2_lo = (h2 - h2_hi.astype(F32)).astype(BF16)
        logits = (_dot(h2_hi, wrh_ref[...]) + _dot(h2_lo, wrh_ref[...]) + _dot(h2_hi, wrl_ref[...])
                  + br_ref[...])
        route_out[rows, :] = _route(logits)


def _route(logits):
    lane_i = lax.broadcasted_iota(jnp.int32, logits.shape, 1)
    lane = lane_i.astype(F32)
    big = float(LANES)
    is_g = (lane_i >= N_EXPERTS) & (lane_i < N_EXPERTS + N_GROUPS)
    gl = jnp.where(is_g, logits, NEG_INF)
    gmax = jnp.max(gl, axis=-1, keepdims=True)
    gidx = jnp.min(jnp.where(gl == gmax, lane - N_EXPERTS, big), axis=-1, keepdims=True)
    g_p = 1.0 / jnp.sum(jnp.exp(gl - gmax), axis=-1, keepdims=True)
    lane_group = (lane_i // EXPERTS_PER_GROUP).astype(F32)
    valid = (lane_i < N_EXPERTS) & (lane_group == gidx)
    el = jnp.where(valid, logits, NEG_INF)
    ee = jnp.exp(el - jnp.max(el, axis=-1, keepdims=True))
    within = jnp.where(valid, ee / jnp.sum(ee, axis=-1, keepdims=True), -1.0)
    p1 = jnp.max(within, axis=-1, keepdims=True)
    i1 = jnp.min(jnp.where(within == p1, lane, big), axis=-1, keepdims=True)
    within2 = jnp.where(lane == i1, -1.0, within)
    p2 = jnp.max(within2, axis=-1, keepdims=True)
    i2 = jnp.min(jnp.where(within2 == p2, lane, big), axis=-1, keepdims=True)
    psum = p1 + p2
    w1 = g_p * (p1 / psum)
    w2 = g_p * (p2 / psum)
    route = jnp.where(lane_i == 0, i1, 0.0)
    route = jnp.where(lane_i == 1, i2, route)
    route = jnp.where(lane_i == 2, w1, route)
    return jnp.where(lane_i == 3, w2, route)


def _merge(attn, conv, gates, x2, w, tm):
    n = x2.shape[0]
    row = lambda width: pl.BlockSpec((tm, width), lambda i: (i, 0))
    weights = [w["wo"], w["g_conv_ln"], w["b_conv_ln"], w["wco"], w["wout"],
               w["g_ffn"], w["wr_hi"], w["wr_lo"], w["br"]]
    tile_rows = (tm // SUBLANES,) + PACKED_TILE
    return pl.pallas_call(
        _merge_kernel,
        grid=(n // tm,),
        in_specs=[row(N_HEADS * V_DIM), row(CONV_W), row(2 * D_MODEL), row(D_MODEL)]
        + [_full(a.shape) for a in weights],
        out_specs=[row(D_MODEL), pl.BlockSpec(tile_rows, lambda i: (i, 0, 0, 0)), row(LANES)],
        out_shape=[jax.ShapeDtypeStruct((n, D_MODEL), F32),
                   jax.ShapeDtypeStruct((n // SUBLANES,) + PACKED_TILE, jnp.uint32),
                   jax.ShapeDtypeStruct((n, LANES), F32)],
        compiler_params=_params("parallel"),
        name="merge_router",
    )(attn, conv, gates, x2, *weights)


def _rank_kernel(route_ref, pos_out, meta_out, r1_sc, r2_sc):
    tb = route_ref.shape[0]
    lane = lax.broadcasted_iota(jnp.int32, (RANK_CHUNK, LANES), 1).astype(F32)
    rr = lax.broadcasted_iota(jnp.int32, (RANK_CHUNK, RANK_CHUNK), 0)
    cc = lax.broadcasted_iota(jnp.int32, (RANK_CHUNK, RANK_CHUNK), 1)
    lower = jnp.where(rr > cc, 1.0, 0.0).astype(BF16)
    run = jnp.zeros((1, LANES), F32)
    for choice, sc in ((0, r1_sc), (1, r2_sc)):
        for ci in range(tb // RANK_CHUNK):
            rows = slice(ci * RANK_CHUNK, (ci + 1) * RANK_CHUNK)
            oh = jnp.where(lane == route_ref[rows, choice:choice + 1], 1.0, 0.0)
            sc[rows, :] = run + _dot(lower, oh.astype(BF16))
            run = run + jnp.sum(oh, axis=0, keepdims=True)
    cnt = run
    cnt8 = jnp.ceil(cnt * (1.0 / SUBLANES)) * SUBLANES
    ur = lax.broadcasted_iota(jnp.int32, (LANES, LANES), 0)
    uc = lax.broadcasted_iota(jnp.int32, (LANES, LANES), 1)
    upper = jnp.where(ur < uc, 1.0, 0.0)
    off = _dot(jnp.broadcast_to(cnt8, (SUBLANES, LANES)), upper, precision=lax.Precision.HIGHEST)[0:1, :]
    lane_t = lax.broadcasted_iota(jnp.int32, (tb, LANES), 1)
    lane_tf = lane_t.astype(F32)
    p1 = jnp.sum(jnp.where(lane_tf == route_ref[:, 0:1], off + r1_sc[...], 0.0), axis=-1, keepdims=True)
    p2 = jnp.sum(jnp.where(lane_tf == route_ref[:, 1:2], off + r2_sc[...], 0.0), axis=-1, keepdims=True)
    q1 = p1 + (SUBLANES * ROW_PIECES - SUBLANES) * jnp.floor(p1 * (1.0 / SUBLANES))
    q2 = p2 + (SUBLANES * ROW_PIECES - SUBLANES) * jnp.floor(p2 * (1.0 / SUBLANES))
    pos = jnp.where(lane_t == 0, q1, jnp.where(lane_t == 1, q2, 0.0))
    pos_out[...] = jnp.transpose(pos)[0:SUBLANES, :].astype(jnp.int32)
    row8 = lax.broadcasted_iota(jnp.int32, (SUBLANES, LANES), 0)
    meta = jnp.where(row8 == 0, jnp.broadcast_to(off, (SUBLANES, LANES)),
                     jnp.where(row8 == 1, jnp.broadcast_to(cnt, (SUBLANES, LANES)), 0.0))
    meta_out[...] = meta.astype(jnp.int32)


def _rank(route, b, t):
    return pl.pallas_call(
        _rank_kernel,
        grid=(b,),
        in_specs=[pl.BlockSpec((t, LANES), lambda i: (i, 0))],
        out_specs=[pl.BlockSpec((SUBLANES, t), lambda i: (i, 0)), pl.BlockSpec((SUBLANES, LANES), lambda i: (i, 0))],
        out_shape=[jax.ShapeDtypeStruct((b * SUBLANES, t), jnp.int32),
                   jax.ShapeDtypeStruct((b * SUBLANES, LANES), jnp.int32)],
        scratch_shapes=[pltpu.VMEM((t, LANES), F32), pltpu.VMEM((t, LANES), F32)],
        compiler_params=_params("parallel"),
        name="rank",
    )(route)


def _sorted_rows(t):
    return 2 * t + N_EXPERTS * SUBLANES + ROW_TILE


GROUP_ROWS = SUBLANES * ROW_PIECES


def _load_flat_tile(ref, base, rows):
    cols = []
    for c in range(ROW_PIECES):
        cols.append(jnp.concatenate(
            [ref[pl.ds(base + g * GROUP_ROWS + c * SUBLANES, SUBLANES), :] for g in range(rows // SUBLANES)], axis=0))
    return jnp.concatenate(cols, axis=1)


def _store_flat_tile(ref, base, words):
    for c in range(ROW_PIECES):
        for g in range(words.shape[0] // SUBLANES):
            ref[pl.ds(base + g * GROUP_ROWS + c * SUBLANES, SUBLANES), :] = (
                words[g * SUBLANES:(g + 1) * SUBLANES, c * LANES:(c + 1) * LANES])


def _moe_kernel(n_seq, off_ref, cnt_ref, qd1_ref, qd2_ref, qc1_ref, qc2_ref, h2p_ref, wg_ref, wu_ref, wd_ref,
                x1_ref, route_ref, out_ref, xs0, xs1, ys0, ys1, g1, g2):
    o = pl.program_id(0)
    s = pl.program_id(1)
    ts = x1_ref.shape[0]
    half = D_MODEL // 2
    seq_e = jnp.clip(o - 1, 0, n_seq - 1)

    @pl.when((o == 0) & (s == 0))
    def _init():
        for ref in (xs0, xs1, ys0, ys1):
            ref[...] = jnp.zeros(ref.shape, ref.dtype)

    def expert_tile(xs_e, ys_e, k, base, i):
        start = pl.multiple_of(jnp.where(i >= 0, base + i * (ROW_TILE * ROW_PIECES),
                                         xs_e.shape[0] - ROW_TILE * ROW_PIECES), GROUP_ROWS)
        lo, hi = _unpack_words(_load_flat_tile(xs_e, start, ROW_TILE))
        xb = jnp.concatenate([lo.astype(BF16), hi.astype(BF16)], axis=1)
        a = _dot(xb, wg_ref[k])
        mid = (a * _sigmoid(a) * _dot(xb, wu_ref[k])).astype(BF16)
        y = _dot(mid, wd_ref[k])
        _store_flat_tile(ys_e, start, _pack_words(y[:, :half], y[:, half:]))

    def body(xs_d, xs_e, ys_e, ys_c, with_experts):
        @pl.when(s == 0)
        def _clear():
            xs_d[...] = jnp.zeros(xs_d.shape, xs_d.dtype)

        bases, tiles = [], []
        if with_experts:
            for k in range(EXPERTS_PER_STEP):
                idx = seq_e * N_EXPERTS + s * EXPERTS_PER_STEP + k
                bases.append(off_ref[idx] * ROW_PIECES)
                tiles.append(lax.shift_right_logical(cnt_ref[idx] + (ROW_TILE - 1), ROW_TILE.bit_length() - 1))
            for k in range(EXPERTS_PER_STEP):
                expert_tile(xs_e, ys_e, k, bases[k], tiles[k] - 1)

        t0 = s * ts
        for g in range(ts // SUBLANES):
            for r in range(SUBLANES):
                row = h2p_ref[_row_ds(g * GROUP_ROWS + r), :]
                tok = t0 + g * SUBLANES + r
                xs_d[_row_ds(qd1_ref[tok]), :] = row
                xs_d[_row_ds(qd2_ref[tok]), :] = row

        for g in range(ts // SUBLANES):
            for r in range(SUBLANES):
                tok = t0 + g * SUBLANES + r
                g1[_row_ds(g * GROUP_ROWS + r), :] = ys_c[_row_ds(qc1_ref[tok]), :]
                g2[_row_ds(g * GROUP_ROWS + r), :] = ys_c[_row_ds(qc2_ref[tok]), :]
        w1 = route_ref[:, 2:3]
        w2 = route_ref[:, 3:4]
        for c in range(ROW_PIECES):
            tile_of = lambda ref: jnp.concatenate(
                [ref[g * GROUP_ROWS + c * SUBLANES:g * GROUP_ROWS + (c + 1) * SUBLANES, :]
                 for g in range(ts // SUBLANES)], axis=0)
            lo1, hi1 = _unpack_words(tile_of(g1))
            lo2, hi2 = _unpack_words(tile_of(g2))
            cols = slice(c * LANES, (c + 1) * LANES)
            cols_hi = slice(half + c * LANES, half + (c + 1) * LANES)
            out_ref[:, cols] = x1_ref[:, cols] + (w1 * lo1 + w2 * lo2)
            out_ref[:, cols_hi] = x1_ref[:, cols_hi] + (w1 * hi1 + w2 * hi2)

        def more(j, carry):
            for k in range(EXPERTS_PER_STEP):
                expert_tile(xs_e, ys_e, k, bases[k], tiles[k] - 1 - j)
            return carry

        if with_experts:
            lax.fori_loop(1, functools.reduce(jnp.maximum, tiles), more, 0)

    has_experts = (o >= 1) & (o <= n_seq)
    for parity, bufs in ((0, (xs0, xs1, ys1, ys0)), (1, (xs1, xs0, ys0, ys1))):
        for flag in (True, False):
            pl.when((o % 2 == parity) & (has_experts == flag))(functools.partial(body, *bufs, flag))


def _moe(off, cnt, q1, q2, h2p_flat, x1, route, w, b, t):
    flat_rows = _sorted_rows(t) * ROW_PIECES
    n_steps = N_EXPERTS // EXPERTS_PER_STEP
    assert t % (n_steps * SUBLANES) == 0
    ts = t // n_steps
    seq_d = lambda o: jnp.minimum(o, b - 1)
    seq_c = lambda o: jnp.clip(o - 2, 0, b - 1)
    chunk = lambda o, s, *_: (jnp.where(o < 2, 0, seq_c(o) * n_steps + s), 0)
    wmap = lambda o, s, *_: (s, 0, 0)
    smem = lambda f: pl.BlockSpec((t,), lambda o, s, *_: (f(o),), memory_space=pltpu.SMEM)
    grid_spec = pltpu.PrefetchScalarGridSpec(
        num_scalar_prefetch=2,
        grid=(b + 2, n_steps),
        in_specs=[smem(seq_d), smem(seq_d), smem(seq_c), smem(seq_c),
                  pl.BlockSpec((ts * ROW_PIECES, LANES), lambda o, s, *_: (seq_d(o) * n_steps + s, 0)),
                  pl.BlockSpec((EXPERTS_PER_STEP, D_MODEL, EXPERT_FF), wmap),
                  pl.BlockSpec((EXPERTS_PER_STEP, D_MODEL, EXPERT_FF), wmap),
                  pl.BlockSpec((EXPERTS_PER_STEP, EXPERT_FF, D_MODEL), wmap),
                  pl.BlockSpec((ts, D_MODEL), chunk),
                  pl.BlockSpec((ts, LANES), chunk)],
        out_specs=pl.BlockSpec((ts, D_MODEL), chunk),
        scratch_shapes=[pltpu.VMEM((flat_rows, LANES), jnp.uint32)] * 4
        + [pltpu.VMEM((ts * ROW_PIECES, LANES), jnp.uint32)] * 2,
    )
    return pl.pallas_call(
        functools.partial(_moe_kernel, b),
        grid_spec=grid_spec,
        out_shape=jax.ShapeDtypeStruct((b * t, D_MODEL), F32),
        compiler_params=_params("arbitrary", "arbitrary"),
        name="moe",
    )(off, cnt, q1, q2, q1, q2, h2p_flat, w["wg_e"], w["wu_e"], w["wd"], x1, route)


def _head_slots(wmat, width):
    k = wmat.shape[0]
    w3 = wmat.reshape(k, N_HEADS, width)
    return jnp.pad(w3, ((0, 0), (0, 0), (0, HEAD_SLOT - width))).reshape(k, N_HEADS * HEAD_SLOT)


def _gain3(g):
    gp = jnp.pad(g.astype(F32), (0, HEAD_SLOT - QK_DIM))
    return jnp.stack([gp, jnp.roll(gp, ROPE_HALF), jnp.roll(gp, -ROPE_HALF)])


def _prepare_weights(g_mix, w_in, g_q_lora, w_uq, g_kv_lora, w_ukv, g_qk_q, g_qk_k, w_o_mla, w_dw, b_dw,
                     g_conv_ln, b_conv_ln, w_conv_out, b_gates, w_out, g_ffn, w_group_router, b_group_router,
                     w_expert_router, b_expert_router, w_e_gate, w_e_up, w_e_down):
    row = lambda a: a.astype(F32).reshape(1, -1)
    s0, s1, s2, s3 = Q_LORA, Q_LORA + KV_LORA, Q_LORA + KV_LORA + ROPE, Q_LORA + KV_LORA + ROPE + 2 * CONV_W
    w = {}
    w["g_mix"] = row(g_mix)
    w["wa"] = jnp.concatenate([w_in[:, :s1], jnp.pad(w_in[:, s1:s2], ((0, 0), (0, LANES - ROPE)))], axis=1).astype(BF16)
    w["wc"] = w_in[:, s2:s3].astype(BF16)
    w["wg"] = w_in[:, s3:].astype(BF16)
    w["b_gates"] = row(b_gates)
    w["g_q_lora"] = row(g_q_lora)
    w["wuq"] = _head_slots(w_uq, QK_DIM).astype(BF16)
    w["g_kv_lora"] = row(g_kv_lora)
    kv3 = w_ukv.reshape(KV_LORA, N_HEADS, NOPE + V_DIM)
    wk_nope = _head_slots(kv3[:, :, :NOPE].reshape(KV_LORA, N_HEADS * NOPE), NOPE)
    w["wv"] = kv3[:, :, NOPE:].reshape(KV_LORA, N_HEADS * V_DIM).astype(BF16)
    src = jnp.arange(LANES)[:, None]
    dst = jnp.arange(N_HEADS * HEAD_SLOT)[None, :]
    rope_copy = ((src < ROPE) & ((dst % HEAD_SLOT) == (src + NOPE))).astype(F32)
    w["wk"] = jnp.concatenate([wk_nope, rope_copy], axis=0).astype(BF16)
    w["gq3"] = _gain3(g_qk_q)
    w["gk3"] = _gain3(g_qk_k)
    w["wo"] = w_o_mla.astype(BF16)
    w["w_dw"] = jnp.pad(w_dw.astype(F32), ((0, 32 - CONV_K), (0, 0)))
    w["b_dw"] = row(b_dw)
    w["g_conv_ln"] = row(g_conv_ln)
    w["b_conv_ln"] = row(b_conv_ln)
    w["wco"] = w_conv_out.astype(BF16)
    w["wout"] = w_out.astype(BF16)
    w["g_ffn"] = row(g_ffn)
    pad_r = LANES - N_EXPERTS - N_GROUPS
    wr = jnp.concatenate([w_expert_router, w_group_router, jnp.zeros((D_MODEL, pad_r), F32)], axis=1).astype(F32)
    w["wr_hi"] = wr.astype(BF16)
    w["wr_lo"] = (wr - w["wr_hi"].astype(F32)).astype(BF16)
    w["br"] = jnp.concatenate([b_expert_router, b_group_router, jnp.zeros((pad_r,), F32)]).astype(F32).reshape(1, LANES)
    w["wg_e"] = w_e_gate.astype(BF16)
    w["wu_e"] = w_e_up.astype(BF16)
    w["wd"] = w_e_down.astype(BF16)
    return w


def kernel(x, positions, g_mix, w_in, g_q_lora, w_uq, g_kv_lora, w_ukv, g_qk_q, g_qk_k, w_o_mla, w_dw, b_dw,
           g_conv_ln, b_conv_ln, w_conv_out, b_gates, w_out, g_ffn, w_group_router, b_group_router,
           w_expert_router, b_expert_router, w_e_gate, w_e_up, w_e_down):
    b, t, d = x.shape
    assert d == D_MODEL and g_mix.shape[0] == 1
    assert t % ATTN_TILE == 0 and t % RANK_CHUNK == 0 and t % COMBINE_TILE == 0
    tm = min(TOKEN_BLOCK, t)
    assert t % tm == 0 and t % CONV_CHUNK == 0
    n = b * t
    w = _prepare_weights(g_mix[0], w_in[0], g_q_lora[0], w_uq[0], g_kv_lora[0], w_ukv[0], g_qk_q[0], g_qk_k[0],
                         w_o_mla[0], w_dw[0], b_dw[0], g_conv_ln[0], b_conv_ln[0], w_conv_out[0], b_gates[0],
                         w_out[0], g_ffn[0], w_group_router[0], b_group_router[0], w_expert_router[0],
                         b_expert_router[0], w_e_gate[0], w_e_up[0], w_e_down[0])
    x2 = x.reshape(n, d)
    cos, sin = _rope_tables(positions)
    q, k, v, u, gates = _pre_attention(x2, cos, sin, w, tm)
    attn, conv = _attention(q, k, v, u, w, b, t)
    x1, h2p, route = _merge(attn, conv, gates, x2, w, tm)
    pos, meta = _rank(route, b, t)
    pos3 = pos.reshape(b, SUBLANES, t)
    q1 = pos3[:, 0, :].reshape(n)
    q2 = pos3[:, 1, :].reshape(n)
    meta3 = meta.reshape(b, SUBLANES, LANES)
    off = meta3[:, 0, :N_EXPERTS].reshape(b * N_EXPERTS)
    cnt = meta3[:, 1, :N_EXPERTS].reshape(b * N_EXPERTS)
    out = _moe(off, cnt, q1, q2, h2p.reshape(n * ROW_PIECES, LANES), x1, route, w, b, t)
    return out.reshape(b, t, d)
```

```python
import functools

import jax
import jax.numpy as jnp
import numpy as np
from jax import lax
from jax.experimental import pallas as pl
from jax.experimental.pallas import tpu as pltpu

D_MODEL = 1024
N_HEADS = 8
NOPE = 64
ROPE = 32
ROPE_HALF = ROPE // 2
QK_DIM = NOPE + ROPE
V_DIM = 64
Q_LORA = 256
KV_LORA = 128
CONV_W = 512
CONV_K = 31
N_GROUPS = 4
EXPERTS_PER_GROUP = 8
N_EXPERTS = N_GROUPS * EXPERTS_PER_GROUP
EXPERT_FF = 256
EPS = 1e-6
ROPE_THETA = 10000.0
CHUNK = 64

LANES = 128
SUBLANES = 8
HEAD_SLOT = LANES
VMEM_LIMIT_BYTES = 56 * 1024 * 1024

TOKEN_BLOCK = 1024
PROJ_BLOCK = 1024
ATTN_TILE = 256
CONV_HALO = 32
ROW_TILE = 256
EXPERTS_PER_STEP = 4
COMBINE_TILE = 256
RANK_CHUNK = 256
SUB_ROWS = 512
CONV_CHUNK = 16

F32 = jnp.float32
BF16 = jnp.bfloat16
NEG_INF = float("-inf")
LOG2E = 1.4426950408889634


def _dot(a, b, **kw):
    return jnp.dot(a, b, preferred_element_type=F32, **kw)


def _rms(x, g):
    return x * lax.rsqrt(jnp.mean(x * x, axis=-1, keepdims=True) + EPS) * g


def _sigmoid(x):
    return 1.0 / (1.0 + jnp.exp(-x))


def _pack_words(lo, hi):
    return pltpu.pack_elementwise([lo, hi], packed_dtype=BF16)


def _unpack_words(wd):
    lo = pltpu.unpack_elementwise(wd, index=0, packed_dtype=BF16, unpacked_dtype=F32)
    hi = pltpu.unpack_elementwise(wd, index=1, packed_dtype=BF16, unpacked_dtype=F32)
    return lo, hi


PACKED_TILE = (D_MODEL // 2 // LANES, SUBLANES, LANES)
ROW_PIECES = PACKED_TILE[0]


def _store_tile_rows(ref, g0, words, lead=()):
    groups = words.shape[0] // SUBLANES
    for c in range(ROW_PIECES):
        piece = words[:, c * LANES:(c + 1) * LANES].reshape(groups, SUBLANES, LANES)
        ref[lead + (pl.ds(g0, groups), c)] = piece


def _load_tile_rows(ref, g0, rows, lead=()):
    groups = rows // SUBLANES
    pieces = [ref[lead + (pl.ds(g0, groups), c)].reshape(rows, LANES) for c in range(ROW_PIECES)]
    return jnp.concatenate(pieces, axis=1)


def _row_ds(q):
    return pl.ds(q, ROW_PIECES, stride=SUBLANES)


def _params(*sem):
    return pltpu.CompilerParams(dimension_semantics=sem, vmem_limit_bytes=VMEM_LIMIT_BYTES)


def _full(shape):
    nd = len(shape)
    return pl.BlockSpec(shape, lambda *_: (0,) * nd)


def _rope_table_kernel(pos_ref, freq_ref, cos_ref, sin_ref):
    ang = pos_ref[...].astype(F32) * freq_ref[...]
    cos_ref[...] = jnp.cos(ang)
    sin_ref[...] = jnp.sin(ang)


def _rope_tables(positions):
    n = positions.size
    per_row = LANES // ROPE_HALF
    rows = n // per_row
    pos_rep = jnp.repeat(positions.reshape(rows, per_row), ROPE_HALF, axis=1)
    inv_freq = ROPE_THETA ** (-jnp.arange(ROPE_HALF, dtype=F32) / ROPE_HALF)
    freq = jnp.tile(inv_freq, per_row).reshape(1, LANES)
    rb = min(rows, 512)
    cos, sin = pl.pallas_call(
        _rope_table_kernel,
        grid=(rows // rb,),
        in_specs=[pl.BlockSpec((rb, LANES), lambda i: (i, 0)), _full((1, LANES))],
        out_specs=[pl.BlockSpec((rb, LANES), lambda i: (i, 0))] * 2,
        out_shape=[jax.ShapeDtypeStruct((rows, LANES), F32)] * 2,
        compiler_params=_params("parallel"),
        name="rope_tables",
    )(pos_rep, freq)
    return cos.reshape(n, ROPE_HALF), sin.reshape(n, ROPE_HALF)


def _pre_attention_kernel(x_ref, cos_ref, sin_ref, gmix_ref, wa_ref, wc_ref, wg_ref, bg_ref,
                          gql_ref, wuq_ref, gkv_ref, wk_ref, wv_ref, gq_ref, gk_ref,
                          q_out, k_out, v_out, u_out, gate_out):
    tm = x_ref.shape[0]
    hb = _rms(x_ref[...], gmix_ref[...]).astype(BF16)

    a = _dot(hb, wa_ref[...])
    qdn = _rms(a[:, :Q_LORA], gql_ref[...]).astype(BF16)
    kvn = _rms(a[:, Q_LORA:Q_LORA + KV_LORA], gkv_ref[...]).astype(BF16)
    kr = a[:, Q_LORA + KV_LORA:].astype(BF16)
    qf = _dot(qdn, wuq_ref[...])
    kf = _dot(jnp.concatenate([kvn, kr], axis=1), wk_ref[...])
    v_out[...] = _dot(kvn, wv_ref[...]).astype(BF16)

    cos = cos_ref[...]
    sin = sin_ref[...]
    z = lambda w: jnp.zeros((tm, w), F32)
    c_tab = jnp.concatenate([jnp.ones((tm, NOPE), F32), cos, cos, z(HEAD_SLOT - QK_DIM)], axis=-1)
    s_up = jnp.concatenate([z(NOPE + ROPE_HALF), sin, z(HEAD_SLOT - QK_DIM)], axis=-1)
    s_dn = jnp.concatenate([z(NOPE), -sin, z(HEAD_SLOT - NOPE - ROPE_HALF)], axis=-1)

    def norm_rope(f, g_ref, out, scale):
        ta = c_tab * g_ref[0:1, :]
        tu = s_up * g_ref[1:2, :]
        td = s_dn * g_ref[2:3, :]
        for h in range(N_HEADS):
            xs = f[:, h * HEAD_SLOT:(h + 1) * HEAD_SLOT]
            ss = jnp.sum(xs * xs, axis=-1, keepdims=True)
            s = lax.rsqrt(ss * (1.0 / QK_DIM) + EPS) * scale
            y = xs * ta + pltpu.roll(xs, ROPE_HALF, 1) * tu + pltpu.roll(xs, HEAD_SLOT - ROPE_HALF, 1) * td
            out[:, h * HEAD_SLOT:(h + 1) * HEAD_SLOT] = (y * s).astype(BF16)

    norm_rope(qf, gq_ref, q_out, LOG2E * QK_DIM ** -0.5)
    norm_rope(kf, gk_ref, k_out, 1.0)

    c = _dot(hb, wc_ref[...])
    u_out[...] = c[:, :CONV_W] * _sigmoid(c[:, CONV_W:])
    gate_out[...] = _sigmoid(_dot(hb, wg_ref[...]) + bg_ref[...]).astype(BF16)


def _pre_attention(x2, cos, sin, w, tm):
    n = x2.shape[0]
    row = lambda width: pl.BlockSpec((tm, width), lambda i: (i, 0))
    hs = N_HEADS * HEAD_SLOT
    weights = [w["g_mix"], w["wa"], w["wc"], w["wg"], w["b_gates"], w["g_q_lora"], w["wuq"],
               w["g_kv_lora"], w["wk"], w["wv"], w["gq3"], w["gk3"]]
    return pl.pallas_call(
        _pre_attention_kernel,
        grid=(n // tm,),
        in_specs=[row(D_MODEL), row(ROPE_HALF), row(ROPE_HALF)] + [_full(a.shape) for a in weights],
        out_specs=[row(hs), row(hs), row(N_HEADS * V_DIM), row(CONV_W), row(2 * D_MODEL)],
        out_shape=[jax.ShapeDtypeStruct((n, hs), BF16), jax.ShapeDtypeStruct((n, hs), BF16),
                   jax.ShapeDtypeStruct((n, N_HEADS * V_DIM), BF16), jax.ShapeDtypeStruct((n, CONV_W), F32),
                   jax.ShapeDtypeStruct((n, 2 * D_MODEL), BF16)],
        compiler_params=_params("parallel"),
        name="pre_attention",
    )(x2, cos, sin, *weights)


def _causal_conv_slice(u_ref, wdw_ref, bdw_ref, conv_ref, sh_ref):
    seq = u_ref.shape[0]
    sh_ref[0, 0:CONV_HALO, :] = jnp.zeros((CONV_HALO, LANES), F32)
    sh_ref[0, CONV_HALO:, :] = u_ref[...]
    span = seq + CONV_HALO - SUBLANES
    for s in range(1, SUBLANES):
        sh_ref[s, 0:span, :] = sh_ref[0, s:s + span, :]
    base = CONV_HALO - (CONV_K - 1)
    done = []
    for ck in range(seq // CONV_CHUNK):
        c0 = ck * CONV_CHUNK
        acc = jnp.broadcast_to(bdw_ref[...], (CONV_CHUNK, LANES))
        for j in range(CONV_K):
            s, a = (base + j) % SUBLANES, (base + j) // SUBLANES * SUBLANES
            acc = acc + sh_ref[s, c0 + a:c0 + a + CONV_CHUNK, :] * wdw_ref[j:j + 1, :]
        conv_ref[c0:c0 + CONV_CHUNK, :] = acc
        done.append(acc[0:SUBLANES, :])
    return done


def _ordering_zero(values):
    bits = functools.reduce(jnp.bitwise_or, [lax.bitcast_convert_type(v, jnp.uint32) for v in values])
    half_width = jnp.uint32(16)
    return lax.bitcast_convert_type(
        lax.shift_right_logical(lax.shift_right_logical(bits, half_width), half_width), F32)


def _attention_kernel(q_ref, k_ref, v_ref, u_ref, wdw_ref, bdw_ref, o_ref, conv_ref, kbd, vbd, sh_ref):
    conv_done = _causal_conv_slice(u_ref, wdw_ref, bdw_ref, conv_ref, sh_ref)
    n_q = q_ref.shape[0] // ATTN_TILE
    work = [i + 1 for i in range(n_q)]
    share_end = [len(conv_done) * sum(work[:i + 1]) // sum(work) for i in range(n_q)]

    t = ATTN_TILE
    hk = HEAD_SLOT
    seq = q_ref.shape[0]
    lane = lax.broadcasted_iota(jnp.int32, (hk, HEAD_SLOT), 1)
    zero = jnp.zeros((hk, HEAD_SLOT), BF16)
    for c in range(seq // hk):
        rows = slice(c * hk, (c + 1) * hk)
        top = slice(2 * c * hk, (2 * c + 1) * hk)
        bot = slice((2 * c + 1) * hk, (2 * c + 2) * hk)
        kbd[top, :HEAD_SLOT] = k_ref[rows, :HEAD_SLOT]
        kbd[top, HEAD_SLOT:] = zero
        kbd[bot, :HEAD_SLOT] = zero
        kbd[bot, HEAD_SLOT:] = k_ref[rows, HEAD_SLOT:]
        vp = v_ref[rows, :]
        vbd[top, :] = jnp.where(lane < V_DIM, vp, zero)
        vbd[bot, :] = jnp.where(lane >= V_DIM, vp, zero)

    qrow = lax.broadcasted_iota(jnp.int32, (t, hk), 0) // CHUNK
    kcol = lax.broadcasted_iota(jnp.int32, (t, hk), 1)
    diag_masks = [(kcol + h2 * hk) // CHUNK <= qrow for h2 in range(t // hk)]
    out_lane = lax.broadcasted_iota(jnp.int32, (t, HEAD_SLOT), 1)
    for i in range(seq // t):
        rows = slice(i * t, (i + 1) * t)
        n_tiles = 2 * (i + 1) * t // hk
        s = lax.dot_general(q_ref[rows, :], kbd[0:n_tiles * hk, :], (((1,), (1,)), ((), ())),
                            preferred_element_type=F32)
        tiles = [s[:, j * hk:(j + 1) * hk] for j in range(n_tiles)]
        conv_deps = conv_done[(share_end[i - 1] if i else 0):share_end[i]]
        if conv_deps:
            tiles[0] = tiles[0] + jnp.tile(_ordering_zero(conv_deps), (t // SUBLANES, 1))
        first_diag = n_tiles - 2 * (t // hk)
        for j in range(first_diag, n_tiles):
            tiles[j] = jnp.where(diag_masks[(j - first_diag) // 2], tiles[j], NEG_INF)
        probs = [None] * n_tiles
        inv_l = []
        for h in range(2):
            mine = range(h, n_tiles, 2)
            m = jnp.max(functools.reduce(jnp.maximum, [tiles[j] for j in mine]), axis=-1, keepdims=True)
            for j in mine:
                probs[j] = jnp.exp2(tiles[j] - m)
            l = jnp.sum(functools.reduce(jnp.add, [probs[j] for j in mine]), axis=-1, keepdims=True)
            inv_l.append(1.0 / l)
        p = jnp.concatenate(probs, axis=1).astype(BF16)
        o = _dot(p, vbd[0:n_tiles * hk, :])
        o_ref[rows, :] = (o * jnp.where(out_lane < V_DIM, inv_l[0], inv_l[1])).astype(BF16)


def _attention(q, k, v, u, w, b, t):
    hs2 = 2 * HEAD_SLOT
    assert CONV_W == (N_HEADS // 2) * LANES
    spec = pl.BlockSpec((t, hs2), lambda bi, p: (bi, p))
    lane_tile = pl.BlockSpec((t, LANES), lambda bi, p: (bi, p))
    wslice = lambda rows: pl.BlockSpec((rows, LANES), lambda bi, p: (0, p))
    return pl.pallas_call(
        _attention_kernel,
        grid=(b, N_HEADS // 2),
        in_specs=[spec, spec, lane_tile, lane_tile, wslice(w["w_dw"].shape[0]), wslice(1)],
        out_specs=[lane_tile, lane_tile],
        out_shape=[jax.ShapeDtypeStruct((b * t, N_HEADS * V_DIM), BF16),
                   jax.ShapeDtypeStruct((b * t, CONV_W), F32)],
        scratch_shapes=[pltpu.VMEM((2 * t, hs2), BF16), pltpu.VMEM((2 * t, 2 * V_DIM), BF16),
                        pltpu.VMEM((SUBLANES, t + CONV_HALO, LANES), F32)],
        compiler_params=_params("parallel", "parallel"),
        name="attention",
    )(q, k, v, u, w["w_dw"], w["b_dw"])


def _merge_kernel(attn_ref, conv_ref, gate_ref, x_ref, wo_ref, gln_ref, bln_ref, wco_ref, wout_ref, gffn_ref,
                  wrh_ref, wrl_ref, br_ref, x1_out, h2p_out, route_out):
    tm = x_ref.shape[0]
    half = D_MODEL // 2
    for sb in range(tm // SUB_ROWS):
        rows = slice(sb * SUB_ROWS, (sb + 1) * SUB_ROWS)
        conv = conv_ref[rows, :]
        mu = jnp.mean(conv, axis=-1, keepdims=True)
        cen = conv - mu
        var = jnp.mean(cen * cen, axis=-1, keepdims=True)
        ln = cen * lax.rsqrt(var + EPS) * gln_ref[...] + bln_ref[...]
        ub = (ln * _sigmoid(ln)).astype(BF16)
        y_b = _dot(ub, wco_ref[...])
        y_a = _dot(attn_ref[rows, :], wo_ref[...])
        merged = gate_ref[rows, :D_MODEL].astype(F32) * y_a + gate_ref[rows, D_MODEL:].astype(F32) * y_b
        x1 = x_ref[rows, :] + _dot(merged.astype(BF16), wout_ref[...])
        x1_out[rows, :] = x1
        h2 = _rms(x1, gffn_ref[...])
        _store_tile_rows(h2p_out, sb * (SUB_ROWS // SUBLANES), _pack_words(h2[:, :half], h2[:, half:]))
        h2_hi = h2.astype(BF16)
        h2_lo = (h2 - h2_hi.astype(F32)).astype(BF16)
        logits = (_dot(h2_hi, wrh_ref[...]) + _dot(h2_lo, wrh_ref[...]) + _dot(h2_hi, wrl_ref[...])
                  + br_ref[...])
        route_out[rows, :] = _route(logits)


def _route(logits):
    lane_i = lax.broadcasted_iota(jnp.int32, logits.shape, 1)
    lane = lane_i.astype(F32)
    big = float(LANES)
    is_g = (lane_i >= N_EXPERTS) & (lane_i < N_EXPERTS + N_GROUPS)
    gl = jnp.where(is_g, logits, NEG_INF)
    gmax = jnp.max(gl, axis=-1, keepdims=True)
    gidx = jnp.min(jnp.where(gl == gmax, lane - N_EXPERTS, big), axis=-1, keepdims=True)
    g_p = 1.0 / jnp.sum(jnp.exp(gl - gmax), axis=-1, keepdims=True)
    lane_group = (lane_i // EXPERTS_PER_GROUP).astype(F32)
    valid = (lane_i < N_EXPERTS) & (lane_group == gidx)
    el = jnp.where(valid, logits, NEG_INF)
    ee = jnp.exp(el - jnp.max(el, axis=-1, keepdims=True))
    within = jnp.where(valid, ee / jnp.sum(ee, axis=-1, keepdims=True), -1.0)
    p1 = jnp.max(within, axis=-1, keepdims=True)
    i1 = jnp.min(jnp.where(within == p1, lane, big), axis=-1, keepdims=True)
    within2 = jnp.where(lane == i1, -1.0, within)
    p2 = jnp.max(within2, axis=-1, keepdims=True)
    i2 = jnp.min(jnp.where(within2 == p2, lane, big), axis=-1, keepdims=True)
    psum = p1 + p2
    w1 = g_p * (p1 / psum)
    w2 = g_p * (p2 / psum)
    route = jnp.where(lane_i == 0, i1, 0.0)
    route = jnp.where(lane_i == 1, i2, route)
    route = jnp.where(lane_i == 2, w1, route)
    return jnp.where(lane_i == 3, w2, route)


def _merge(attn, conv, gates, x2, w, tm):
    n = x2.shape[0]
    row = lambda width: pl.BlockSpec((tm, width), lambda i: (i, 0))
    weights = [w["wo"], w["g_conv_ln"], w["b_conv_ln"], w["wco"], w["wout"],
               w["g_ffn"], w["wr_hi"], w["wr_lo"], w["br"]]
    tile_rows = (tm // SUBLANES,) + PACKED_TILE
    return pl.pallas_call(
        _merge_kernel,
        grid=(n // tm,),
        in_specs=[row(N_HEADS * V_DIM), row(CONV_W), row(2 * D_MODEL), row(D_MODEL)]
        + [_full(a.shape) for a in weights],
        out_specs=[row(D_MODEL), pl.BlockSpec(tile_rows, lambda i: (i, 0, 0, 0)), row(LANES)],
        out_shape=[jax.ShapeDtypeStruct((n, D_MODEL), F32),
                   jax.ShapeDtypeStruct((n // SUBLANES,) + PACKED_TILE, jnp.uint32),
                   jax.ShapeDtypeStruct((n, LANES), F32)],
        compiler_params=_params("parallel"),
        name="merge_router",
    )(attn, conv, gates, x2, *weights)


def _rank_kernel(route_ref, pos_out, meta_out, r1_sc, r2_sc):
    tb = route_ref.shape[0]
    lane = lax.broadcasted_iota(jnp.int32, (RANK_CHUNK, LANES), 1).astype(F32)
    rr = lax.broadcasted_iota(jnp.int32, (RANK_CHUNK, RANK_CHUNK), 0)
    cc = lax.broadcasted_iota(jnp.int32, (RANK_CHUNK, RANK_CHUNK), 1)
    lower = jnp.where(rr > cc, 1.0, 0.0).astype(BF16)
    run = jnp.zeros((1, LANES), F32)
    for choice, sc in ((0, r1_sc), (1, r2_sc)):
        for ci in range(tb // RANK_CHUNK):
            rows = slice(ci * RANK_CHUNK, (ci + 1) * RANK_CHUNK)
            oh = jnp.where(lane == route_ref[rows, choice:choice + 1], 1.0, 0.0)
            sc[rows, :] = run + _dot(lower, oh.astype(BF16))
            run = run + jnp.sum(oh, axis=0, keepdims=True)
    cnt = run
    cnt8 = jnp.ceil(cnt * (1.0 / SUBLANES)) * SUBLANES
    ur = lax.broadcasted_iota(jnp.int32, (LANES, LANES), 0)
    uc = lax.broadcasted_iota(jnp.int32, (LANES, LANES), 1)
    upper = jnp.where(ur < uc, 1.0, 0.0)
    off = _dot(jnp.broadcast_to(cnt8, (SUBLANES, LANES)), upper, precision=lax.Precision.HIGHEST)[0:1, :]
    lane_t = lax.broadcasted_iota(jnp.int32, (tb, LANES), 1)
    lane_tf = lane_t.astype(F32)
    p1 = jnp.sum(jnp.where(lane_tf == route_ref[:, 0:1], off + r1_sc[...], 0.0), axis=-1, keepdims=True)
    p2 = jnp.sum(jnp.where(lane_tf == route_ref[:, 1:2], off + r2_sc[...], 0.0), axis=-1, keepdims=True)
    q1 = p1 + (SUBLANES * ROW_PIECES - SUBLANES) * jnp.floor(p1 * (1.0 / SUBLANES))
    q2 = p2 + (SUBLANES * ROW_PIECES - SUBLANES) * jnp.floor(p2 * (1.0 / SUBLANES))
    pos = jnp.where(lane_t == 0, q1, jnp.where(lane_t == 1, q2, 0.0))
    pos_out[...] = jnp.transpose(pos)[0:SUBLANES, :].astype(jnp.int32)
    row8 = lax.broadcasted_iota(jnp.int32, (SUBLANES, LANES), 0)
    meta = jnp.where(row8 == 0, jnp.broadcast_to(off, (SUBLANES, LANES)),
                     jnp.where(row8 == 1, jnp.broadcast_to(cnt, (SUBLANES, LANES)), 0.0))
    meta_out[...] = meta.astype(jnp.int32)


def _rank(route, b, t):
    return pl.pallas_call(
        _rank_kernel,
        grid=(b,),
        in_specs=[pl.BlockSpec((t, LANES), lambda i: (i, 0))],
        out_specs=[pl.BlockSpec((SUBLANES, t), lambda i: (i, 0)), pl.BlockSpec((SUBLANES, LANES), lambda i: (i, 0))],
        out_shape=[jax.ShapeDtypeStruct((b * SUBLANES, t), jnp.int32),
                   jax.ShapeDtypeStruct((b * SUBLANES, LANES), jnp.int32)],
        scratch_shapes=[pltpu.VMEM((t, LANES), F32), pltpu.VMEM((t, LANES), F32)],
        compiler_params=_params("parallel"),
        name="rank",
    )(route)


def _sorted_rows(t):
    return 2 * t + N_EXPERTS * SUBLANES + ROW_TILE


GROUP_ROWS = SUBLANES * ROW_PIECES


def _load_flat_tile(ref, base, rows):
    cols = []
    for c in range(ROW_PIECES):
        cols.append(jnp.concatenate(
            [ref[pl.ds(base + g * GROUP_ROWS + c * SUBLANES, SUBLANES), :] for g in range(rows // SUBLANES)], axis=0))
    return jnp.concatenate(cols, axis=1)


def _store_flat_tile(ref, base, words):
    for c in range(ROW_PIECES):
        for g in range(words.shape[0] // SUBLANES):
            ref[pl.ds(base + g * GROUP_ROWS + c * SUBLANES, SUBLANES), :] = (
                words[g * SUBLANES:(g + 1) * SUBLANES, c * LANES:(c + 1) * LANES])


def _moe_kernel(n_seq, off_ref, cnt_ref, qd1_ref, qd2_ref, qc1_ref, qc2_ref, h2p_ref, wg_ref, wu_ref, wd_ref,
                x1_ref, route_ref, out_ref, xs0, xs1, ys0, ys1, g1, g2):
    o = pl.program_id(0)
    s = pl.program_id(1)
    ts = x1_ref.shape[0]
    half = D_MODEL // 2
    seq_e = jnp.clip(o - 1, 0, n_seq - 1)

    @pl.when((o == 0) & (s == 0))
    def _init():
        for ref in (xs0, xs1, ys0, ys1):
            ref[...] = jnp.zeros(ref.shape, ref.dtype)

    def expert_tile(xs_e, ys_e, k, base, i):
        start = pl.multiple_of(jnp.where(i >= 0, base + i * (ROW_TILE * ROW_PIECES),
                                         xs_e.shape[0] - ROW_TILE * ROW_PIECES), GROUP_ROWS)
        lo, hi = _unpack_words(_load_flat_tile(xs_e, start, ROW_TILE))
        xb = jnp.concatenate([lo.astype(BF16), hi.astype(BF16)], axis=1)
        a = _dot(xb, wg_ref[k])
        mid = (a * _sigmoid(a) * _dot(xb, wu_ref[k])).astype(BF16)
        y = _dot(mid, wd_ref[k])
        _store_flat_tile(ys_e, start, _pack_words(y[:, :half], y[:, half:]))

    def body(xs_d, xs_e, ys_e, ys_c, with_experts):
        @pl.when(s == 0)
        def _clear():
            xs_d[...] = jnp.zeros(xs_d.shape, xs_d.dtype)

        bases, tiles = [], []
        if with_experts:
            for k in range(EXPERTS_PER_STEP):
                idx = seq_e * N_EXPERTS + s * EXPERTS_PER_STEP + k
                bases.append(off_ref[idx] * ROW_PIECES)
                tiles.append(lax.shift_right_logical(cnt_ref[idx] + (ROW_TILE - 1), ROW_TILE.bit_length() - 1))
            for k in range(EXPERTS_PER_STEP):
                expert_tile(xs_e, ys_e, k, bases[k], tiles[k] - 1)

        t0 = s * ts
        for g in range(ts // SUBLANES):
            for r in range(SUBLANES):
                row = h2p_ref[_row_ds(g * GROUP_ROWS + r), :]
                tok = t0 + g * SUBLANES + r
                xs_d[_row_ds(qd1_ref[tok]), :] = row
                xs_d[_row_ds(qd2_ref[tok]), :] = row

        for g in range(ts // SUBLANES):
            for r in range(SUBLANES):
                tok = t0 + g * SUBLANES + r
                g1[_row_ds(g * GROUP_ROWS + r), :] = ys_c[_row_ds(qc1_ref[tok]), :]
                g2[_row_ds(g * GROUP_ROWS + r), :] = ys_c[_row_ds(qc2_ref[tok]), :]
        w1 = route_ref[:, 2:3]
        w2 = route_ref[:, 3:4]
        for c in range(ROW_PIECES):
            tile_of = lambda ref: jnp.concatenate(
                [ref[g * GROUP_ROWS + c * SUBLANES:g * GROUP_ROWS + (c + 1) * SUBLANES, :]
                 for g in range(ts // SUBLANES)], axis=0)
            lo1, hi1 = _unpack_words(tile_of(g1))
            lo2, hi2 = _unpack_words(tile_of(g2))
            cols = slice(c * LANES, (c + 1) * LANES)
            cols_hi = slice(half + c * LANES, half + (c + 1) * LANES)
            out_ref[:, cols] = x1_ref[:, cols] + (w1 * lo1 + w2 * lo2)
            out_ref[:, cols_hi] = x1_ref[:, cols_hi] + (w1 * hi1 + w2 * hi2)

        def more(j, carry):
            for k in range(EXPERTS_PER_STEP):
                expert_tile(xs_e, ys_e, k, bases[k], tiles[k] - 1 - j)
            return carry

        if with_experts:
            lax.fori_loop(1, functools.reduce(jnp.maximum, tiles), more, 0)

    has_experts = (o >= 1) & (o <= n_seq)
    for parity, bufs in ((0, (xs0, xs1, ys1, ys0)), (1, (xs1, xs0, ys0, ys1))):
        for flag in (True, False):
            pl.when((o % 2 == parity) & (has_experts == flag))(functools.partial(body, *bufs, flag))


def _moe(off, cnt, q1, q2, h2p_flat, x1, route, w, b, t):
    flat_rows = _sorted_rows(t) * ROW_PIECES
    n_steps = N_EXPERTS // EXPERTS_PER_STEP
    assert t % (n_steps * SUBLANES) == 0
    ts = t // n_steps
    seq_d = lambda o: jnp.minimum(o, b - 1)
    seq_c = lambda o: jnp.clip(o - 2, 0, b - 1)
    chunk = lambda o, s, *_: (jnp.where(o < 2, 0, seq_c(o) * n_steps + s), 0)
    wmap = lambda o, s, *_: (s, 0, 0)
    smem = lambda f: pl.BlockSpec((t,), lambda o, s, *_: (f(o),), memory_space=pltpu.SMEM)
    grid_spec = pltpu.PrefetchScalarGridSpec(
        num_scalar_prefetch=2,
        grid=(b + 2, n_steps),
        in_specs=[smem(seq_d), smem(seq_d), smem(seq_c), smem(seq_c),
                  pl.BlockSpec((ts * ROW_PIECES, LANES), lambda o, s, *_: (seq_d(o) * n_steps + s, 0)),
                  pl.BlockSpec((EXPERTS_PER_STEP, D_MODEL, EXPERT_FF), wmap),
                  pl.BlockSpec((EXPERTS_PER_STEP, D_MODEL, EXPERT_FF), wmap),
                  pl.BlockSpec((EXPERTS_PER_STEP, EXPERT_FF, D_MODEL), wmap),
                  pl.BlockSpec((ts, D_MODEL), chunk),
                  pl.BlockSpec((ts, LANES), chunk)],
        out_specs=pl.BlockSpec((ts, D_MODEL), chunk),
        scratch_shapes=[pltpu.VMEM((flat_rows, LANES), jnp.uint32)] * 4
        + [pltpu.VMEM((ts * ROW_PIECES, LANES), jnp.uint32)] * 2,
    )
    return pl.pallas_call(
        functools.partial(_moe_kernel, b),
        grid_spec=grid_spec,
        out_shape=jax.ShapeDtypeStruct((b * t, D_MODEL), F32),
        compiler_params=_params("arbitrary", "arbitrary"),
        name="moe",
    )(off, cnt, q1, q2, q1, q2, h2p_flat, w["wg_e"], w["wu_e"], w["wd"], x1, route)


def _head_slots(wmat, width):
    k = wmat.shape[0]
    w3 = wmat.reshape(k, N_HEADS, width)
    return jnp.pad(w3, ((0, 0), (0, 0), (0, HEAD_SLOT - width))).reshape(k, N_HEADS * HEAD_SLOT)


def _gain3(g):
    gp = jnp.pad(g.astype(F32), (0, HEAD_SLOT - QK_DIM))
    return jnp.stack([gp, jnp.roll(gp, ROPE_HALF), jnp.roll(gp, -ROPE_HALF)])


def _prepare_weights(g_mix, w_in, g_q_lora, w_uq, g_kv_lora, w_ukv, g_qk_q, g_qk_k, w_o_mla, w_dw, b_dw,
                     g_conv_ln, b_conv_ln, w_conv_out, b_gates, w_out, g_ffn, w_group_router, b_group_router,
                     w_expert_router, b_expert_router, w_e_gate, w_e_up, w_e_down):
    row = lambda a: a.astype(F32).reshape(1, -1)
    s0, s1, s2, s3 = Q_LORA, Q_LORA + KV_LORA, Q_LORA + KV_LORA + ROPE, Q_LORA + KV_LORA + ROPE + 2 * CONV_W
    w = {}
    w["g_mix"] = row(g_mix)
    w["wa"] = jnp.concatenate([w_in[:, :s1], jnp.pad(w_in[:, s1:s2], ((0, 0), (0, LANES - ROPE)))], axis=1).astype(BF16)
    w["wc"] = w_in[:, s2:s3].astype(BF16)
    w["wg"] = w_in[:, s3:].astype(BF16)
    w["b_gates"] = row(b_gates)
    w["g_q_lora"] = row(g_q_lora)
    w["wuq"] = _head_slots(w_uq, QK_DIM).astype(BF16)
    w["g_kv_lora"] = row(g_kv_lora)
    kv3 = w_ukv.reshape(KV_LORA, N_HEADS, NOPE + V_DIM)
    wk_nope = _head_slots(kv3[:, :, :NOPE].reshape(KV_LORA, N_HEADS * NOPE), NOPE)
    w["wv"] = kv3[:, :, NOPE:].reshape(KV_LORA, N_HEADS * V_DIM).astype(BF16)
    src = jnp.arange(LANES)[:, None]
    dst = jnp.arange(N_HEADS * HEAD_SLOT)[None, :]
    rope_copy = ((src < ROPE) & ((dst % HEAD_SLOT) == (src + NOPE))).astype(F32)
    w["wk"] = jnp.concatenate([wk_nope, rope_copy], axis=0).astype(BF16)
    w["gq3"] = _gain3(g_qk_q)
    w["gk3"] = _gain3(g_qk_k)
    w["wo"] = w_o_mla.astype(BF16)
    w["w_dw"] = jnp.pad(w_dw.astype(F32), ((0, 32 - CONV_K), (0, 0)))
    w["b_dw"] = row(b_dw)
    w["g_conv_ln"] = row(g_conv_ln)
    w["b_conv_ln"] = row(b_conv_ln)
    w["wco"] = w_conv_out.astype(BF16)
    w["wout"] = w_out.astype(BF16)
    w["g_ffn"] = row(g_ffn)
    pad_r = LANES - N_EXPERTS - N_GROUPS
    wr = jnp.concatenate([w_expert_router, w_group_router, jnp.zeros((D_MODEL, pad_r), F32)], axis=1).astype(F32)
    w["wr_hi"] = wr.astype(BF16)
    w["wr_lo"] = (wr - w["wr_hi"].astype(F32)).astype(BF16)
    w["br"] = jnp.concatenate([b_expert_router, b_group_router, jnp.zeros((pad_r,), F32)]).astype(F32).reshape(1, LANES)
    w["wg_e"] = w_e_gate.astype(BF16)
    w["wu_e"] = w_e_up.astype(BF16)
    w["wd"] = w_e_down.astype(BF16)
    return w


def kernel(x, positions, g_mix, w_in, g_q_lora, w_uq, g_kv_lora, w_ukv, g_qk_q, g_qk_k, w_o_mla, w_dw, b_dw,
           g_conv_ln, b_conv_ln, w_conv_out, b_gates, w_out, g_ffn, w_group_router, b_group_router,
           w_expert_router, b_expert_router, w_e_gate, w_e_up, w_e_down):
    b, t, d = x.shape
    assert d == D_MODEL and g_mix.shape[0] == 1
    assert t % ATTN_TILE == 0 and t % RANK_CHUNK == 0 and t % COMBINE_TILE == 0
    tm = min(TOKEN_BLOCK, t)
    assert t % tm == 0 and t % CONV_CHUNK == 0
    n = b * t
    w = _prepare_weights(g_mix[0], w_in[0], g_q_lora[0], w_uq[0], g_kv_lora[0], w_ukv[0], g_qk_q[0], g_qk_k[0],
                         w_o_mla[0], w_dw[0], b_dw[0], g_conv_ln[0], b_conv_ln[0], w_conv_out[0], b_gates[0],
                         w_out[0], g_ffn[0], w_group_router[0], b_group_router[0], w_expert_router[0],
                         b_expert_router[0], w_e_gate[0], w_e_up[0], w_e_down[0])
    x2 = x.reshape(n, d)
    cos, sin = _rope_tables(positions)
    q, k, v, u, gates = _pre_attention(x2, cos, sin, w, min(PROJ_BLOCK, t))
    attn, conv = _attention(q, k, v, u, w, b, t)
    x1, h2p, route = _merge(attn, conv, gates, x2, w, tm)
    pos, meta = _rank(route, b, t)
    pos3 = pos.reshape(b, SUBLANES, t)
    q1 = pos3[:, 0, :].reshape(n)
    q2 = pos3[:, 1, :].reshape(n)
    meta3 = meta.reshape(b, SUBLANES, LANES)
    off = meta3[:, 0, :N_EXPERTS].reshape(b * N_EXPERTS)
    cnt = meta3[:, 1, :N_EXPERTS].reshape(b * N_EXPERTS)
    out = _moe(off, cnt, q1, q2, h2p.reshape(n * ROW_PIECES, LANES), x1, route, w, b, t)
    return out.reshape(b, t, d)
```

```python
import functools

import jax
import jax.numpy as jnp
from jax import lax
from jax.experimental import pallas as pl
from jax.experimental.pallas import tpu as pltpu

D_MODEL = 1024
N_HEADS = 8
NOPE = 64
ROPE = 32
ROPE_HALF = ROPE // 2
QK_DIM = NOPE + ROPE
V_DIM = 64
Q_LORA = 256
KV_LORA = 128
CONV_W = 512
CONV_K = 31
N_GROUPS = 4
EXPERTS_PER_GROUP = 8
N_EXPERTS = N_GROUPS * EXPERTS_PER_GROUP
EXPERT_FF = 256
EPS = 1e-6
ROPE_THETA = 10000.0
CHUNK = 64

LANES = 128
SUBLANES = 8
HEAD_SLOT = LANES
VMEM_LIMIT_BYTES = 56 * 1024 * 1024

TOKEN_BLOCK = 1024
PROJ_BLOCK = 1024
ATTN_TILE = 256
CONV_HALO = 32
ROW_TILE = 256
EXPERTS_PER_STEP = 4
COMBINE_TILE = 256
RANK_CHUNK = 256
SUB_ROWS = 512
CONV_CHUNK = 16

F32 = jnp.float32
BF16 = jnp.bfloat16
NEG_INF = float("-inf")
LOG2E = 1.4426950408889634


def _dot(a, b, **kw):
    return jnp.dot(a, b, preferred_element_type=F32, **kw)


def _rms(x, g):
    return x * lax.rsqrt(jnp.mean(x * x, axis=-1, keepdims=True) + EPS) * g


def _sigmoid(x):
    return 1.0 / (1.0 + jnp.exp(-x))


def _pack_words(lo, hi):
    return pltpu.pack_elementwise([lo, hi], packed_dtype=BF16)


def _unpack_words(wd):
    lo = pltpu.unpack_elementwise(wd, index=0, packed_dtype=BF16, unpacked_dtype=F32)
    hi = pltpu.unpack_elementwise(wd, index=1, packed_dtype=BF16, unpacked_dtype=F32)
    return lo, hi


PACKED_TILE = (D_MODEL // 2 // LANES, SUBLANES, LANES)
ROW_PIECES = PACKED_TILE[0]


def _store_tile_rows(ref, g0, words, lead=()):
    groups = words.shape[0] // SUBLANES
    for c in range(ROW_PIECES):
        piece = words[:, c * LANES:(c + 1) * LANES].reshape(groups, SUBLANES, LANES)
        ref[lead + (pl.ds(g0, groups), c)] = piece


def _row_ds(q):
    return pl.ds(q, ROW_PIECES, stride=SUBLANES)


def _params(*sem):
    return pltpu.CompilerParams(dimension_semantics=sem, vmem_limit_bytes=VMEM_LIMIT_BYTES)


def _full(shape):
    nd = len(shape)
    return pl.BlockSpec(shape, lambda *_: (0,) * nd)


def _rope_table_kernel(pos_ref, freq_ref, cos_ref, sin_ref):
    ang = pos_ref[...].astype(F32) * freq_ref[...]
    cos_ref[...] = jnp.cos(ang)
    sin_ref[...] = jnp.sin(ang)


def _rope_tables(positions):
    n = positions.size
    per_row = LANES // ROPE_HALF
    rows = n // per_row
    pos_rep = jnp.repeat(positions.reshape(rows, per_row), ROPE_HALF, axis=1)
    inv_freq = ROPE_THETA ** (-jnp.arange(ROPE_HALF, dtype=F32) / ROPE_HALF)
    freq = jnp.tile(inv_freq, per_row).reshape(1, LANES)
    rb = min(rows, 512)
    cos, sin = pl.pallas_call(
        _rope_table_kernel,
        grid=(rows // rb,),
        in_specs=[pl.BlockSpec((rb, LANES), lambda i: (i, 0)), _full((1, LANES))],
        out_specs=[pl.BlockSpec((rb, LANES), lambda i: (i, 0))] * 2,
        out_shape=[jax.ShapeDtypeStruct((rows, LANES), F32)] * 2,
        compiler_params=_params("parallel"),
        name="rope_tables",
    )(pos_rep, freq)
    return cos.reshape(n, ROPE_HALF), sin.reshape(n, ROPE_HALF)


def _pre_attention_kernel(x_ref, cos_ref, sin_ref, gmix_ref, wa_ref, wc_ref, wg_ref, bg_ref,
                          gql_ref, wuq_ref, gkv_ref, wk_ref, wv_ref, gq_ref, gk_ref,
                          q_out, k_out, v_out, u_out, gate_out):
    tm = x_ref.shape[0]
    hb = _rms(x_ref[...], gmix_ref[...]).astype(BF16)

    a = _dot(hb, wa_ref[...])
    qdn = _rms(a[:, :Q_LORA], gql_ref[...]).astype(BF16)
    kvn = _rms(a[:, Q_LORA:Q_LORA + KV_LORA], gkv_ref[...]).astype(BF16)
    kr = a[:, Q_LORA + KV_LORA:].astype(BF16)
    qf = _dot(qdn, wuq_ref[...])
    kf = _dot(jnp.concatenate([kvn, kr], axis=1), wk_ref[...])
    v_out[...] = _dot(kvn, wv_ref[...]).astype(BF16)

    cos = cos_ref[...]
    sin = sin_ref[...]
    z = lambda w: jnp.zeros((tm, w), F32)
    c_tab = jnp.concatenate([jnp.ones((tm, NOPE), F32), cos, cos, z(HEAD_SLOT - QK_DIM)], axis=-1)
    s_up = jnp.concatenate([z(NOPE + ROPE_HALF), sin, z(HEAD_SLOT - QK_DIM)], axis=-1)
    s_dn = jnp.concatenate([z(NOPE), -sin, z(HEAD_SLOT - NOPE - ROPE_HALF)], axis=-1)

    def norm_rope(f, g_ref, out, scale):
        ta = c_tab * g_ref[0:1, :]
        tu = s_up * g_ref[1:2, :]
        td = s_dn * g_ref[2:3, :]
        for h in range(N_HEADS):
            xs = f[:, h * HEAD_SLOT:(h + 1) * HEAD_SLOT]
            ss = jnp.sum(xs * xs, axis=-1, keepdims=True)
            s = lax.rsqrt(ss * (1.0 / QK_DIM) + EPS) * scale
            y = xs * ta + pltpu.roll(xs, ROPE_HALF, 1) * tu + pltpu.roll(xs, HEAD_SLOT - ROPE_HALF, 1) * td
            out[:, h * HEAD_SLOT:(h + 1) * HEAD_SLOT] = (y * s).astype(BF16)

    norm_rope(qf, gq_ref, q_out, LOG2E * QK_DIM ** -0.5)
    norm_rope(kf, gk_ref, k_out, 1.0)

    c = _dot(hb, wc_ref[...])
    u_out[...] = c[:, :CONV_W] * _sigmoid(c[:, CONV_W:])
    gate_out[...] = _sigmoid(_dot(hb, wg_ref[...]) + bg_ref[...]).astype(BF16)


def _pre_attention(x2, cos, sin, w, tm):
    n = x2.shape[0]
    row = lambda width: pl.BlockSpec((tm, width), lambda i: (i, 0))
    hs = N_HEADS * HEAD_SLOT
    weights = [w["g_mix"], w["wa"], w["wc"], w["wg"], w["b_gates"], w["g_q_lora"], w["wuq"],
               w["g_kv_lora"], w["wk"], w["wv"], w["gq3"], w["gk3"]]
    return pl.pallas_call(
        _pre_attention_kernel,
        grid=(n // tm,),
        in_specs=[row(D_MODEL), row(ROPE_HALF), row(ROPE_HALF)] + [_full(a.shape) for a in weights],
        out_specs=[row(hs), row(hs), row(N_HEADS * V_DIM), row(CONV_W), row(2 * D_MODEL)],
        out_shape=[jax.ShapeDtypeStruct((n, hs), BF16), jax.ShapeDtypeStruct((n, hs), BF16),
                   jax.ShapeDtypeStruct((n, N_HEADS * V_DIM), BF16), jax.ShapeDtypeStruct((n, CONV_W), F32),
                   jax.ShapeDtypeStruct((n, 2 * D_MODEL), BF16)],
        compiler_params=_params("parallel"),
        name="pre_attention",
    )(x2, cos, sin, *weights)


def _causal_conv_slice(u_ref, wdw_ref, bdw_ref, conv_ref, sh_ref):
    seq = u_ref.shape[0]
    sh_ref[0, 0:CONV_HALO, :] = jnp.zeros((CONV_HALO, LANES), F32)
    sh_ref[0, CONV_HALO:, :] = u_ref[...]
    span = seq + CONV_HALO - SUBLANES
    for s in range(1, SUBLANES):
        sh_ref[s, 0:span, :] = sh_ref[0, s:s + span, :]
    base = CONV_HALO - (CONV_K - 1)
    done = []
    for ck in range(seq // CONV_CHUNK):
        c0 = ck * CONV_CHUNK
        acc = jnp.broadcast_to(bdw_ref[...], (CONV_CHUNK, LANES))
        for j in range(CONV_K):
            s, a = (base + j) % SUBLANES, (base + j) // SUBLANES * SUBLANES
            acc = acc + sh_ref[s, c0 + a:c0 + a + CONV_CHUNK, :] * wdw_ref[j:j + 1, :]
        conv_ref[c0:c0 + CONV_CHUNK, :] = acc
        done.append(acc[0:SUBLANES, :])
    return done


def _ordering_zero(values):
    bits = functools.reduce(jnp.bitwise_or, [lax.bitcast_convert_type(v, jnp.uint32) for v in values])
    half_width = jnp.uint32(16)
    return lax.bitcast_convert_type(
        lax.shift_right_logical(lax.shift_right_logical(bits, half_width), half_width), F32)


def _attention_kernel(q_ref, k_ref, v_ref, u_ref, wdw_ref, bdw_ref, o_ref, conv_ref, kbd, vbd, sh_ref):
    conv_done = _causal_conv_slice(u_ref, wdw_ref, bdw_ref, conv_ref, sh_ref)
    n_q = q_ref.shape[0] // ATTN_TILE
    work = [i + 1 for i in range(n_q)]
    share_end = [len(conv_done) * sum(work[:i + 1]) // sum(work) for i in range(n_q)]

    t = ATTN_TILE
    hk = HEAD_SLOT
    seq = q_ref.shape[0]
    lane = lax.broadcasted_iota(jnp.int32, (hk, HEAD_SLOT), 1)
    zero = jnp.zeros((hk, HEAD_SLOT), BF16)
    for c in range(seq // hk):
        rows = slice(c * hk, (c + 1) * hk)
        top = slice(2 * c * hk, (2 * c + 1) * hk)
        bot = slice((2 * c + 1) * hk, (2 * c + 2) * hk)
        kbd[top, :HEAD_SLOT] = k_ref[rows, :HEAD_SLOT]
        kbd[top, HEAD_SLOT:] = zero
        kbd[bot, :HEAD_SLOT] = zero
        kbd[bot, HEAD_SLOT:] = k_ref[rows, HEAD_SLOT:]
        vp = v_ref[rows, :]
        vbd[top, :] = jnp.where(lane < V_DIM, vp, zero)
        vbd[bot, :] = jnp.where(lane >= V_DIM, vp, zero)

    qrow = lax.broadcasted_iota(jnp.int32, (t, hk), 0) // CHUNK
    kcol = lax.broadcasted_iota(jnp.int32, (t, hk), 1)
    diag_masks = [(kcol + h2 * hk) // CHUNK <= qrow for h2 in range(t // hk)]
    out_lane = lax.broadcasted_iota(jnp.int32, (t, HEAD_SLOT), 1)
    for i in range(seq // t):
        rows = slice(i * t, (i + 1) * t)
        n_tiles = 2 * (i + 1) * t // hk
        s = lax.dot_general(q_ref[rows, :], kbd[0:n_tiles * hk, :], (((1,), (1,)), ((), ())),
                            preferred_element_type=F32)
        tiles = [s[:, j * hk:(j + 1) * hk] for j in range(n_tiles)]
        conv_deps = conv_done[(share_end[i - 1] if i else 0):share_end[i]]
        if conv_deps:
            tiles[0] = tiles[0] + jnp.tile(_ordering_zero(conv_deps), (t // SUBLANES, 1))
        first_diag = n_tiles - 2 * (t // hk)
        for j in range(first_diag, n_tiles):
            tiles[j] = jnp.where(diag_masks[(j - first_diag) // 2], tiles[j], NEG_INF)
        probs = [None] * n_tiles
        inv_l = []
        for h in range(2):
            mine = range(h, n_tiles, 2)
            m = jnp.max(functools.reduce(jnp.maximum, [tiles[j] for j in mine]), axis=-1, keepdims=True)
            for j in mine:
                probs[j] = jnp.exp2(tiles[j] - m)
            l = jnp.sum(functools.reduce(jnp.add, [probs[j] for j in mine]), axis=-1, keepdims=True)
            inv_l.append(1.0 / l)
        p = jnp.concatenate(probs, axis=1).astype(BF16)
        o = _dot(p, vbd[0:n_tiles * hk, :])
        o_ref[rows, :] = (o * jnp.where(out_lane < V_DIM, inv_l[0], inv_l[1])).astype(BF16)


def _attention(q, k, v, u, w, b, t):
    hs2 = 2 * HEAD_SLOT
    assert CONV_W == (N_HEADS // 2) * LANES
    spec = pl.BlockSpec((t, hs2), lambda bi, p: (bi, p))
    lane_tile = pl.BlockSpec((t, LANES), lambda bi, p: (bi, p))
    wslice = lambda rows: pl.BlockSpec((rows, LANES), lambda bi, p: (0, p))
    return pl.pallas_call(
        _attention_kernel,
        grid=(b, N_HEADS // 2),
        in_specs=[spec, spec, lane_tile, lane_tile, wslice(w["w_dw"].shape[0]), wslice(1)],
        out_specs=[lane_tile, lane_tile],
        out_shape=[jax.ShapeDtypeStruct((b * t, N_HEADS * V_DIM), BF16),
                   jax.ShapeDtypeStruct((b * t, CONV_W), F32)],
        scratch_shapes=[pltpu.VMEM((2 * t, hs2), BF16), pltpu.VMEM((2 * t, 2 * V_DIM), BF16),
                        pltpu.VMEM((SUBLANES, t + CONV_HALO, LANES), F32)],
        compiler_params=_params("parallel", "parallel"),
        name="attention",
    )(q, k, v, u, w["w_dw"], w["b_dw"])


def _merge_kernel(attn_ref, conv_ref, gate_ref, x_ref, wo_ref, gln_ref, bln_ref, wco_ref, wout_ref, gffn_ref,
                  wrh_ref, wrl_ref, br_ref, x1_out, h2p_out, route_out):
    tm = x_ref.shape[0]
    half = D_MODEL // 2
    for sb in range(tm // SUB_ROWS):
        rows = slice(sb * SUB_ROWS, (sb + 1) * SUB_ROWS)
        conv = conv_ref[rows, :]
        mu = jnp.mean(conv, axis=-1, keepdims=True)
        cen = conv - mu
        var = jnp.mean(cen * cen, axis=-1, keepdims=True)
        ln = cen * lax.rsqrt(var + EPS) * gln_ref[...] + bln_ref[...]
        ub = (ln * _sigmoid(ln)).astype(BF16)
        y_b = _dot(ub, wco_ref[...])
        y_a = _dot(attn_ref[rows, :], wo_ref[...])
        merged = gate_ref[rows, :D_MODEL].astype(F32) * y_a + gate_ref[rows, D_MODEL:].astype(F32) * y_b
        x1 = x_ref[rows, :] + _dot(merged.astype(BF16), wout_ref[...])
        x1_out[rows, :] = x1
        h2 = _rms(x1, gffn_ref[...])
        _store_tile_rows(h2p_out, sb * (SUB_ROWS // SUBLANES), _pack_words(h2[:, :half], h2[:, half:]))
        h2_hi = h2.astype(BF16)
        h2_lo = (h2 - h2_hi.astype(F32)).astype(BF16)
        logits = (_dot(h2_hi, wrh_ref[...]) + _dot(h2_lo, wrh_ref[...]) + _dot(h2_hi, wrl_ref[...])
                  + br_ref[...])
        route_out[rows, :] = _route(logits)


def _route(logits):
    lane_i = lax.broadcasted_iota(jnp.int32, logits.shape, 1)
    lane = lane_i.astype(F32)
    big = float(LANES)
    is_g = (lane_i >= N_EXPERTS) & (lane_i < N_EXPERTS + N_GROUPS)
    gl = jnp.where(is_g, logits, NEG_INF)
    gmax = jnp.max(gl, axis=-1, keepdims=True)
    gidx = jnp.min(jnp.where(gl == gmax, lane - N_EXPERTS, big), axis=-1, keepdims=True)
    g_p = 1.0 / jnp.sum(jnp.exp(gl - gmax), axis=-1, keepdims=True)
    lane_group = (lane_i // EXPERTS_PER_GROUP).astype(F32)
    valid = (lane_i < N_EXPERTS) & (lane_group == gidx)
    el = jnp.where(valid, logits, NEG_INF)
    ee = jnp.exp(el - jnp.max(el, axis=-1, keepdims=True))
    within = jnp.where(valid, ee / jnp.sum(ee, axis=-1, keepdims=True), -1.0)
    p1 = jnp.max(within, axis=-1, keepdims=True)
    i1 = jnp.min(jnp.where(within == p1, lane, big), axis=-1, keepdims=True)
    within2 = jnp.where(lane == i1, -1.0, within)
    p2 = jnp.max(within2, axis=-1, keepdims=True)
    i2 = jnp.min(jnp.where(within2 == p2, lane, big), axis=-1, keepdims=True)
    psum = p1 + p2
    w1 = g_p * (p1 / psum)
    w2 = g_p * (p2 / psum)
    route = jnp.where(lane_i == 0, i1, 0.0)
    route = jnp.where(lane_i == 1, i2, route)
    route = jnp.where(lane_i == 2, w1, route)
    return jnp.where(lane_i == 3, w2, route)


def _merge(attn, conv, gates, x2, w, tm):
    n = x2.shape[0]
    row = lambda width: pl.BlockSpec((tm, width), lambda i: (i, 0))
    weights = [w["wo"], w["g_conv_ln"], w["b_conv_ln"], w["wco"], w["wout"],
               w["g_ffn"], w["wr_hi"], w["wr_lo"], w["br"]]
    tile_rows = (tm // SUBLANES,) + PACKED_TILE
    return pl.pallas_call(
        _merge_kernel,
        grid=(n // tm,),
        in_specs=[row(N_HEADS * V_DIM), row(CONV_W), row(2 * D_MODEL), row(D_MODEL)]
        + [_full(a.shape) for a in weights],
        out_specs=[row(D_MODEL), pl.BlockSpec(tile_rows, lambda i: (i, 0, 0, 0)), row(LANES)],
        out_shape=[jax.ShapeDtypeStruct((n, D_MODEL), F32),
                   jax.ShapeDtypeStruct((n // SUBLANES,) + PACKED_TILE, jnp.uint32),
                   jax.ShapeDtypeStruct((n, LANES), F32)],
        compiler_params=_params("parallel"),
        name="merge_router",
    )(attn, conv, gates, x2, *weights)


def _rank_kernel(route_ref, pos_out, meta_out, r1_sc, r2_sc):
    tb = route_ref.shape[0]
    lane = lax.broadcasted_iota(jnp.int32, (RANK_CHUNK, LANES), 1).astype(F32)
    rr = lax.broadcasted_iota(jnp.int32, (RANK_CHUNK, RANK_CHUNK), 0)
    cc = lax.broadcasted_iota(jnp.int32, (RANK_CHUNK, RANK_CHUNK), 1)
    lower = jnp.where(rr > cc, 1.0, 0.0).astype(BF16)
    run = jnp.zeros((1, LANES), F32)
    for choice, sc in ((0, r1_sc), (1, r2_sc)):
        for ci in range(tb // RANK_CHUNK):
            rows = slice(ci * RANK_CHUNK, (ci + 1) * RANK_CHUNK)
            oh = jnp.where(lane == route_ref[rows, choice:choice + 1], 1.0, 0.0)
            sc[rows, :] = run + _dot(lower, oh.astype(BF16))
            run = run + jnp.sum(oh, axis=0, keepdims=True)
    cnt = run
    cnt8 = jnp.ceil(cnt * (1.0 / SUBLANES)) * SUBLANES
    ur = lax.broadcasted_iota(jnp.int32, (LANES, LANES), 0)
    uc = lax.broadcasted_iota(jnp.int32, (LANES, LANES), 1)
    upper = jnp.where(ur < uc, 1.0, 0.0)
    off = _dot(jnp.broadcast_to(cnt8, (SUBLANES, LANES)), upper, precision=lax.Precision.HIGHEST)[0:1, :]
    lane_t = lax.broadcasted_iota(jnp.int32, (tb, LANES), 1)
    lane_tf = lane_t.astype(F32)
    p1 = jnp.sum(jnp.where(lane_tf == route_ref[:, 0:1], off + r1_sc[...], 0.0), axis=-1, keepdims=True)
    p2 = jnp.sum(jnp.where(lane_tf == route_ref[:, 1:2], off + r2_sc[...], 0.0), axis=-1, keepdims=True)
    q1 = p1 + (SUBLANES * ROW_PIECES - SUBLANES) * jnp.floor(p1 * (1.0 / SUBLANES))
    q2 = p2 + (SUBLANES * ROW_PIECES - SUBLANES) * jnp.floor(p2 * (1.0 / SUBLANES))
    pos = jnp.where(lane_t == 0, q1, jnp.where(lane_t == 1, q2, 0.0))
    pos_out[...] = jnp.transpose(pos)[0:SUBLANES, :].astype(jnp.int32)
    row8 = lax.broadcasted_iota(jnp.int32, (SUBLANES, LANES), 0)
    meta = jnp.where(row8 == 0, jnp.broadcast_to(off, (SUBLANES, LANES)),
                     jnp.where(row8 == 1, jnp.broadcast_to(cnt, (SUBLANES, LANES)), 0.0))
    meta_out[...] = meta.astype(jnp.int32)


def _rank(route, b, t):
    return pl.pallas_call(
        _rank_kernel,
        grid=(b,),
        in_specs=[pl.BlockSpec((t, LANES), lambda i: (i, 0))],
        out_specs=[pl.BlockSpec((SUBLANES, t), lambda i: (i, 0)), pl.BlockSpec((SUBLANES, LANES), lambda i: (i, 0))],
        out_shape=[jax.ShapeDtypeStruct((b * SUBLANES, t), jnp.int32),
                   jax.ShapeDtypeStruct((b * SUBLANES, LANES), jnp.int32)],
        scratch_shapes=[pltpu.VMEM((t, LANES), F32), pltpu.VMEM((t, LANES), F32)],
        compiler_params=_params("parallel"),
        name="rank",
    )(route)


def _sorted_rows(t):
    return 2 * t + N_EXPERTS * SUBLANES + ROW_TILE


GROUP_ROWS = SUBLANES * ROW_PIECES


def _load_flat_tile(ref, base, rows):
    cols = []
    for c in range(ROW_PIECES):
        cols.append(jnp.concatenate(
            [ref[pl.ds(base + g * GROUP_ROWS + c * SUBLANES, SUBLANES), :] for g in range(rows // SUBLANES)], axis=0))
    return jnp.concatenate(cols, axis=1)


def _store_flat_tile(ref, base, words):
    for c in range(ROW_PIECES):
        for g in range(words.shape[0] // SUBLANES):
            ref[pl.ds(base + g * GROUP_ROWS + c * SUBLANES, SUBLANES), :] = (
                words[g * SUBLANES:(g + 1) * SUBLANES, c * LANES:(c + 1) * LANES])


def _moe_kernel(n_seq, off_ref, cnt_ref, qd1_ref, qd2_ref, qc1_ref, qc2_ref, h2p_ref, wg_ref, wu_ref, wd_ref,
                x1_ref, route_ref, out_ref, xs0, xs1, ys0, ys1, g1, g2):
    o = pl.program_id(0)
    s = pl.program_id(1)
    ts = x1_ref.shape[0]
    half = D_MODEL // 2
    seq_e = jnp.clip(o - 1, 0, n_seq - 1)

    @pl.when((o == 0) & (s == 0))
    def _init():
        for ref in (xs0, xs1, ys0, ys1):
            ref[...] = jnp.zeros(ref.shape, ref.dtype)

    def expert_tile(xs_e, ys_e, k, base, i):
        start = pl.multiple_of(jnp.where(i >= 0, base + i * (ROW_TILE * ROW_PIECES),
                                         xs_e.shape[0] - ROW_TILE * ROW_PIECES), GROUP_ROWS)
        lo, hi = _unpack_words(_load_flat_tile(xs_e, start, ROW_TILE))
        xb = jnp.concatenate([lo.astype(BF16), hi.astype(BF16)], axis=1)
        a = _dot(xb, wg_ref[k])
        mid = (a * _sigmoid(a) * _dot(xb, wu_ref[k])).astype(BF16)
        y = _dot(mid, wd_ref[k])
        _store_flat_tile(ys_e, start, _pack_words(y[:, :half], y[:, half:]))

    def body(xs_d, xs_e, ys_e, ys_c, with_experts):
        @pl.when(s == 0)
        def _clear():
            xs_d[...] = jnp.zeros(xs_d.shape, xs_d.dtype)

        bases, tiles = [], []
        if with_experts:
            for k in range(EXPERTS_PER_STEP):
                idx = seq_e * N_EXPERTS + s * EXPERTS_PER_STEP + k
                bases.append(off_ref[idx] * ROW_PIECES)
                tiles.append(lax.shift_right_logical(cnt_ref[idx] + (ROW_TILE - 1), ROW_TILE.bit_length() - 1))
            for k in range(EXPERTS_PER_STEP):
                expert_tile(xs_e, ys_e, k, bases[k], tiles[k] - 1)

        t0 = s * ts
        for g in range(ts // SUBLANES):
            for r in range(SUBLANES):
                row = h2p_ref[_row_ds(g * GROUP_ROWS + r), :]
                tok = t0 + g * SUBLANES + r
                xs_d[_row_ds(qd1_ref[tok]), :] = row
                xs_d[_row_ds(qd2_ref[tok]), :] = row

        for g in range(ts // SUBLANES):
            for r in range(SUBLANES):
                tok = t0 + g * SUBLANES + r
                g1[_row_ds(g * GROUP_ROWS + r), :] = ys_c[_row_ds(qc1_ref[tok]), :]
                g2[_row_ds(g * GROUP_ROWS + r), :] = ys_c[_row_ds(qc2_ref[tok]), :]
        w1 = route_ref[:, 2:3]
        w2 = route_ref[:, 3:4]
        for c in range(ROW_PIECES):
            tile_of = lambda ref: jnp.concatenate(
                [ref[g * GROUP_ROWS + c * SUBLANES:g * GROUP_ROWS + (c + 1) * SUBLANES, :]
                 for g in range(ts // SUBLANES)], axis=0)
            lo1, hi1 = _unpack_words(tile_of(g1))
            lo2, hi2 = _unpack_words(tile_of(g2))
            cols = slice(c * LANES, (c + 1) * LANES)
            cols_hi = slice(half + c * LANES, half + (c + 1) * LANES)
            out_ref[:, cols] = x1_ref[:, cols] + (w1 * lo1 + w2 * lo2)
            out_ref[:, cols_hi] = x1_ref[:, cols_hi] + (w1 * hi1 + w2 * hi2)

        def more(j, carry):
            for k in range(EXPERTS_PER_STEP):
                expert_tile(xs_e, ys_e, k, bases[k], tiles[k] - 1 - j)
            return carry

        if with_experts:
            lax.fori_loop(1, functools.reduce(jnp.maximum, tiles), more, 0)

    has_experts = (o >= 1) & (o <= n_seq)
    for parity, bufs in ((0, (xs0, xs1, ys1, ys0)), (1, (xs1, xs0, ys0, ys1))):
        for flag in (True, False):
            pl.when((o % 2 == parity) & (has_experts == flag))(functools.partial(body, *bufs, flag))


def _moe(off, cnt, q1, q2, h2p_flat, x1, route, w, b, t):
    flat_rows = _sorted_rows(t) * ROW_PIECES
    n_steps = N_EXPERTS // EXPERTS_PER_STEP
    assert t % (n_steps * SUBLANES) == 0
    ts = t // n_steps
    seq_d = lambda o: jnp.minimum(o, b - 1)
    seq_c = lambda o: jnp.clip(o - 2, 0, b - 1)
    chunk = lambda o, s, *_: (jnp.where(o < 2, 0, seq_c(o) * n_steps + s), 0)
    wmap = lambda o, s, *_: (s, 0, 0)
    smem = lambda f: pl.BlockSpec((t,), lambda o, s, *_: (f(o),), memory_space=pltpu.SMEM)
    grid_spec = pltpu.PrefetchScalarGridSpec(
        num_scalar_prefetch=2,
        grid=(b + 2, n_steps),
        in_specs=[smem(seq_d), smem(seq_d), smem(seq_c), smem(seq_c),
                  pl.BlockSpec((ts * ROW_PIECES, LANES), lambda o, s, *_: (seq_d(o) * n_steps + s, 0)),
                  pl.BlockSpec((EXPERTS_PER_STEP, D_MODEL, EXPERT_FF), wmap),
                  pl.BlockSpec((EXPERTS_PER_STEP, D_MODEL, EXPERT_FF), wmap),
                  pl.BlockSpec((EXPERTS_PER_STEP, EXPERT_FF, D_MODEL), wmap),
                  pl.BlockSpec((ts, D_MODEL), chunk),
                  pl.BlockSpec((ts, LANES), chunk)],
        out_specs=pl.BlockSpec((ts, D_MODEL), chunk),
        scratch_shapes=[pltpu.VMEM((flat_rows, LANES), jnp.uint32)] * 4
        + [pltpu.VMEM((ts * ROW_PIECES, LANES), jnp.uint32)] * 2,
    )
    return pl.pallas_call(
        functools.partial(_moe_kernel, b),
        grid_spec=grid_spec,
        out_shape=jax.ShapeDtypeStruct((b * t, D_MODEL), F32),
        compiler_params=_params("arbitrary", "arbitrary"),
        name="moe",
    )(off, cnt, q1, q2, q1, q2, h2p_flat, w["wg_e"], w["wu_e"], w["wd"], x1, route)


def _head_slots(wmat, width):
    k = wmat.shape[0]
    w3 = wmat.reshape(k, N_HEADS, width)
    return jnp.pad(w3, ((0, 0), (0, 0), (0, HEAD_SLOT - width))).reshape(k, N_HEADS * HEAD_SLOT)


def _gain3(g):
    gp = jnp.pad(g.astype(F32), (0, HEAD_SLOT - QK_DIM))
    return jnp.stack([gp, jnp.roll(gp, ROPE_HALF), jnp.roll(gp, -ROPE_HALF)])


def _prepare_weights(g_mix, w_in, g_q_lora, w_uq, g_kv_lora, w_ukv, g_qk_q, g_qk_k, w_o_mla, w_dw, b_dw,
                     g_conv_ln, b_conv_ln, w_conv_out, b_gates, w_out, g_ffn, w_group_router, b_group_router,
                     w_expert_router, b_expert_router, w_e_gate, w_e_up, w_e_down):
    row = lambda a: a.astype(F32).reshape(1, -1)
    s0, s1, s2, s3 = Q_LORA, Q_LORA + KV_LORA, Q_LORA + KV_LORA + ROPE, Q_LORA + KV_LORA + ROPE + 2 * CONV_W
    w = {}
    w["g_mix"] = row(g_mix)
    w["wa"] = jnp.concatenate([w_in[:, :s1], jnp.pad(w_in[:, s1:s2], ((0, 0), (0, LANES - ROPE)))], axis=1).astype(BF16)
    w["wc"] = w_in[:, s2:s3].astype(BF16)
    w["wg"] = w_in[:, s3:].astype(BF16)
    w["b_gates"] = row(b_gates)
    w["g_q_lora"] = row(g_q_lora)
    w["wuq"] = _head_slots(w_uq, QK_DIM).astype(BF16)
    w["g_kv_lora"] = row(g_kv_lora)
    kv3 = w_ukv.reshape(KV_LORA, N_HEADS, NOPE + V_DIM)
    wk_nope = _head_slots(kv3[:, :, :NOPE].reshape(KV_LORA, N_HEADS * NOPE), NOPE)
    w["wv"] = kv3[:, :, NOPE:].reshape(KV_LORA, N_HEADS * V_DIM).astype(BF16)
    src = jnp.arange(LANES)[:, None]
    dst = jnp.arange(N_HEADS * HEAD_SLOT)[None, :]
    rope_copy = ((src < ROPE) & ((dst % HEAD_SLOT) == (src + NOPE))).astype(F32)
    w["wk"] = jnp.concatenate([wk_nope, rope_copy], axis=0).astype(BF16)
    w["gq3"] = _gain3(g_qk_q)
    w["gk3"] = _gain3(g_qk_k)
    w["wo"] = w_o_mla.astype(BF16)
    w["w_dw"] = jnp.pad(w_dw.astype(F32), ((0, 32 - CONV_K), (0, 0)))
    w["b_dw"] = row(b_dw)
    w["g_conv_ln"] = row(g_conv_ln)
    w["b_conv_ln"] = row(b_conv_ln)
    w["wco"] = w_conv_out.astype(BF16)
    w["wout"] = w_out.astype(BF16)
    w["g_ffn"] = row(g_ffn)
    pad_r = LANES - N_EXPERTS - N_GROUPS
    wr = jnp.concatenate([w_expert_router, w_group_router, jnp.zeros((D_MODEL, pad_r), F32)], axis=1).astype(F32)
    w["wr_hi"] = wr.astype(BF16)
    w["wr_lo"] = (wr - w["wr_hi"].astype(F32)).astype(BF16)
    w["br"] = jnp.concatenate([b_expert_router, b_group_router, jnp.zeros((pad_r,), F32)]).astype(F32).reshape(1, LANES)
    w["wg_e"] = w_e_gate.astype(BF16)
    w["wu_e"] = w_e_up.astype(BF16)
    w["wd"] = w_e_down.astype(BF16)
    return w


def kernel(x, positions, g_mix, w_in, g_q_lora, w_uq, g_kv_lora, w_ukv, g_qk_q, g_qk_k, w_o_mla, w_dw, b_dw,
           g_conv_ln, b_conv_ln, w_conv_out, b_gates, w_out, g_ffn, w_group_router, b_group_router,
           w_expert_router, b_expert_router, w_e_gate, w_e_up, w_e_down):
    b, t, d = x.shape
    assert d == D_MODEL and g_mix.shape[0] == 1
    assert t % ATTN_TILE == 0 and t % RANK_CHUNK == 0 and t % COMBINE_TILE == 0
    tm = min(TOKEN_BLOCK, t)
    assert t % tm == 0 and t % CONV_CHUNK == 0
    n = b * t
    w = _prepare_weights(g_mix[0], w_in[0], g_q_lora[0], w_uq[0], g_kv_lora[0], w_ukv[0], g_qk_q[0], g_qk_k[0],
                         w_o_mla[0], w_dw[0], b_dw[0], g_conv_ln[0], b_conv_ln[0], w_conv_out[0], b_gates[0],
                         w_out[0], g_ffn[0], w_group_router[0], b_group_router[0], w_expert_router[0],
                         b_expert_router[0], w_e_gate[0], w_e_up[0], w_e_down[0])
    x2 = x.reshape(n, d)
    cos, sin = _rope_tables(positions)
    q, k, v, u, gates = _pre_attention(x2, cos, sin, w, min(PROJ_BLOCK, t))
    attn, conv = _attention(q, k, v, u, w, b, t)
    x1, h2p, route = _merge(attn, conv, gates, x2, w, tm)
    pos, meta = _rank(route, b, t)
    pos3 = pos.reshape(b, SUBLANES, t)
    q1 = pos3[:, 0, :].reshape(n)
    q2 = pos3[:, 1, :].reshape(n)
    meta3 = meta.reshape(b, SUBLANES, LANES)
    off = meta3[:, 0, :N_EXPERTS].reshape(b * N_EXPERTS)
    cnt = meta3[:, 1, :N_EXPERTS].reshape(b * N_EXPERTS)
    out = _moe(off, cnt, q1, q2, h2p.reshape(n * ROW_PIECES, LANES), x1, route, w, b, t)
    return out.reshape(b, t, d)
```
